```python
import jax, jax.numpy as jnp
from jax import lax
import numpy as np

D_MODEL = 2048
BATCH = 2
SEQ = 4096
DEPTH = 1

D_MIX = D_MODEL
D_CONV = D_MIX // 2
CONV_WIDTH = 3
N_HEADS = 8
N_KV_HEADS = 2
HEAD_DIM = (D_MIX - D_CONV) // N_HEADS
KV_REP = N_HEADS // N_KV_HEADS
ROPE_DIM = HEAD_DIM // 4
ROPE_THETA = 500000.0
IDX_HEADS = 16
IDX_DIM = 64
IDX_ROPE_DIM = IDX_DIM // 4
TOPK_MAX = 256
Q_BLOCK = 128
IN_SPLITS = (D_CONV, D_CONV, D_CONV,
             N_HEADS * HEAD_DIM, N_KV_HEADS * HEAD_DIM, N_KV_HEADS * HEAD_DIM,
             IDX_HEADS * IDX_DIM, IDX_DIM, IDX_HEADS)
D_IN = 3 * D_CONV + (N_HEADS + 2 * N_KV_HEADS) * HEAD_DIM + IDX_HEADS * IDX_DIM + IDX_DIM + IDX_HEADS
N_EXPERTS = 32
TOP_K = 4
D_FF = D_MODEL
SWIGLU_LIMIT = 7.0
SWIGLU_ALPHA = 1.702
MOE_BLOCK = 256
EPS = 1e-6

kernel_name = "hymba_conv_dsa_moe_layer"


def rmsnorm(x, g):
    xf = x.astype(jnp.float32)
    y = xf * lax.rsqrt(jnp.mean(xf * xf, axis=-1, keepdims=True) + EPS)
    return (y * g.astype(jnp.float32)).astype(x.dtype)


def partial_rope(x, pos, rot_dim):
    half = rot_dim // 2
    inv_freq = ROPE_THETA ** (-jnp.arange(half, dtype=jnp.float32) / half)
    ang = pos.astype(jnp.float32)[..., None] * inv_freq
    cos = jnp.cos(ang)[:, :, None, :]
    sin = jnp.sin(ang)[:, :, None, :]
    xr = x[..., :rot_dim].astype(jnp.float32)
    x1, x2 = xr[..., :half], xr[..., half:]
    rot = jnp.concatenate([x1 * cos - x2 * sin, x2 * cos + x1 * sin], axis=-1).astype(x.dtype)
    return jnp.concatenate([rot, x[..., rot_dim:]], axis=-1)


def short_conv_mixer(b_gate, c_gate, u, conv_w):
    v = c_gate * u
    y = lax.conv_general_dilated(v, conv_w, window_strides=(1,), padding=[(CONV_WIDTH - 1, 0)],
                                 dimension_numbers=('NWC', 'WIO', 'NWC'),
                                 feature_group_count=D_CONV)
    return b_gate * y


def dsa_attention(q, k, v, q_idx, k_idx, w_idx):
    B, S = q.shape[0], q.shape[1]
    topk = min(TOPK_MAX, S // 4)
    n_blocks = S // Q_BLOCK
    key_pos = jnp.arange(S)
    scale = HEAD_DIM ** -0.5

    def block(i):
        t0 = i * Q_BLOCK
        qb = lax.dynamic_slice_in_dim(q, t0, Q_BLOCK, axis=1)
        qib = lax.dynamic_slice_in_dim(q_idx, t0, Q_BLOCK, axis=1)
        wb = lax.dynamic_slice_in_dim(w_idx, t0, Q_BLOCK, axis=1)
        qpos = t0 + jnp.arange(Q_BLOCK)
        causal = key_pos[None, :] <= qpos[:, None]
        dots = jnp.einsum('bqhd,bsd->bqhs', qib, k_idx).astype(jnp.float32)
        score = jnp.einsum('bqh,bqhs->bqs', wb.astype(jnp.float32), jax.nn.relu(dots))
        score = jnp.where(causal[None], score, -jnp.inf)
        _, sel = lax.top_k(score, topk)
        valid = sel <= qpos[None, :, None]
        k_sel = jax.vmap(lambda kb, ib: kb[ib])(k, sel)
        v_sel = jax.vmap(lambda vb, ib: vb[ib])(v, sel)
        qg = qb.reshape(B, Q_BLOCK, N_KV_HEADS, KV_REP, HEAD_DIM)
        logits = jnp.einsum('bqgrd,bqkgd->bqgrk', qg, k_sel).astype(jnp.float32) * scale
        logits = jnp.where(valid[:, :, None, None, :], logits, -jnp.inf)
        p = jax.nn.softmax(logits, axis=-1).astype(v.dtype)
        o = jnp.einsum('bqgrk,bqkgd->bqgrd', p, v_sel)
        return o.reshape(B, Q_BLOCK, N_HEADS * HEAD_DIM)

    out = lax.map(block, jnp.arange(n_blocks))
    return out.transpose(1, 0, 2, 3).reshape(B, S, N_HEADS * HEAD_DIM)


def hybrid_mixer(h, positions, w_in, conv_w, g_conv_out, g_attn_out, w_out):
    B, S, _ = h.shape
    proj = h @ w_in
    split_at = [int(s) for s in np.cumsum(IN_SPLITS)[:-1]]
    b_g, c_g, u, q, k, v, q_i, k_i, w_i = jnp.split(proj, split_at, axis=-1)
    y_conv = short_conv_mixer(b_g, c_g, u, conv_w)
    q = partial_rope(q.reshape(B, S, N_HEADS, HEAD_DIM), positions, ROPE_DIM)
    k = partial_rope(k.reshape(B, S, N_KV_HEADS, HEAD_DIM), positions, ROPE_DIM)
    v = v.reshape(B, S, N_KV_HEADS, HEAD_DIM)
    q_i = partial_rope(q_i.reshape(B, S, IDX_HEADS, IDX_DIM), positions, IDX_ROPE_DIM)
    k_i = partial_rope(k_i.reshape(B, S, 1, IDX_DIM), positions, IDX_ROPE_DIM)[:, :, 0]
    y_attn = dsa_attention(q, k, v, q_i, k_i, w_i)
    merged = jnp.concatenate([rmsnorm(y_conv, g_conv_out), rmsnorm(y_attn, g_attn_out)], axis=-1)
    return merged @ w_out


def clamped_swiglu(gu):
    gate = jnp.minimum(gu[..., ::2], SWIGLU_LIMIT)
    up = jnp.clip(gu[..., 1::2], -SWIGLU_LIMIT, SWIGLU_LIMIT)
    return gate * jax.nn.sigmoid(SWIGLU_ALPHA * gate) * (up + 1.0)


def moe_ffn(h, w_router, b_router, w_gate_up, b_gate_up, w_down, b_down):
    B, S, D = h.shape
    N = B * S
    NK = N * TOP_K
    xf = h.reshape(N, D)
    logits = (xf @ w_router + b_router).astype(jnp.float32)
    top_logit, top_idx = lax.top_k(logits, TOP_K)
    gates = jax.nn.softmax(top_logit, axis=-1)
    flat_e = top_idx.reshape(NK)
    flat_tok = jnp.arange(NK, dtype=jnp.int32) // TOP_K
    order = jnp.argsort(flat_e)
    sorted_e = flat_e[order]
    sorted_tok = flat_tok[order]
    sorted_gate = gates.reshape(NK)[order].astype(h.dtype)
    counts = jnp.zeros((N_EXPERTS,), jnp.int32).at[flat_e].add(1)
    starts = jnp.cumsum(counts) - counts
    padded = (counts + MOE_BLOCK - 1) // MOE_BLOCK * MOE_BLOCK
    pad_ends = jnp.cumsum(padded)
    pad_starts = pad_ends - padded
    dest = pad_starts[sorted_e] + (jnp.arange(NK, dtype=jnp.int32) - starts[sorted_e])
    n_blocks = -(-NK // MOE_BLOCK) + N_EXPERTS
    P = n_blocks * MOE_BLOCK
    row_tok = jnp.zeros((P,), jnp.int32).at[dest].set(sorted_tok)
    row_valid = jnp.zeros((P,), bool).at[dest].set(True)
    x_rows = jnp.where(row_valid[:, None], xf[row_tok], 0).reshape(n_blocks, MOE_BLOCK, D)
    block_e = jnp.minimum(jnp.searchsorted(pad_ends, jnp.arange(n_blocks) * MOE_BLOCK, side='right'),
                          N_EXPERTS - 1)

    def expert_block(args):
        xb, e = args
        gu = xb @ w_gate_up[e] + b_gate_up[e]
        return clamped_swiglu(gu) @ w_down[e] + b_down[e]

    y_rows = lax.map(expert_block, (x_rows, block_e)).reshape(P, D)
    contrib = y_rows[dest] * sorted_gate[:, None]
    out = jnp.zeros((N, D), h.dtype).at[sorted_tok].add(contrib)
    return out.reshape(B, S, D)


def setup_inputs(seed: int = 0) -> dict:
    key = jax.random.key(seed)
    ks = jax.random.split(key, 24)
    f32 = jnp.float32
    nrm = lambda k, shape, s: jax.random.normal(k, shape, f32) * s
    gain = lambda k, shape: 1.0 + 0.02 * jax.random.normal(k, shape, f32)
    L = DEPTH
    x = jax.random.normal(ks[0], (BATCH, SEQ, D_MODEL), f32)
    c = jax.random.normal(ks[1], (BATCH, D_MODEL), f32)
    offset = jax.random.randint(ks[2], (BATCH, 1), 0, 1024, dtype=jnp.int32)
    positions = offset + jnp.arange(SEQ, dtype=jnp.int32)[None, :]
    return {
        "x": x, "c": c, "positions": positions,
        "w_ada": nrm(ks[3], (L, D_MODEL, 6 * D_MODEL), 0.2 * D_MODEL ** -0.5),
        "b_ada": nrm(ks[4], (L, 6 * D_MODEL), 0.01),
        "g_pre_mix": gain(ks[5], (L, D_MODEL)),
        "g_post_mix": gain(ks[6], (L, D_MODEL)),
        "w_in": nrm(ks[7], (L, D_MODEL, D_IN), D_MODEL ** -0.5),
        "conv_w": nrm(ks[8], (L, CONV_WIDTH, 1, D_CONV), CONV_WIDTH ** -0.5),
        "g_conv_out": gain(ks[9], (L, D_CONV)),
        "g_attn_out": gain(ks[10], (L, N_HEADS * HEAD_DIM)),
        "w_out": nrm(ks[11], (L, D_MIX, D_MODEL), D_MIX ** -0.5),
        "g_pre_ffn": gain(ks[12], (L, D_MODEL)),
        "g_post_ffn": gain(ks[13], (L, D_MODEL)),
        "w_router": nrm(ks[14], (L, D_MODEL, N_EXPERTS), D_MODEL ** -0.5),
        "b_router": nrm(ks[15], (L, N_EXPERTS), 0.01),
        "w_gate_up": nrm(ks[16], (L, N_EXPERTS, D_MODEL, 2 * D_FF), D_MODEL ** -0.5),
        "b_gate_up": nrm(ks[17], (L, N_EXPERTS, 2 * D_FF), 0.01),
        "w_down": nrm(ks[18], (L, N_EXPERTS, D_FF, D_MODEL), D_FF ** -0.5),
        "b_down": nrm(ks[19], (L, N_EXPERTS, D_MODEL), 0.01),
    }


def reference(x, c, positions, w_ada, b_ada, g_pre_mix, g_post_mix, w_in, conv_w, g_conv_out,
              g_attn_out, w_out, g_pre_ffn, g_post_ffn, w_router, b_router, w_gate_up, b_gate_up,
              w_down, b_down):
    for l in range(DEPTH):
        ada = jax.nn.silu(c) @ w_ada[l] + b_ada[l]
        sh1, sc1, gt1, sh2, sc2, gt2 = [a[:, None, :] for a in jnp.split(ada, 6, axis=-1)]
        h = rmsnorm(x, g_pre_mix[l]) * (1.0 + sc1) + sh1
        mix = hybrid_mixer(h, positions, w_in[l], conv_w[l], g_conv_out[l], g_attn_out[l], w_out[l])
        x = x + gt1 * rmsnorm(mix, g_post_mix[l])
        h = rmsnorm(x, g_pre_ffn[l]) * (1.0 + sc2) + sh2
        y = moe_ffn(h, w_router[l], b_router[l], w_gate_up[l], b_gate_up[l], w_down[l], b_down[l])
        x = x + gt2 * rmsnorm(y, g_post_ffn[l])
    return x
```

```python
import functools

import numpy as np
import jax
import jax.numpy as jnp
from jax import lax
from jax.experimental import pallas as pl
from jax.experimental.pallas import tpu as pltpu

F32 = jnp.float32
BF16 = jnp.bfloat16
I32 = jnp.int32

EPS = 1e-6
LANES = 128
D_CONV = 1024
CONV_WIDTH = 3
N_HEADS = 8
N_KV_HEADS = 2
HEAD_DIM = 128
KV_REP = N_HEADS // N_KV_HEADS
ROPE_DIM = HEAD_DIM // 4
ROPE_THETA = 500000.0
IDX_HEADS = 16
IDX_DIM = 64
IDX_ROPE_DIM = IDX_DIM // 4
TOPK_MAX = 256
N_EXPERTS = 32
TOP_K = 4
SWIGLU_LIMIT = 7.0
SWIGLU_ALPHA = 1.702

D_ATT = N_HEADS * HEAD_DIM
D_KV = N_KV_HEADS * HEAD_DIM
D_QI = IDX_HEADS * IDX_DIM

INT_MIN = -2147483648
NEG_BIG = -1e30

ROW_TILE = 256
ATT_TILE = 256
MOE_BLOCK = 512
FF_TILE = 512
GATHER_TILE = 128
DMA_CHUNK = 128
VMEM_LIMIT = 56 * 1024 * 1024


def _params(*sem):
    return pltpu.CompilerParams(dimension_semantics=sem, vmem_limit_bytes=VMEM_LIMIT)


def _rms(x, g):
    return x * lax.rsqrt(jnp.mean(x * x, axis=-1, keepdims=True) + EPS) * g


def _ada_body(c_ref, w_ref, b_ref, o_ref):
    c = c_ref[...]
    s = (c * jax.nn.sigmoid(c)).astype(BF16)
    o_ref[...] = jnp.dot(s, w_ref[...].astype(BF16), preferred_element_type=F32) + b_ref[...]


def _ada(c, w, b):
    bsz, d = c.shape
    n = w.shape[1]
    tn = 1536
    rows = 8
    cp = jnp.zeros((rows, d), F32).at[:bsz].set(c)
    out = pl.pallas_call(
        _ada_body,
        grid=(n // tn,),
        in_specs=[pl.BlockSpec((rows, d), lambda j: (0, 0)),
                  pl.BlockSpec((d, tn), lambda j: (0, j)),
                  pl.BlockSpec((1, tn), lambda j: (0, j))],
        out_specs=pl.BlockSpec((rows, tn), lambda j: (0, j)),
        out_shape=jax.ShapeDtypeStruct((rows, n), F32),
        compiler_params=_params("arbitrary"),
        name="ada",
    )(cp, w, b.reshape(1, n))
    return out[:bsz]


C_BCU = 0
C_Q = 3 * D_CONV
C_K = C_Q + D_ATT
C_V = C_K + D_KV
C_QI = C_V + D_KV
C_KI = C_QI + D_QI
C_WI = C_KI + LANES
C_END = C_WI + LANES


def _rope_tables(pos, inv_freq, width, rot):
    half = rot // 2
    rows = pos.shape[0]
    lane = lax.broadcasted_iota(I32, (rows, LANES), 1) & (width - 1)
    ang = pos * inv_freq
    cos = jnp.cos(ang)
    sin = jnp.sin(ang)
    c = jnp.where(lane < rot, cos, 1.0)
    a = jnp.where((lane >= half) & (lane < rot), sin, 0.0)
    b = jnp.where(lane < half, -sin, 0.0)
    return c, a, b, half


def _rope(x, tabs):
    c, a, b, half = tabs
    return x * c + pltpu.roll(x, half, 1) * a + pltpu.roll(x, LANES - half, 1) * b


def _in_proj_body(tiles_per_batch, x_ref, ada_ref, g_ref, pos_ref, w_ref, cw_ref, gc_ref, fq_ref, fi_ref,
                  yc_ref, q_ref, k_ref, v_ref, qi_ref, ki_ref, wi_ref, vbuf):
    i = pl.program_id(0)
    tm = x_ref.shape[0]
    h = _rms(x_ref[...], g_ref[...]) * (1.0 + ada_ref[1:2, :]) + ada_ref[0:1, :]
    hb = h.astype(BF16)

    def proj(lo, hi):
        return jnp.dot(hb, w_ref[:, lo:hi], preferred_element_type=F32)

    @pl.when(i % tiles_per_batch == 0)
    def _():
        vbuf[0:8, :] = jnp.zeros((8, D_CONV), F32)

    bcu = proj(C_BCU, C_Q)
    v0 = bcu[:, D_CONV:2 * D_CONV] * bcu[:, 2 * D_CONV:]
    vbuf[8:8 + tm, :] = v0
    v1 = vbuf[7:7 + tm, :]
    v2 = vbuf[6:6 + tm, :]
    y = cw_ref[0:1, :] * v2 + cw_ref[1:2, :] * v1 + cw_ref[2:3, :] * v0
    yc_ref[...] = _rms(bcu[:, :D_CONV] * y, gc_ref[...]).astype(BF16)
    vbuf[0:8, :] = vbuf[tm:tm + 8, :]

    pos = pos_ref[...].astype(F32)
    tq = _rope_tables(pos, fq_ref[...], HEAD_DIM, ROPE_DIM)
    ti = _rope_tables(pos, fi_ref[...], IDX_DIM, IDX_ROPE_DIM)
    scale = HEAD_DIM ** -0.5

    qf = proj(C_Q, C_K)
    for hh in range(N_HEADS):
        sl = slice(hh * LANES, (hh + 1) * LANES)
        q_ref[:, sl] = (_rope(qf[:, sl], tq) * scale).astype(BF16)
    kf = proj(C_K, C_V)
    for hh in range(N_KV_HEADS):
        sl = slice(hh * LANES, (hh + 1) * LANES)
        k_ref[:, sl] = _rope(kf[:, sl], tq).astype(BF16)
    v_ref[...] = proj(C_V, C_QI).astype(BF16)
    qif = proj(C_QI, C_KI)
    for hh in range(D_QI // LANES):
        sl = slice(hh * LANES, (hh + 1) * LANES)
        qi_ref[:, sl] = _rope(qif[:, sl], ti).astype(BF16)
    ki_ref[...] = _rope(proj(C_KI, C_WI), ti).astype(BF16)
    wi_ref[...] = proj(C_WI, C_END)


def _in_proj(x2, ada3, g_pre, pos2, w_pack, conv_w, g_conv, seq):
    n, d = x2.shape
    tm = min(ROW_TILE, seq)
    tiles_per_batch = seq // tm
    half_q = ROPE_DIM // 2
    half_i = IDX_ROPE_DIM // 2
    lane = np.arange(LANES)
    fq = jnp.asarray(ROPE_THETA, F32) ** (-jnp.asarray(lane % half_q, F32) / half_q)
    fi = jnp.asarray(ROPE_THETA, F32) ** (-jnp.asarray(lane % half_i, F32) / half_i)
    row = lambda i: (i, 0)
    fixed = lambda i: (0, 0)
    outs = [(D_CONV, BF16), (D_ATT, BF16), (D_KV, BF16), (D_KV, BF16), (D_QI, BF16), (LANES, BF16), (LANES, F32)]
    return pl.pallas_call(
        functools.partial(_in_proj_body, tiles_per_batch),
        grid=(n // tm,),
        in_specs=[pl.BlockSpec((tm, d), row),
                  pl.BlockSpec((None, 6, d), lambda i: (i // tiles_per_batch, 0, 0)),
                  pl.BlockSpec((1, d), fixed),
                  pl.BlockSpec((tm, 1), row),
                  pl.BlockSpec((d, C_END), fixed, pipeline_mode=pl.Buffered(1)),
                  pl.BlockSpec((CONV_WIDTH, D_CONV), fixed),
                  pl.BlockSpec((1, D_CONV), fixed),
                  pl.BlockSpec((1, LANES), fixed),
                  pl.BlockSpec((1, LANES), fixed)],
        out_specs=[pl.BlockSpec((tm, w), row) for w, _ in outs],
        out_shape=[jax.ShapeDtypeStruct((n, w), dt) for w, dt in outs],
        scratch_shapes=[pltpu.VMEM((tm + 8, D_CONV), F32)],
        compiler_params=_params("arbitrary"),
        name="in_proj",
    )(x2, ada3, g_pre, pos2, w_pack, conv_w, g_conv, fq.reshape(1, LANES), fi.reshape(1, LANES))


def _dsa_body(topk, nbits, q_ref, qi_ref, wi_ref, k_ref, v_ref, ki_ref, g_ref, o_ref,
              keys_ref, qm_ref, wb_ref, m_ref, l_ref, acc_ref):
    i = pl.program_id(1)
    tq = q_ref.shape[0]
    tk = tq
    nch = tk // LANES
    nkb = i + 1
    nt = (((1,), (1,)), ((), ()))

    lane = lax.broadcasted_iota(I32, (tq, LANES), 1)
    for p in range(IDX_HEADS // 2):
        qp = qi_ref[:, p * LANES:(p + 1) * LANES].astype(F32)
        qm_ref[(2 * p) * tq:(2 * p + 1) * tq, :] = jnp.where(lane < IDX_DIM, qp, 0.0).astype(BF16)
        qm_ref[(2 * p + 1) * tq:(2 * p + 2) * tq, :] = jnp.where(lane >= IDX_DIM, qp, 0.0).astype(BF16)
    w = wi_ref[...]
    for hh in range(IDX_HEADS):
        wb_ref[hh * tq:(hh + 1) * tq, :] = jnp.broadcast_to(w[:, hh:hh + 1], (tq, LANES))

    row = lax.broadcasted_iota(I32, (tq, tk), 0)
    col = lax.broadcasted_iota(I32, (tq, tk), 1)

    def score_block(kb, carry):
        k0 = pl.multiple_of(kb * tk, tk)
        kib = ki_ref[pl.ds(k0, tk), :]
        acc = jnp.zeros((tq, tk), F32)
        for j in range(IDX_HEADS // 4):
            d = lax.dot_general(qm_ref[4 * j * tq:4 * (j + 1) * tq, :], kib, nt, preferred_element_type=F32)
            wbj = wb_ref[4 * j * tq:4 * (j + 1) * tq, :]
            e = jnp.maximum(d, 0.0) * jnp.concatenate([wbj] * nch, axis=1)
            acc = acc + ((e[0:tq] + e[tq:2 * tq]) + (e[2 * tq:3 * tq] + e[3 * tq:4 * tq]))
        bits = lax.bitcast_convert_type(acc, I32)
        key = jnp.where(bits < 0, bits ^ 0x7FFFFFFF, bits)
        keys_ref[kb] = jnp.where(k0 + col <= i * tq + row, key, INT_MIN)
        return carry

    lax.fori_loop(0, nkb, score_block, 0)

    def count(pred):
        def body(kb, part):
            kk = keys_ref[kb]
            for c in range(nch):
                part = part + jnp.where(pred(kk[:, c * LANES:(c + 1) * LANES], kb * tk + c * LANES), 1.0, 0.0)
            return part
        part = lax.fori_loop(0, nkb, body, jnp.zeros((tq, LANES), F32))
        return jnp.broadcast_to(jnp.sum(part, axis=1, keepdims=True), (tq, LANES))

    kf = float(topk)
    zero = jnp.zeros((tq, LANES), I32)
    t0 = jnp.where(count(lambda kk, _: kk >= zero) >= kf, 0, INT_MIN)

    def value_bit(bi, t):
        cand = t + lax.shift_left(jnp.int32(1), 30 - bi)
        return jnp.where(count(lambda kk, _: kk >= cand) >= kf, cand, t)

    thr = lax.fori_loop(0, 31, value_bit, t0)

    n_ge = count(lambda kk, _: kk >= thr)
    surplus = jnp.where(thr > INT_MIN, n_ge - kf, 0.0)

    @pl.when(jnp.max(surplus) > 0.0)
    def _():
        need = kf - count(lambda kk, _: kk > thr)

        def index_bit(bi, p):
            cand = p + lax.shift_left(jnp.int32(1), nbits - 1 - bi)
            f = count(lambda kk, base: (kk == thr) & (base + lane < cand))
            return jnp.where(f <= need, cand, p)

        pend = lax.fori_loop(0, nbits, index_bit, zero)

        def drop(kb, carry):
            kk = keys_ref[kb]
            parts = []
            for c in range(nch):
                kc = kk[:, c * LANES:(c + 1) * LANES]
                parts.append(jnp.where((kc == thr) & (kb * tk + c * LANES + lane >= pend), INT_MIN, kc))
            keys_ref[kb] = jnp.concatenate(parts, axis=1)
            return carry

        lax.fori_loop(0, nkb, drop, 0)

    tsel = jnp.maximum(thr, INT_MIN + 1)

    m_ref[...] = jnp.full(m_ref.shape, NEG_BIG, F32)
    l_ref[...] = jnp.zeros(l_ref.shape, F32)
    acc_ref[...] = jnp.zeros(acc_ref.shape, F32)

    def attend(kb, carry):
        k0 = pl.multiple_of(kb * tk, tk)
        kk = keys_ref[kb]
        sel = jnp.concatenate([jnp.where(kk[:, c * LANES:(c + 1) * LANES] >= tsel, 0.0, NEG_BIG)
                               for c in range(nch)], axis=1)
        for g in range(N_KV_HEADS):
            kg = k_ref[pl.ds(k0, tk), g * LANES:(g + 1) * LANES]
            vg = v_ref[pl.ds(k0, tk), g * LANES:(g + 1) * LANES]
            qg = jnp.concatenate([q_ref[:, (KV_REP * g + r) * LANES:(KV_REP * g + r + 1) * LANES]
                                  for r in range(KV_REP)], axis=0)
            s = lax.dot_general(qg, kg, nt, preferred_element_type=F32)
            s = jnp.concatenate([s[r * tq:(r + 1) * tq] + sel for r in range(KV_REP)], axis=0)
            m_old = m_ref[g]
            m_new = jnp.maximum(m_old, jnp.max(s, axis=1, keepdims=True))
            alpha = jnp.exp(m_old - m_new)
            p = jnp.exp(s - jnp.concatenate([m_new] * nch, axis=1))
            l_ref[g] = alpha * l_ref[g] + jnp.sum(p, axis=1, keepdims=True)
            acc_ref[g] = alpha * acc_ref[g] + jnp.dot(p.astype(BF16), vg, preferred_element_type=F32)
            m_ref[g] = m_new
        return carry

    lax.fori_loop(0, nkb, attend, 0)

    heads = []
    for g in range(N_KV_HEADS):
        og = acc_ref[g] / l_ref[g]
        heads += [og[r * tq:(r + 1) * tq] for r in range(KV_REP)]
    o_ref[...] = _rms(jnp.concatenate(heads, axis=1), g_ref[...]).astype(BF16)


def _dsa(q, qi, wi, k, v, ki, g_attn):
    bsz, seq, _ = q.shape
    tq = min(ATT_TILE, seq)
    nq = seq // tq
    topk = min(TOPK_MAX, seq // 4)
    nbits = int(seq).bit_length()
    qblk = lambda w: pl.BlockSpec((None, tq, w), lambda b, i: (b, i, 0))
    full = lambda w: pl.BlockSpec((None, seq, w), lambda b, i: (b, 0, 0))
    return pl.pallas_call(
        functools.partial(_dsa_body, topk, nbits),
        grid=(bsz, nq),
        in_specs=[qblk(D_ATT), qblk(D_QI), qblk(LANES), full(D_KV), full(D_KV), full(LANES),
                  pl.BlockSpec((1, D_ATT), lambda b, i: (0, 0))],
        out_specs=qblk(D_ATT),
        out_shape=jax.ShapeDtypeStruct((bsz, seq, D_ATT), BF16),
        scratch_shapes=[pltpu.VMEM((nq, tq, tq), I32),
                        pltpu.VMEM((IDX_HEADS * tq, LANES), BF16),
                        pltpu.VMEM((IDX_HEADS * tq, LANES), F32),
                        pltpu.VMEM((N_KV_HEADS, KV_REP * tq, LANES), F32),
                        pltpu.VMEM((N_KV_HEADS, KV_REP * tq, LANES), F32),
                        pltpu.VMEM((N_KV_HEADS, KV_REP * tq, LANES), F32)],
        compiler_params=_params("arbitrary", "arbitrary"),
        name="dsa",
    )(q, qi, wi, k, v, ki, g_attn)


def _split_bf16(x):
    hi = x.astype(BF16)
    return hi, (x - hi.astype(F32)).astype(BF16)


def _out_proj_body(yc_ref, ya_ref, x_ref, ada_ref, wo_ref, gpm_ref, gpf_ref, wr_ref, br_ref,
                   x1_ref, h2_ref, idx_ref, gate_ref):
    tm = x_ref.shape[0]
    mix = (jnp.dot(yc_ref[...], wo_ref[0:D_CONV, :], preferred_element_type=F32)
           + jnp.dot(ya_ref[...], wo_ref[D_CONV:, :], preferred_element_type=F32))
    x1 = x_ref[...] + ada_ref[2:3, :] * _rms(mix, gpm_ref[...])
    x1_ref[...] = x1
    h2 = _rms(x1, gpf_ref[...]) * (1.0 + ada_ref[4:5, :]) + ada_ref[3:4, :]
    h2_ref[...] = h2

    h_hi, h_lo = _split_bf16(h2)
    w_hi, w_lo = _split_bf16(wr_ref[...])
    logits = (jnp.dot(h_hi, w_hi, preferred_element_type=F32) + jnp.dot(h_hi, w_lo, preferred_element_type=F32)
              + jnp.dot(h_lo, w_hi, preferred_element_type=F32)) + br_ref[...]
    lane = lax.broadcasted_iota(I32, (tm, LANES), 1).astype(F32)
    cur = jnp.where(lane < N_EXPERTS, logits, -jnp.inf)
    vals, idxs = [], []
    for _ in range(TOP_K):
        m = jnp.max(cur, axis=1, keepdims=True)
        am = jnp.min(jnp.where(cur == m, lane, float(LANES)), axis=1, keepdims=True)
        vals.append(m)
        idxs.append(am)
        cur = jnp.where(lane == am, -jnp.inf, cur)
    es = [jnp.exp(vv - vals[0]) for vv in vals]
    tot = es[0] + es[1] + es[2] + es[3]
    idx_out = jnp.zeros((tm, LANES), I32)
    gate_out = jnp.zeros((tm, LANES), F32)
    for kk in range(TOP_K):
        idx_out = jnp.where(lane == kk, idxs[kk].astype(I32), idx_out)
        gate_out = jnp.where(lane == kk, es[kk] / tot, gate_out)
    idx_ref[...] = idx_out
    gate_ref[...] = gate_out


def _out_proj(yc, ya, x2, ada3, w_out, g_post_mix, g_pre_ffn, w_router, b_router, seq):
    n, d = x2.shape
    tm = min(ROW_TILE, seq)
    tiles_per_batch = seq // tm
    row = lambda i: (i, 0)
    fixed = lambda i: (0, 0)
    return pl.pallas_call(
        _out_proj_body,
        grid=(n // tm,),
        in_specs=[pl.BlockSpec((tm, D_CONV), row),
                  pl.BlockSpec((tm, D_ATT), row),
                  pl.BlockSpec((tm, d), row),
                  pl.BlockSpec((None, 6, d), lambda i: (i // tiles_per_batch, 0, 0)),
                  pl.BlockSpec((D_CONV + D_ATT, d), fixed, pipeline_mode=pl.Buffered(1)),
                  pl.BlockSpec((1, d), fixed),
                  pl.BlockSpec((1, d), fixed),
                  pl.BlockSpec((d, LANES), fixed),
                  pl.BlockSpec((1, LANES), fixed)],
        out_specs=[pl.BlockSpec((tm, d), row), pl.BlockSpec((tm, d), row),
                   pl.BlockSpec((tm, LANES), row), pl.BlockSpec((tm, LANES), row)],
        out_shape=[jax.ShapeDtypeStruct((n, d), F32), jax.ShapeDtypeStruct((n, d), F32),
                   jax.ShapeDtypeStruct((n, LANES), I32), jax.ShapeDtypeStruct((n, LANES), F32)],
        compiler_params=_params("arbitrary"),
        name="out_proj",
    )(yc, ya, x2, ada3, w_out, g_post_mix, g_pre_ffn, w_router, b_router)


def _rank_body(idx_ref, rank_ref, cnt_ref, carry):
    i = pl.program_id(0)
    tm = idx_ref.shape[0]

    @pl.when(i == 0)
    def _():
        carry[...] = jnp.zeros(carry.shape, F32)

    lane = lax.broadcasted_iota(I32, (tm, LANES), 1)
    r = lax.broadcasted_iota(I32, (tm, tm), 0)
    c = lax.broadcasted_iota(I32, (tm, tm), 1)
    before = jnp.where(c < r, 1.0, 0.0).astype(BF16)
    idx = idx_ref[...]
    base = carry[0:1, :]
    out = jnp.zeros((tm, LANES), F32)
    for kk in range(TOP_K):
        onehot = jnp.where(lane == idx[:, kk:kk + 1], 1.0, 0.0)
        prefix = jnp.dot(before, onehot.astype(BF16), preferred_element_type=F32) + base
        rk = jnp.sum(onehot * prefix, axis=1, keepdims=True)
        out = jnp.where(lane == kk, rk, out)
        base = base + jnp.sum(onehot, axis=0, keepdims=True)
    rank_ref[...] = out.astype(I32)
    carry[0:1, :] = base
    cnt_ref[...] = jnp.broadcast_to(base, cnt_ref.shape).astype(I32)


def _rank(top_idx):
    n = top_idx.shape[0]
    tm = min(ROW_TILE, n)
    return pl.pallas_call(
        _rank_body,
        grid=(n // tm,),
        in_specs=[pl.BlockSpec((tm, LANES), lambda i: (i, 0))],
        out_specs=[pl.BlockSpec((tm, LANES), lambda i: (i, 0)), pl.BlockSpec((8, LANES), lambda i: (0, 0))],
        out_shape=[jax.ShapeDtypeStruct((n, LANES), I32), jax.ShapeDtypeStruct((8, LANES), I32)],
        scratch_shapes=[pltpu.VMEM((8, LANES), F32)],
        compiler_params=_params("arbitrary"),
        name="rank",
    )(top_idx)


def _dispatch_body(dest_ref, zstart_ref, h_ref, o_ref, zeros_ref, sem, zsem):
    npairs = dest_ref.shape[0]
    blk = zeros_ref.shape[0]

    zeros_ref[...] = jnp.zeros(zeros_ref.shape, F32)

    def zcopy(e):
        start = pl.multiple_of(jnp.maximum(zstart_ref[e], 0), blk)
        return pltpu.make_async_copy(zeros_ref, o_ref.at[pl.ds(start, blk)], zsem)

    for e in range(N_EXPERTS):
        @pl.when(zstart_ref[e] >= 0)
        def _():
            zcopy(e).start()
    for e in range(N_EXPERTS):
        @pl.when(zstart_ref[e] >= 0)
        def _():
            zcopy(e).wait()

    def row_copy(p):
        tok = lax.shift_right_logical(p, 2)
        return pltpu.make_async_copy(h_ref.at[pl.ds(tok, 1)], o_ref.at[pl.ds(dest_ref[p], 1)], sem)

    nchunks = npairs // DMA_CHUNK

    def issue(ci):
        for j in range(DMA_CHUNK):
            row_copy(ci * DMA_CHUNK + j).start()

    def drain(ci):
        for j in range(DMA_CHUNK):
            row_copy(ci * DMA_CHUNK + j).wait()

    issue(0)

    def step(ci, carry):
        issue(ci)
        drain(ci - 1)
        return carry

    lax.fori_loop(1, nchunks, step, 0)
    drain(nchunks - 1)


def _dispatch(dest_flat, zstart, h2, n_rows):
    n, d = h2.shape
    return pl.pallas_call(
        _dispatch_body,
        grid_spec=pltpu.PrefetchScalarGridSpec(
            num_scalar_prefetch=2,
            grid=(1,),
            in_specs=[pl.BlockSpec(memory_space=pl.ANY)],
            out_specs=pl.BlockSpec(memory_space=pl.ANY),
            scratch_shapes=[pltpu.VMEM((MOE_BLOCK, d), F32), pltpu.SemaphoreType.DMA, pltpu.SemaphoreType.DMA]),
        out_shape=jax.ShapeDtypeStruct((n_rows, d), F32),
        compiler_params=_params("arbitrary"),
        name="dispatch",
    )(dest_flat, zstart, h2)


def _experts_body(be_ref, nu_ref, x_ref, wg_ref, wu_ref, wd_ref, bg_ref, bu_ref, bd_ref, y_ref, xb_ref):
    j = pl.program_id(0)
    f = pl.program_id(1)

    @pl.when(j < nu_ref[0])
    def _():
        @pl.when(f == 0)
        def _():
            xb_ref[...] = x_ref[...].astype(BF16)
            y_ref[...] = jnp.broadcast_to(bd_ref[...], y_ref.shape)

        xb = xb_ref[...]
        g = jnp.dot(xb, wg_ref[...], preferred_element_type=F32) + bg_ref[...]
        u = jnp.dot(xb, wu_ref[...], preferred_element_type=F32) + bu_ref[...]
        gate = jnp.minimum(g, SWIGLU_LIMIT)
        up = jnp.clip(u, -SWIGLU_LIMIT, SWIGLU_LIMIT)
        act = gate * jax.nn.sigmoid(SWIGLU_ALPHA * gate) * (up + 1.0)
        y_ref[...] += jnp.dot(act.astype(BF16), wd_ref[...], preferred_element_type=F32)


def _experts(block_e, n_used, x_rows, wg, wu, wd, bg, bu, bd):
    p, d = x_rows.shape
    dff = wg.shape[2]
    nb = p // MOE_BLOCK
    nf = dff // FF_TILE

    def blk(j, f, be, nu):
        return jnp.minimum(j, nu[0] - 1)

    def fidx(j, f, be, nu):
        return jnp.where(j < nu[0], f, nf - 1)

    return pl.pallas_call(
        _experts_body,
        grid_spec=pltpu.PrefetchScalarGridSpec(
            num_scalar_prefetch=2,
            grid=(nb, nf),
            in_specs=[pl.BlockSpec((MOE_BLOCK, d), lambda j, f, be, nu: (blk(j, f, be, nu), 0)),
                      pl.BlockSpec((None, d, FF_TILE), lambda j, f, be, nu: (be[j], 0, fidx(j, f, be, nu))),
                      pl.BlockSpec((None, d, FF_TILE), lambda j, f, be, nu: (be[j], 0, fidx(j, f, be, nu))),
                      pl.BlockSpec((None, FF_TILE, d), lambda j, f, be, nu: (be[j], fidx(j, f, be, nu), 0)),
                      pl.BlockSpec((None, 1, FF_TILE), lambda j, f, be, nu: (be[j], 0, fidx(j, f, be, nu))),
                      pl.BlockSpec((None, 1, FF_TILE), lambda j, f, be, nu: (be[j], 0, fidx(j, f, be, nu))),
                      pl.BlockSpec((None, 1, d), lambda j, f, be, nu: (be[j], 0, 0))],
            out_specs=pl.BlockSpec((MOE_BLOCK, d), lambda j, f, be, nu: (blk(j, f, be, nu), 0)),
            scratch_shapes=[pltpu.VMEM((MOE_BLOCK, d), BF16)]),
        out_shape=jax.ShapeDtypeStruct((p, d), F32),
        compiler_params=_params("arbitrary", "arbitrary"),
        name="experts",
    )(block_e, n_used, x_rows, wg, wu, wd, bg, bu, bd)


def _combine_body(tiles_per_batch, dest_ref, y_ref, gate_ref, x1_ref, ada_ref, g_ref, o_ref, buf, sem):
    i = pl.program_id(0)
    n_tiles = pl.num_programs(0)
    tm = x1_ref.shape[0]

    def row_copy(tile, slot, r, kk):
        src = dest_ref[(tile * tm + r) * TOP_K + kk]
        return pltpu.make_async_copy(y_ref.at[pl.ds(src, 1)], buf.at[slot, kk, pl.ds(r, 1)], sem.at[slot])

    def issue(tile, slot):
        def body(r, carry):
            for kk in range(TOP_K):
                row_copy(tile, slot, r, kk).start()
            return carry
        lax.fori_loop(0, tm, body, 0)

    def drain(tile, slot):
        def body(r, carry):
            for kk in range(TOP_K):
                row_copy(tile, slot, r, kk).wait()
            return carry
        lax.fori_loop(0, tm, body, 0)

    slot = i % 2

    @pl.when(i == 0)
    def _():
        issue(0, 0)

    @pl.when(i + 1 < n_tiles)
    def _():
        issue(i + 1, 1 - slot)

    drain(i, slot)
    gates = gate_ref[...]
    y = jnp.zeros(x1_ref.shape, F32)
    for kk in range(TOP_K):
        y = y + gates[:, kk:kk + 1] * buf[slot, kk]
    o_ref[...] = x1_ref[...] + ada_ref[5:6, :] * _rms(y, g_ref[...])


def _combine(dest_flat, y_rows, gates, x1, ada3, g_post_ffn, seq):
    n, d = x1.shape
    tm = min(GATHER_TILE, seq)
    tiles_per_batch = seq // tm
    return pl.pallas_call(
        functools.partial(_combine_body, tiles_per_batch),
        grid_spec=pltpu.PrefetchScalarGridSpec(
            num_scalar_prefetch=1,
            grid=(n // tm,),
            in_specs=[pl.BlockSpec(memory_space=pl.ANY),
                      pl.BlockSpec((tm, LANES), lambda i, dst: (i, 0)),
                      pl.BlockSpec((tm, d), lambda i, dst: (i, 0)),
                      pl.BlockSpec((None, 6, d), lambda i, dst: (i // tiles_per_batch, 0, 0)),
                      pl.BlockSpec((1, d), lambda i, dst: (0, 0))],
            out_specs=pl.BlockSpec((tm, d), lambda i, dst: (i, 0)),
            scratch_shapes=[pltpu.VMEM((2, TOP_K, tm, d), F32), pltpu.SemaphoreType.DMA((2,))]),
        out_shape=jax.ShapeDtypeStruct((n, d), F32),
        compiler_params=_params("arbitrary"),
        name="combine",
    )(dest_flat, y_rows, gates, x1, ada3, g_post_ffn)


def _pack_w_in(w_in):
    d = w_in.shape[0]
    o_ki = C_QI + D_QI
    o_wi = o_ki + IDX_DIM
    ki = w_in[:, o_ki:o_wi]
    wi = w_in[:, o_wi:o_wi + IDX_HEADS]
    pad = jnp.zeros((d, LANES - IDX_HEADS), w_in.dtype)
    return jnp.concatenate([w_in[:, :o_ki], ki, ki, wi, pad], axis=1).astype(BF16)


def _layer(x, c, positions, w_ada, b_ada, g_pre_mix, g_post_mix, w_in, conv_w, g_conv_out, g_attn_out, w_out,
           g_pre_ffn, g_post_ffn, w_router, b_router, w_gate_up, b_gate_up, w_down, b_down):
    bsz, seq, d = x.shape
    n = bsz * seq
    x2 = x.reshape(n, d)
    ada3 = _ada(c, w_ada, b_ada).reshape(bsz, 6, d)

    yc, q, k, v, qi, ki, wi = _in_proj(
        x2, ada3, g_pre_mix.reshape(1, d), positions.reshape(n, 1).astype(I32), _pack_w_in(w_in),
        conv_w.reshape(CONV_WIDTH, D_CONV), g_conv_out.reshape(1, D_CONV), seq)
    b3 = lambda a: a.reshape(bsz, seq, a.shape[-1])
    ya = _dsa(b3(q), b3(qi), b3(wi), b3(k), b3(v), b3(ki), g_attn_out.reshape(1, D_ATT)).reshape(n, D_ATT)

    wr = jnp.zeros((d, LANES), F32).at[:, :N_EXPERTS].set(w_router)
    br = jnp.zeros((1, LANES), F32).at[0, :N_EXPERTS].set(b_router)
    x1, h2, top_idx, gates = _out_proj(yc, ya, x2, ada3, w_out.astype(BF16), g_post_mix.reshape(1, d),
                                       g_pre_ffn.reshape(1, d), wr, br, seq)

    rank, cnt = _rank(top_idx)
    counts = cnt[0, :N_EXPERTS]
    padded = (counts + MOE_BLOCK - 1) // MOE_BLOCK * MOE_BLOCK
    pad_ends = jnp.cumsum(padded)
    pad_starts = pad_ends - padded
    nb = -(-(n * TOP_K) // MOE_BLOCK) + N_EXPERTS
    dest = (pad_starts[top_idx[:, :TOP_K]] + rank[:, :TOP_K]).reshape(n * TOP_K).astype(I32)
    block_e = jnp.minimum(jnp.searchsorted(pad_ends, jnp.arange(nb, dtype=I32) * MOE_BLOCK, side='right'),
                          N_EXPERTS - 1).astype(I32)
    n_used = (pad_ends[-1:] // MOE_BLOCK).astype(I32)
    zstart = jnp.where(padded > 0, pad_ends - MOE_BLOCK, -1).astype(I32)

    x_rows = _dispatch(dest, zstart, h2, nb * MOE_BLOCK)
    y_rows = _experts(block_e, n_used, x_rows,
                      w_gate_up[:, :, 0::2].astype(BF16), w_gate_up[:, :, 1::2].astype(BF16), w_down.astype(BF16),
                      b_gate_up[:, None, 0::2], b_gate_up[:, None, 1::2], b_down[:, None, :])
    out = _combine(dest, y_rows, gates, x1, ada3, g_post_ffn.reshape(1, d), seq)
    return out.reshape(bsz, seq, d)


def kernel(x, c, positions, w_ada, b_ada, g_pre_mix, g_post_mix, w_in, conv_w, g_conv_out, g_attn_out, w_out,
           g_pre_ffn, g_post_ffn, w_router, b_router, w_gate_up, b_gate_up, w_down, b_down):
    for l in range(w_ada.shape[0]):
        x = _layer(x, c, positions, w_ada[l], b_ada[l], g_pre_mix[l], g_post_mix[l], w_in[l], conv_w[l],
                   g_conv_out[l], g_attn_out[l], w_out[l], g_pre_ffn[l], g_post_ffn[l], w_router[l], b_router[l],
                   w_gate_up[l], b_gate_up[l], w_down[l], b_down[l])
    return x
```

```python
import functools

import numpy as np
import jax
import jax.numpy as jnp
from jax import lax
from jax.experimental import pallas as pl
from jax.experimental.pallas import tpu as pltpu

F32 = jnp.float32
BF16 = jnp.bfloat16
I32 = jnp.int32

EPS = 1e-6
LANES = 128
D_CONV = 1024
CONV_WIDTH = 3
N_HEADS = 8
N_KV_HEADS = 2
HEAD_DIM = 128
KV_REP = N_HEADS // N_KV_HEADS
ROPE_DIM = HEAD_DIM // 4
ROPE_THETA = 500000.0
IDX_HEADS = 16
IDX_DIM = 64
IDX_ROPE_DIM = IDX_DIM // 4
TOPK_MAX = 256
N_EXPERTS = 32
TOP_K = 4
SWIGLU_LIMIT = 7.0
SWIGLU_ALPHA = 1.702

D_ATT = N_HEADS * HEAD_DIM
D_KV = N_KV_HEADS * HEAD_DIM
D_QI = IDX_HEADS * IDX_DIM

INT_MIN = -2147483648
NEG_BIG = -1e30

ROW_TILE = 256
ATT_TILE = 256
MOE_BLOCK = 512
FF_TILE = 512
GATHER_TILE = 128
DMA_UNROLL = 8
VMEM_LIMIT = 56 * 1024 * 1024


def _params(*sem):
    return pltpu.CompilerParams(dimension_semantics=sem, vmem_limit_bytes=VMEM_LIMIT)


def _rms(x, g):
    return x * lax.rsqrt(jnp.mean(x * x, axis=-1, keepdims=True) + EPS) * g


def _ada_body(c_ref, w_ref, b_ref, o_ref):
    c = c_ref[...]
    s = (c * jax.nn.sigmoid(c)).astype(BF16)
    o_ref[...] = jnp.dot(s, w_ref[...].astype(BF16), preferred_element_type=F32) + b_ref[...]


def _ada(c, w, b):
    bsz, d = c.shape
    n = w.shape[1]
    tn = 1536
    rows = 8
    cp = jnp.zeros((rows, d), F32).at[:bsz].set(c)
    out = pl.pallas_call(
        _ada_body,
        grid=(n // tn,),
        in_specs=[pl.BlockSpec((rows, d), lambda j: (0, 0)),
                  pl.BlockSpec((d, tn), lambda j: (0, j)),
                  pl.BlockSpec((1, tn), lambda j: (0, j))],
        out_specs=pl.BlockSpec((rows, tn), lambda j: (0, j)),
        out_shape=jax.ShapeDtypeStruct((rows, n), F32),
        compiler_params=_params("arbitrary"),
        name="ada",
    )(cp, w, b.reshape(1, n))
    return out[:bsz]


C_BCU = 0
C_Q = 3 * D_CONV
C_K = C_Q + D_ATT
C_V = C_K + D_KV
C_QI = C_V + D_KV
C_KI = C_QI + D_QI
C_WI = C_KI + LANES
C_END = C_WI + LANES


def _rope_tables(pos, inv_freq, width, rot):
    half = rot // 2
    rows = pos.shape[0]
    lane = lax.broadcasted_iota(I32, (rows, LANES), 1) & (width - 1)
    ang = pos * inv_freq
    cos = jnp.cos(ang)
    sin = jnp.sin(ang)
    c = jnp.where(lane < rot, cos, 1.0)
    a = jnp.where((lane >= half) & (lane < rot), sin, 0.0)
    b = jnp.where(lane < half, -sin, 0.0)
    return c, a, b, half


def _rope(x, tabs):
    c, a, b, half = tabs
    return x * c + pltpu.roll(x, half, 1) * a + pltpu.roll(x, LANES - half, 1) * b


def _in_proj_body(tiles_per_batch, x_ref, ada_ref, g_ref, pos_ref, w_ref, cw_ref, gc_ref, fq_ref, fi_ref,
                  yc_ref, q_ref, k_ref, v_ref, qi_ref, ki_ref, wi_ref, vbuf):
    i = pl.program_id(0)
    tm = x_ref.shape[0]
    h = _rms(x_ref[...], g_ref[...]) * (1.0 + ada_ref[1:2, :]) + ada_ref[0:1, :]
    hb = h.astype(BF16)

    def proj(lo, hi):
        return jnp.dot(hb, w_ref[:, lo:hi], preferred_element_type=F32)

    @pl.when(i % tiles_per_batch == 0)
    def _():
        vbuf[0:8, :] = jnp.zeros((8, D_CONV), F32)

    bcu = proj(C_BCU, C_Q)
    v0 = bcu[:, D_CONV:2 * D_CONV] * bcu[:, 2 * D_CONV:]
    vbuf[8:8 + tm, :] = v0
    v1 = vbuf[7:7 + tm, :]
    v2 = vbuf[6:6 + tm, :]
    y = cw_ref[0:1, :] * v2 + cw_ref[1:2, :] * v1 + cw_ref[2:3, :] * v0
    yc_ref[...] = _rms(bcu[:, :D_CONV] * y, gc_ref[...]).astype(BF16)
    vbuf[0:8, :] = vbuf[tm:tm + 8, :]

    pos = pos_ref[...].astype(F32)
    tq = _rope_tables(pos, fq_ref[...], HEAD_DIM, ROPE_DIM)
    ti = _rope_tables(pos, fi_ref[...], IDX_DIM, IDX_ROPE_DIM)
    scale = HEAD_DIM ** -0.5

    qf = proj(C_Q, C_K)
    for hh in range(N_HEADS):
        sl = slice(hh * LANES, (hh + 1) * LANES)
        q_ref[:, sl] = (_rope(qf[:, sl], tq) * scale).astype(BF16)
    kf = proj(C_K, C_V)
    for hh in range(N_KV_HEADS):
        sl = slice(hh * LANES, (hh + 1) * LANES)
        k_ref[:, sl] = _rope(kf[:, sl], tq).astype(BF16)
    v_ref[...] = proj(C_V, C_QI).astype(BF16)
    qif = proj(C_QI, C_KI)
    for hh in range(D_QI // LANES):
        sl = slice(hh * LANES, (hh + 1) * LANES)
        qi_ref[:, sl] = _rope(qif[:, sl], ti).astype(BF16)
    ki_ref[...] = _rope(proj(C_KI, C_WI), ti).astype(BF16)
    wi_ref[...] = proj(C_WI, C_END)


def _in_proj(x2, ada3, g_pre, pos2, w_pack, conv_w, g_conv, seq):
    n, d = x2.shape
    tm = min(ROW_TILE, seq)
    tiles_per_batch = seq // tm
    half_q = ROPE_DIM // 2
    half_i = IDX_ROPE_DIM // 2
    lane = np.arange(LANES)
    fq = jnp.asarray(ROPE_THETA, F32) ** (-jnp.asarray(lane % half_q, F32) / half_q)
    fi = jnp.asarray(ROPE_THETA, F32) ** (-jnp.asarray(lane % half_i, F32) / half_i)
    row = lambda i: (i, 0)
    fixed = lambda i: (0, 0)
    outs = [(D_CONV, BF16), (D_ATT, BF16), (D_KV, BF16), (D_KV, BF16), (D_QI, BF16), (LANES, BF16), (LANES, F32)]
    return pl.pallas_call(
        functools.partial(_in_proj_body, tiles_per_batch),
        grid=(n // tm,),
        in_specs=[pl.BlockSpec((tm, d), row),
                  pl.BlockSpec((None, 6, d), lambda i: (i // tiles_per_batch, 0, 0)),
                  pl.BlockSpec((1, d), fixed),
                  pl.BlockSpec((tm, 1), row),
                  pl.BlockSpec((d, C_END), fixed, pipeline_mode=pl.Buffered(1)),
                  pl.BlockSpec((CONV_WIDTH, D_CONV), fixed),
                  pl.BlockSpec((1, D_CONV), fixed),
                  pl.BlockSpec((1, LANES), fixed),
                  pl.BlockSpec((1, LANES), fixed)],
        out_specs=[pl.BlockSpec((tm, w), row) for w, _ in outs],
        out_shape=[jax.ShapeDtypeStruct((n, w), dt) for w, dt in outs],
        scratch_shapes=[pltpu.VMEM((tm + 8, D_CONV), F32)],
        compiler_params=_params("arbitrary"),
        name="in_proj",
    )(x2, ada3, g_pre, pos2, w_pack, conv_w, g_conv, fq.reshape(1, LANES), fi.reshape(1, LANES))


def _dsa_body(topk, nbits, q_ref, qi_ref, wi_ref, k_ref, v_ref, ki_ref, g_ref, o_ref,
              keys_ref, qm_ref, wb_ref, m_ref, l_ref, acc_ref):
    i = pl.program_id(1)
    tq = q_ref.shape[0]
    tk = tq
    nch = tk // LANES
    nkb = i + 1
    nt = (((1,), (1,)), ((), ()))

    lane = lax.broadcasted_iota(I32, (tq, LANES), 1)
    for p in range(IDX_HEADS // 2):
        qp = qi_ref[:, p * LANES:(p + 1) * LANES].astype(F32)
        qm_ref[(2 * p) * tq:(2 * p + 1) * tq, :] = jnp.where(lane < IDX_DIM, qp, 0.0).astype(BF16)
        qm_ref[(2 * p + 1) * tq:(2 * p + 2) * tq, :] = jnp.where(lane >= IDX_DIM, qp, 0.0).astype(BF16)
    w = wi_ref[...]
    for hh in range(IDX_HEADS):
        wb_ref[hh * tq:(hh + 1) * tq, :] = jnp.broadcast_to(w[:, hh:hh + 1], (tq, LANES))

    row = lax.broadcasted_iota(I32, (tq, tk), 0)
    col = lax.broadcasted_iota(I32, (tq, tk), 1)

    def score_block(kb, carry):
        k0 = pl.multiple_of(kb * tk, tk)
        kib = ki_ref[pl.ds(k0, tk), :]
        acc = jnp.zeros((tq, tk), F32)
        for j in range(IDX_HEADS // 4):
            d = lax.dot_general(qm_ref[4 * j * tq:4 * (j + 1) * tq, :], kib, nt, preferred_element_type=F32)
            wbj = wb_ref[4 * j * tq:4 * (j + 1) * tq, :]
            e = jnp.maximum(d, 0.0) * jnp.concatenate([wbj] * nch, axis=1)
            acc = acc + ((e[0:tq] + e[tq:2 * tq]) + (e[2 * tq:3 * tq] + e[3 * tq:4 * tq]))
        bits = lax.bitcast_convert_type(acc, I32)
        key = jnp.where(bits < 0, bits ^ 0x7FFFFFFF, bits)
        keys_ref[kb] = jnp.where(k0 + col <= i * tq + row, key, INT_MIN)
        return carry

    lax.fori_loop(0, nkb, score_block, 0)

    def count(pred):
        def body(kb, part):
            kk = keys_ref[kb]
            for c in range(nch):
                part = part + jnp.where(pred(kk[:, c * LANES:(c + 1) * LANES], kb * tk + c * LANES), 1.0, 0.0)
            return part
        part = lax.fori_loop(0, nkb, body, jnp.zeros((tq, LANES), F32))
        return jnp.broadcast_to(jnp.sum(part, axis=1, keepdims=True), (tq, LANES))

    kf = float(topk)
    zero = jnp.zeros((tq, LANES), I32)
    t0 = jnp.where(count(lambda kk, _: kk >= zero) >= kf, 0, INT_MIN)

    def value_bit(bi, t):
        cand = t + lax.shift_left(jnp.int32(1), 30 - bi)
        return jnp.where(count(lambda kk, _: kk >= cand) >= kf, cand, t)

    thr = lax.fori_loop(0, 31, value_bit, t0)

    n_ge = count(lambda kk, _: kk >= thr)
    surplus = jnp.where(thr > INT_MIN, n_ge - kf, 0.0)

    @pl.when(jnp.max(surplus) > 0.0)
    def _():
        need = kf - count(lambda kk, _: kk > thr)

        def index_bit(bi, p):
            cand = p + lax.shift_left(jnp.int32(1), nbits - 1 - bi)
            f = count(lambda kk, base: (kk == thr) & (base + lane < cand))
            return jnp.where(f <= need, cand, p)

        pend = lax.fori_loop(0, nbits, index_bit, zero)

        def drop(kb, carry):
            kk = keys_ref[kb]
            parts = []
            for c in range(nch):
                kc = kk[:, c * LANES:(c + 1) * LANES]
                parts.append(jnp.where((kc == thr) & (kb * tk + c * LANES + lane >= pend), INT_MIN, kc))
            keys_ref[kb] = jnp.concatenate(parts, axis=1)
            return carry

        lax.fori_loop(0, nkb, drop, 0)

    tsel = jnp.maximum(thr, INT_MIN + 1)

    m_ref[...] = jnp.full(m_ref.shape, NEG_BIG, F32)
    l_ref[...] = jnp.zeros(l_ref.shape, F32)
    acc_ref[...] = jnp.zeros(acc_ref.shape, F32)

    def attend(kb, carry):
        k0 = pl.multiple_of(kb * tk, tk)
        kk = keys_ref[kb]
        sel = jnp.concatenate([jnp.where(kk[:, c * LANES:(c + 1) * LANES] >= tsel, 0.0, NEG_BIG)
                               for c in range(nch)], axis=1)
        for g in range(N_KV_HEADS):
            kg = k_ref[pl.ds(k0, tk), g * LANES:(g + 1) * LANES]
            vg = v_ref[pl.ds(k0, tk), g * LANES:(g + 1) * LANES]
            qg = jnp.concatenate([q_ref[:, (KV_REP * g + r) * LANES:(KV_REP * g + r + 1) * LANES]
                                  for r in range(KV_REP)], axis=0)
            s = lax.dot_general(qg, kg, nt, preferred_element_type=F32)
            s = jnp.concatenate([s[r * tq:(r + 1) * tq] + sel for r in range(KV_REP)], axis=0)
            m_old = m_ref[g]
            m_new = jnp.maximum(m_old, jnp.max(s, axis=1, keepdims=True))
            alpha = jnp.exp(m_old - m_new)
            p = jnp.exp(s - jnp.concatenate([m_new] * nch, axis=1))
            l_ref[g] = alpha * l_ref[g] + jnp.sum(p, axis=1, keepdims=True)
            acc_ref[g] = alpha * acc_ref[g] + jnp.dot(p.astype(BF16), vg, preferred_element_type=F32)
            m_ref[g] = m_new
        return carry

    lax.fori_loop(0, nkb, attend, 0)

    heads = []
    for g in range(N_KV_HEADS):
        og = acc_ref[g] / l_ref[g]
        heads += [og[r * tq:(r + 1) * tq] for r in range(KV_REP)]
    o_ref[...] = _rms(jnp.concatenate(heads, axis=1), g_ref[...]).astype(BF16)


def _dsa(q, qi, wi, k, v, ki, g_attn):
    bsz, seq, _ = q.shape
    tq = min(ATT_TILE, seq)
    nq = seq // tq
    topk = min(TOPK_MAX, seq // 4)
    nbits = int(seq).bit_length()
    qblk = lambda w: pl.BlockSpec((None, tq, w), lambda b, i: (b, i, 0))
    full = lambda w: pl.BlockSpec((None, seq, w), lambda b, i: (b, 0, 0))
    return pl.pallas_call(
        functools.partial(_dsa_body, topk, nbits),
        grid=(bsz, nq),
        in_specs=[qblk(D_ATT), qblk(D_QI), qblk(LANES), full(D_KV), full(D_KV), full(LANES),
                  pl.BlockSpec((1, D_ATT), lambda b, i: (0, 0))],
        out_specs=qblk(D_ATT),
        out_shape=jax.ShapeDtypeStruct((bsz, seq, D_ATT), BF16),
        scratch_shapes=[pltpu.VMEM((nq, tq, tq), I32),
                        pltpu.VMEM((IDX_HEADS * tq, LANES), BF16),
                        pltpu.VMEM((IDX_HEADS * tq, LANES), F32),
                        pltpu.VMEM((N_KV_HEADS, KV_REP * tq, LANES), F32),
                        pltpu.VMEM((N_KV_HEADS, KV_REP * tq, LANES), F32),
                        pltpu.VMEM((N_KV_HEADS, KV_REP * tq, LANES), F32)],
        compiler_params=_params("arbitrary", "arbitrary"),
        name="dsa",
    )(q, qi, wi, k, v, ki, g_attn)


def _split_bf16(x):
    hi = x.astype(BF16)
    return hi, (x - hi.astype(F32)).astype(BF16)


def _out_proj_body(yc_ref, ya_ref, x_ref, ada_ref, wo_ref, gpm_ref, gpf_ref, wr_ref, br_ref,
                   x1_ref, h2_ref, idx_ref, gate_ref):
    tm = x_ref.shape[0]
    mix = (jnp.dot(yc_ref[...], wo_ref[0:D_CONV, :], preferred_element_type=F32)
           + jnp.dot(ya_ref[...], wo_ref[D_CONV:, :], preferred_element_type=F32))
    x1 = x_ref[...] + ada_ref[2:3, :] * _rms(mix, gpm_ref[...])
    x1_ref[...] = x1
    h2 = _rms(x1, gpf_ref[...]) * (1.0 + ada_ref[4:5, :]) + ada_ref[3:4, :]
    half = h2.shape[1] // 2
    lo = lax.bitcast_convert_type(h2[:, :half].astype(BF16).astype(F32), jnp.uint32)
    hi = lax.bitcast_convert_type(h2[:, half:].astype(BF16).astype(F32), jnp.uint32)
    h2_ref[...] = hi | lax.shift_right_logical(lo, jnp.uint32(16))

    h_hi, h_lo = _split_bf16(h2)
    w_hi, w_lo = _split_bf16(wr_ref[...])
    logits = (jnp.dot(h_hi, w_hi, preferred_element_type=F32) + jnp.dot(h_hi, w_lo, preferred_element_type=F32)
              + jnp.dot(h_lo, w_hi, preferred_element_type=F32)) + br_ref[...]
    lane = lax.broadcasted_iota(I32, (tm, LANES), 1).astype(F32)
    cur = jnp.where(lane < N_EXPERTS, logits, -jnp.inf)
    vals, idxs = [], []
    for _ in range(TOP_K):
        m = jnp.max(cur, axis=1, keepdims=True)
        am = jnp.min(jnp.where(cur == m, lane, float(LANES)), axis=1, keepdims=True)
        vals.append(m)
        idxs.append(am)
        cur = jnp.where(lane == am, -jnp.inf, cur)
    es = [jnp.exp(vv - vals[0]) for vv in vals]
    tot = es[0] + es[1] + es[2] + es[3]
    idx_out = jnp.zeros((tm, LANES), I32)
    gate_out = jnp.zeros((tm, LANES), F32)
    for kk in range(TOP_K):
        idx_out = jnp.where(lane == kk, idxs[kk].astype(I32), idx_out)
        gate_out = jnp.where(lane == kk, es[kk] / tot, gate_out)
    idx_ref[...] = idx_out
    gate_ref[...] = gate_out


def _out_proj(yc, ya, x2, ada3, w_out, g_post_mix, g_pre_ffn, w_router, b_router, seq):
    n, d = x2.shape
    tm = min(ROW_TILE, seq)
    tiles_per_batch = seq // tm
    row = lambda i: (i, 0)
    fixed = lambda i: (0, 0)
    return pl.pallas_call(
        _out_proj_body,
        grid=(n // tm,),
        in_specs=[pl.BlockSpec((tm, D_CONV), row),
                  pl.BlockSpec((tm, D_ATT), row),
                  pl.BlockSpec((tm, d), row),
                  pl.BlockSpec((None, 6, d), lambda i: (i // tiles_per_batch, 0, 0)),
                  pl.BlockSpec((D_CONV + D_ATT, d), fixed, pipeline_mode=pl.Buffered(1)),
                  pl.BlockSpec((1, d), fixed),
                  pl.BlockSpec((1, d), fixed),
                  pl.BlockSpec((d, LANES), fixed),
                  pl.BlockSpec((1, LANES), fixed)],
        out_specs=[pl.BlockSpec((tm, d), row), pl.BlockSpec((tm, d // 2), row),
                   pl.BlockSpec((tm, LANES), row), pl.BlockSpec((tm, LANES), row)],
        out_shape=[jax.ShapeDtypeStruct((n, d), F32), jax.ShapeDtypeStruct((n, d // 2), jnp.uint32),
                   jax.ShapeDtypeStruct((n, LANES), I32), jax.ShapeDtypeStruct((n, LANES), F32)],
        compiler_params=_params("arbitrary"),
        name="out_proj",
    )(yc, ya, x2, ada3, w_out, g_post_mix, g_pre_ffn, w_router, b_router)


def _rank_body(idx_ref, rank_ref, cnt_ref, carry):
    i = pl.program_id(0)
    tm = idx_ref.shape[0]

    @pl.when(i == 0)
    def _():
        carry[...] = jnp.zeros(carry.shape, F32)

    lane = lax.broadcasted_iota(I32, (tm, LANES), 1)
    r = lax.broadcasted_iota(I32, (tm, tm), 0)
    c = lax.broadcasted_iota(I32, (tm, tm), 1)
    before = jnp.where(c < r, 1.0, 0.0).astype(BF16)
    idx = idx_ref[...]
    base = carry[0:1, :]
    out = jnp.zeros((tm, LANES), F32)
    for kk in range(TOP_K):
        onehot = jnp.where(lane == idx[:, kk:kk + 1], 1.0, 0.0)
        prefix = jnp.dot(before, onehot.astype(BF16), preferred_element_type=F32) + base
        rk = jnp.sum(onehot * prefix, axis=1, keepdims=True)
        out = jnp.where(lane == kk, rk, out)
        base = base + jnp.sum(onehot, axis=0, keepdims=True)
    rank_ref[...] = out.astype(I32)
    carry[0:1, :] = base
    cnt_ref[...] = jnp.broadcast_to(base, cnt_ref.shape).astype(I32)


def _rank(top_idx):
    n = top_idx.shape[0]
    tm = min(ROW_TILE, n)
    return pl.pallas_call(
        _rank_body,
        grid=(n // tm,),
        in_specs=[pl.BlockSpec((tm, LANES), lambda i: (i, 0))],
        out_specs=[pl.BlockSpec((tm, LANES), lambda i: (i, 0)), pl.BlockSpec((8, LANES), lambda i: (0, 0))],
        out_shape=[jax.ShapeDtypeStruct((n, LANES), I32), jax.ShapeDtypeStruct((8, LANES), I32)],
        scratch_shapes=[pltpu.VMEM((8, LANES), F32)],
        compiler_params=_params("arbitrary"),
        name="rank",
    )(top_idx)


def _dispatch_body(dest_ref, zstart_ref, nu_ref, h_ref, o_ref, zeros_ref, sem, zsem):
    i = pl.program_id(0)
    tm = h_ref.shape[0]
    blk = zeros_ref.shape[0]
    nb = o_ref.shape[0] // blk

    @pl.when(i == 0)
    def _():
        zeros_ref[...] = jnp.zeros(zeros_ref.shape, zeros_ref.dtype)

        def zcopy(start):
            return pltpu.make_async_copy(zeros_ref, o_ref.at[pl.ds(pl.multiple_of(start, blk), blk)], zsem)

        def for_blocks(fn):
            for e in range(N_EXPERTS):
                @pl.when(zstart_ref[e] >= 0)
                def _():
                    fn(zcopy(jnp.maximum(zstart_ref[e], 0)))
            for jb in range(nb):
                @pl.when(jb >= nu_ref[0])
                def _():
                    fn(zcopy(jb * blk))

        for_blocks(lambda cp: cp.start())
        for_blocks(lambda cp: cp.wait())

    def row_copy(r, kk):
        dst = dest_ref[(i * tm + r) * TOP_K + kk]
        return pltpu.make_async_copy(h_ref.at[pl.ds(r, 1)], o_ref.at[pl.ds(dst, 1)], sem)

    def for_rows(fn):
        def body(rb, carry):
            for jj in range(DMA_UNROLL):
                for kk in range(TOP_K):
                    fn(row_copy(rb * DMA_UNROLL + jj, kk))
            return carry
        lax.fori_loop(0, tm // DMA_UNROLL, body, 0)

    for_rows(lambda cp: cp.start())
    for_rows(lambda cp: cp.wait())


def _dispatch(dest_flat, zstart, n_used, h2p, n_rows):
    n, dw = h2p.shape
    tm = min(ROW_TILE, n)
    return pl.pallas_call(
        _dispatch_body,
        grid_spec=pltpu.PrefetchScalarGridSpec(
            num_scalar_prefetch=3,
            grid=(n // tm,),
            in_specs=[pl.BlockSpec((tm, dw), lambda i, dst, zs, nu: (i, 0))],
            out_specs=pl.BlockSpec(memory_space=pl.ANY),
            scratch_shapes=[pltpu.VMEM((MOE_BLOCK, dw), h2p.dtype), pltpu.SemaphoreType.DMA,
                            pltpu.SemaphoreType.DMA]),
        out_shape=jax.ShapeDtypeStruct((n_rows, dw), h2p.dtype),
        compiler_params=_params("arbitrary"),
        name="dispatch",
    )(dest_flat, zstart, n_used, h2p)


def _experts_body(be_ref, nu_ref, x_ref, wgu_ref, wd_ref, bgu_ref, bd_ref, y_ref, xlo_ref, xhi_ref):
    j = pl.program_id(0)
    f = pl.program_id(1)
    half = xlo_ref.shape[1]

    @pl.when((j >= nu_ref[0]) & (f == 0))
    def _():
        y_ref[...] = jnp.zeros(y_ref.shape, F32)

    @pl.when(j < nu_ref[0])
    def _():
        @pl.when(f == 0)
        def _():
            w = x_ref[...]
            xlo_ref[...] = lax.bitcast_convert_type(lax.shift_left(w, jnp.uint32(16)), F32).astype(BF16)
            xhi_ref[...] = lax.bitcast_convert_type(w & jnp.uint32(0xFFFF0000), F32).astype(BF16)
            y_ref[...] = jnp.broadcast_to(bd_ref[...], y_ref.shape)

        gu = (jnp.dot(xlo_ref[...], wgu_ref[0:half, :].astype(BF16), preferred_element_type=F32)
              + jnp.dot(xhi_ref[...], wgu_ref[half:, :].astype(BF16), preferred_element_type=F32)) + bgu_ref[...]
        gate = jnp.minimum(gu, SWIGLU_LIMIT)
        glu = gate * jax.nn.sigmoid(SWIGLU_ALPHA * gate)
        up1 = jnp.clip(gu, -SWIGLU_LIMIT, SWIGLU_LIMIT) + 1.0
        n2 = gu.shape[1]
        rows2 = lax.broadcasted_iota(I32, (2 * LANES, LANES), 0)
        cols2 = lax.broadcasted_iota(I32, (2 * LANES, LANES), 1)
        sel = jnp.where(rows2 == 2 * cols2, 1.0, 0.0).astype(BF16)
        parts = []
        for c in range(n2 // LANES):
            sl = slice(c * LANES, (c + 1) * LANES)
            parts.append((glu[:, sl] * pltpu.roll(up1[:, sl], LANES - 1, 1)).astype(BF16))
        acts = [jnp.dot(jnp.concatenate(parts[2 * c:2 * c + 2], axis=1), sel, preferred_element_type=F32)
                for c in range(n2 // (2 * LANES))]
        act = jnp.concatenate(acts, axis=1).astype(BF16)
        y_ref[...] += jnp.dot(act, wd_ref[...].astype(BF16), preferred_element_type=F32)


def _experts(block_e, n_used, x_rows, w_gate_up, w_down, b_gate_up, b_down):
    p, dw = x_rows.shape
    d = 2 * dw
    dff = w_down.shape[1]
    nb = p // MOE_BLOCK
    nf = dff // FF_TILE

    def blk(j, f, be, nu):
        return jnp.minimum(j, nu[0] - 1)

    def fidx(j, f, be, nu):
        return jnp.where(j < nu[0], f, nf - 1)

    return pl.pallas_call(
        _experts_body,
        grid_spec=pltpu.PrefetchScalarGridSpec(
            num_scalar_prefetch=2,
            grid=(nb, nf),
            in_specs=[pl.BlockSpec((MOE_BLOCK, dw), lambda j, f, be, nu: (blk(j, f, be, nu), 0)),
                      pl.BlockSpec((None, d, 2 * FF_TILE), lambda j, f, be, nu: (be[j], 0, fidx(j, f, be, nu))),
                      pl.BlockSpec((None, FF_TILE, d), lambda j, f, be, nu: (be[j], fidx(j, f, be, nu), 0)),
                      pl.BlockSpec((None, 1, 2 * FF_TILE), lambda j, f, be, nu: (be[j], 0, fidx(j, f, be, nu))),
                      pl.BlockSpec((None, 1, d), lambda j, f, be, nu: (be[j], 0, 0))],
            out_specs=pl.BlockSpec((MOE_BLOCK, d), lambda j, f, be, nu: (j, 0)),
            scratch_shapes=[pltpu.VMEM((MOE_BLOCK, dw), BF16), pltpu.VMEM((MOE_BLOCK, dw), BF16)]),
        out_shape=jax.ShapeDtypeStruct((p, d), F32),
        compiler_params=_params("arbitrary", "arbitrary"),
        name="experts",
    )(block_e, n_used, x_rows, w_gate_up, w_down, b_gate_up, b_down)


def _combine_body(tiles_per_batch, dest_ref, y_ref, gate_ref, x1_ref, ada_ref, g_ref, o_ref, buf, sem):
    i = pl.program_id(0)
    n_tiles = pl.num_programs(0)
    tm = x1_ref.shape[0]

    def row_copy(tile, slot, r, kk):
        src = dest_ref[(tile * tm + r) * TOP_K + kk]
        return pltpu.make_async_copy(y_ref.at[pl.ds(src, 1)], buf.at[slot, kk, pl.ds(r, 1)], sem.at[slot])

    def issue(tile, slot):
        def body(r, carry):
            for kk in range(TOP_K):
                row_copy(tile, slot, r, kk).start()
            return carry
        lax.fori_loop(0, tm, body, 0)

    def drain(tile, slot):
        def body(r, carry):
            for kk in range(TOP_K):
                row_copy(tile, slot, r, kk).wait()
            return carry
        lax.fori_loop(0, tm, body, 0)

    slot = i % 2

    @pl.when(i == 0)
    def _():
        issue(0, 0)

    @pl.when(i + 1 < n_tiles)
    def _():
        issue(i + 1, 1 - slot)

    drain(i, slot)
    gates = gate_ref[...]
    y = jnp.zeros(x1_ref.shape, F32)
    for kk in range(TOP_K):
        y = y + gates[:, kk:kk + 1] * buf[slot, kk]
    o_ref[...] = x1_ref[...] + ada_ref[5:6, :] * _rms(y, g_ref[...])


def _combine(dest_flat, y_rows, gates, x1, ada3, g_post_ffn, seq):
    n, d = x1.shape
    tm = min(GATHER_TILE, seq)
    tiles_per_batch = seq // tm
    return pl.pallas_call(
        functools.partial(_combine_body, tiles_per_batch),
        grid_spec=pltpu.PrefetchScalarGridSpec(
            num_scalar_prefetch=1,
            grid=(n // tm,),
            in_specs=[pl.BlockSpec(memory_space=pl.ANY),
                      pl.BlockSpec((tm, LANES), lambda i, dst: (i, 0)),
                      pl.BlockSpec((tm, d), lambda i, dst: (i, 0)),
                      pl.BlockSpec((None, 6, d), lambda i, dst: (i // tiles_per_batch, 0, 0)),
                      pl.BlockSpec((1, d), lambda i, dst: (0, 0))],
            out_specs=pl.BlockSpec((tm, d), lambda i, dst: (i, 0)),
            scratch_shapes=[pltpu.VMEM((2, TOP_K, tm, d), F32), pltpu.SemaphoreType.DMA((2,))]),
        out_shape=jax.ShapeDtypeStruct((n, d), F32),
        compiler_params=_params("arbitrary"),
        name="combine",
    )(dest_flat, y_rows, gates, x1, ada3, g_post_ffn)


def _pack_w_in(w_in):
    d = w_in.shape[0]
    o_ki = C_QI + D_QI
    o_wi = o_ki + IDX_DIM
    ki = w_in[:, o_ki:o_wi]
    wi = w_in[:, o_wi:o_wi + IDX_HEADS]
    pad = jnp.zeros((d, LANES - IDX_HEADS), w_in.dtype)
    return jnp.concatenate([w_in[:, :o_ki], ki, ki, wi, pad], axis=1).astype(BF16)


def _layer(x, c, positions, w_ada, b_ada, g_pre_mix, g_post_mix, w_in, conv_w, g_conv_out, g_attn_out, w_out,
           g_pre_ffn, g_post_ffn, w_router, b_router, w_gate_up, b_gate_up, w_down, b_down):
    bsz, seq, d = x.shape
    n = bsz * seq
    x2 = x.reshape(n, d)
    ada3 = _ada(c, w_ada, b_ada).reshape(bsz, 6, d)

    yc, q, k, v, qi, ki, wi = _in_proj(
        x2, ada3, g_pre_mix.reshape(1, d), positions.reshape(n, 1).astype(I32), _pack_w_in(w_in),
        conv_w.reshape(CONV_WIDTH, D_CONV), g_conv_out.reshape(1, D_CONV), seq)
    b3 = lambda a: a.reshape(bsz, seq, a.shape[-1])
    ya = _dsa(b3(q), b3(qi), b3(wi), b3(k), b3(v), b3(ki), g_attn_out.reshape(1, D_ATT)).reshape(n, D_ATT)

    wr = jnp.zeros((d, LANES), F32).at[:, :N_EXPERTS].set(w_router)
    br = jnp.zeros((1, LANES), F32).at[0, :N_EXPERTS].set(b_router)
    x1, h2, top_idx, gates = _out_proj(yc, ya, x2, ada3, w_out.astype(BF16), g_post_mix.reshape(1, d),
                                       g_pre_ffn.reshape(1, d), wr, br, seq)

    rank, cnt = _rank(top_idx)
    counts = cnt[0, :N_EXPERTS]
    padded = (counts + MOE_BLOCK - 1) // MOE_BLOCK * MOE_BLOCK
    pad_ends = jnp.cumsum(padded)
    pad_starts = pad_ends - padded
    nb = -(-(n * TOP_K) // MOE_BLOCK) + N_EXPERTS
    dest = (pad_starts[top_idx[:, :TOP_K]] + rank[:, :TOP_K]).reshape(n * TOP_K).astype(I32)
    block_e = jnp.minimum(jnp.searchsorted(pad_ends, jnp.arange(nb, dtype=I32) * MOE_BLOCK, side='right'),
                          N_EXPERTS - 1).astype(I32)
    n_used = (pad_ends[-1:] // MOE_BLOCK).astype(I32)
    zstart = jnp.where(padded > 0, pad_ends - MOE_BLOCK, -1).astype(I32)

    x_rows = _dispatch(dest, zstart, n_used, h2, nb * MOE_BLOCK)
    y_rows = _experts(block_e, n_used, x_rows, w_gate_up, w_down, b_gate_up[:, None, :], b_down[:, None, :])
    out = _combine(dest, y_rows, gates, x1, ada3, g_post_ffn.reshape(1, d), seq)
    return out.reshape(bsz, seq, d)


def kernel(x, c, positions, w_ada, b_ada, g_pre_mix, g_post_mix, w_in, conv_w, g_conv_out, g_attn_out, w_out,
           g_pre_ffn, g_post_ffn, w_router, b_router, w_gate_up, b_gate_up, w_down, b_down):
    for l in range(w_ada.shape[0]):
        x = _layer(x, c, positions, w_ada[l], b_ada[l], g_pre_mix[l], g_post_mix[l], w_in[l], conv_w[l],
                   g_conv_out[l], g_attn_out[l], w_out[l], g_pre_ffn[l], g_post_ffn[l], w_router[l], b_router[l],
                   w_gate_up[l], b_gate_up[l], w_down[l], b_down[l])
    return x
```

```python
import functools

import numpy as np
import jax
import jax.numpy as jnp
from jax import lax
from jax.experimental import pallas as pl
from jax.experimental.pallas import tpu as pltpu

F32 = jnp.float32
BF16 = jnp.bfloat16
I32 = jnp.int32

EPS = 1e-6
LANES = 128
D_CONV = 1024
CONV_WIDTH = 3
N_HEADS = 8
N_KV_HEADS = 2
HEAD_DIM = 128
KV_REP = N_HEADS // N_KV_HEADS
ROPE_DIM = HEAD_DIM // 4
ROPE_THETA = 500000.0
IDX_HEADS = 16
IDX_DIM = 64
IDX_ROPE_DIM = IDX_DIM // 4
TOPK_MAX = 256
N_EXPERTS = 32
TOP_K = 4
SWIGLU_LIMIT = 7.0
SWIGLU_ALPHA = 1.702

D_ATT = N_HEADS * HEAD_DIM
D_KV = N_KV_HEADS * HEAD_DIM
D_QI = IDX_HEADS * IDX_DIM

INT_MIN = -2147483648
NEG_BIG = -1e30

ROW_TILE = 256
ATT_TILE = 256
MOE_BLOCK = 512
MOE_SUB = 256
FF_TILE = 512
GATHER_TILE = 128
DMA_UNROLL = 8
VMEM_LIMIT = 56 * 1024 * 1024


def _params(*sem):
    return pltpu.CompilerParams(dimension_semantics=sem, vmem_limit_bytes=VMEM_LIMIT)


def _rms(x, g):
    return x * lax.rsqrt(jnp.mean(x * x, axis=-1, keepdims=True) + EPS) * g


def _ada_body(c_ref, w_ref, b_ref, o_ref):
    c = c_ref[...]
    s = (c * jax.nn.sigmoid(c)).astype(BF16)
    o_ref[...] = jnp.dot(s, w_ref[...].astype(BF16), preferred_element_type=F32) + b_ref[...]


def _ada(c, w, b):
    bsz, d = c.shape
    n = w.shape[1]
    tn = 1536
    rows = 8
    cp = jnp.zeros((rows, d), F32).at[:bsz].set(c)
    out = pl.pallas_call(
        _ada_body,
        grid=(n // tn,),
        in_specs=[pl.BlockSpec((rows, d), lambda j: (0, 0)),
                  pl.BlockSpec((d, tn), lambda j: (0, j)),
                  pl.BlockSpec((1, tn), lambda j: (0, j))],
        out_specs=pl.BlockSpec((rows, tn), lambda j: (0, j)),
        out_shape=jax.ShapeDtypeStruct((rows, n), F32),
        compiler_params=_params("arbitrary"),
        name="ada",
    )(cp, w, b.reshape(1, n))
    return out[:bsz]


C_BCU = 0
C_Q = 3 * D_CONV
C_K = C_Q + D_ATT
C_V = C_K + D_KV
C_QI = C_V + D_KV
C_KI = C_QI + D_QI
C_WI = C_KI + LANES
C_END = C_WI + LANES


def _rope_tables(pos, inv_freq, width, rot):
    half = rot // 2
    rows = pos.shape[0]
    lane = lax.broadcasted_iota(I32, (rows, LANES), 1) & (width - 1)
    ang = pos * inv_freq
    cos = jnp.cos(ang)
    sin = jnp.sin(ang)
    c = jnp.where(lane < rot, cos, 1.0)
    a = jnp.where((lane >= half) & (lane < rot), sin, 0.0)
    b = jnp.where(lane < half, -sin, 0.0)
    return c, a, b, half


def _rope(x, tabs):
    c, a, b, half = tabs
    return x * c + pltpu.roll(x, half, 1) * a + pltpu.roll(x, LANES - half, 1) * b


def _in_proj_body(tiles_per_batch, x_ref, ada_ref, g_ref, pos_ref, w_ref, cw_ref, gc_ref, fq_ref, fi_ref,
                  yc_ref, q_ref, k_ref, v_ref, qi_ref, ki_ref, wi_ref, vbuf):
    i = pl.program_id(0)
    tm = x_ref.shape[0]
    h = _rms(x_ref[...], g_ref[...]) * (1.0 + ada_ref[1:2, :]) + ada_ref[0:1, :]
    hb = h.astype(BF16)

    def proj(lo, hi):
        return jnp.dot(hb, w_ref[:, lo:hi], preferred_element_type=F32)

    @pl.when(i % tiles_per_batch == 0)
    def _():
        vbuf[0:8, :] = jnp.zeros((8, D_CONV), F32)

    bcu = proj(C_BCU, C_Q)
    v0 = bcu[:, D_CONV:2 * D_CONV] * bcu[:, 2 * D_CONV:]
    vbuf[8:8 + tm, :] = v0
    v1 = vbuf[7:7 + tm, :]
    v2 = vbuf[6:6 + tm, :]
    y = cw_ref[0:1, :] * v2 + cw_ref[1:2, :] * v1 + cw_ref[2:3, :] * v0
    yc_ref[...] = _rms(bcu[:, :D_CONV] * y, gc_ref[...]).astype(BF16)
    vbuf[0:8, :] = vbuf[tm:tm + 8, :]

    pos = pos_ref[...].astype(F32)
    tq = _rope_tables(pos, fq_ref[...], HEAD_DIM, ROPE_DIM)
    ti = _rope_tables(pos, fi_ref[...], IDX_DIM, IDX_ROPE_DIM)
    scale = HEAD_DIM ** -0.5

    qf = proj(C_Q, C_K)
    for hh in range(N_HEADS):
        sl = slice(hh * LANES, (hh + 1) * LANES)
        q_ref[:, sl] = (_rope(qf[:, sl], tq) * scale).astype(BF16)
    kf = proj(C_K, C_V)
    for hh in range(N_KV_HEADS):
        sl = slice(hh * LANES, (hh + 1) * LANES)
        k_ref[:, sl] = _rope(kf[:, sl], tq).astype(BF16)
    v_ref[...] = proj(C_V, C_QI).astype(BF16)
    qif = proj(C_QI, C_KI)
    for hh in range(D_QI // LANES):
        sl = slice(hh * LANES, (hh + 1) * LANES)
        qi_ref[:, sl] = _rope(qif[:, sl], ti).astype(BF16)
    ki_ref[...] = _rope(proj(C_KI, C_WI), ti).astype(BF16)
    wi_ref[...] = proj(C_WI, C_END)


def _in_proj(x2, ada3, g_pre, pos2, w_pack, conv_w, g_conv, seq):
    n, d = x2.shape
    tm = min(ROW_TILE, seq)
    tiles_per_batch = seq // tm
    half_q = ROPE_DIM // 2
    half_i = IDX_ROPE_DIM // 2
    lane = np.arange(LANES)
    fq = jnp.asarray(ROPE_THETA, F32) ** (-jnp.asarray(lane % half_q, F32) / half_q)
    fi = jnp.asarray(ROPE_THETA, F32) ** (-jnp.asarray(lane % half_i, F32) / half_i)
    row = lambda i: (i, 0)
    fixed = lambda i: (0, 0)
    outs = [(D_CONV, BF16), (D_ATT, BF16), (D_KV, BF16), (D_KV, BF16), (D_QI, BF16), (LANES, BF16), (LANES, F32)]
    return pl.pallas_call(
        functools.partial(_in_proj_body, tiles_per_batch),
        grid=(n // tm,),
        in_specs=[pl.BlockSpec((tm, d), row),
                  pl.BlockSpec((None, 6, d), lambda i: (i // tiles_per_batch, 0, 0)),
                  pl.BlockSpec((1, d), fixed),
                  pl.BlockSpec((tm, 1), row),
                  pl.BlockSpec((d, C_END), fixed, pipeline_mode=pl.Buffered(1)),
                  pl.BlockSpec((CONV_WIDTH, D_CONV), fixed),
                  pl.BlockSpec((1, D_CONV), fixed),
                  pl.BlockSpec((1, LANES), fixed),
                  pl.BlockSpec((1, LANES), fixed)],
        out_specs=[pl.BlockSpec((tm, w), row) for w, _ in outs],
        out_shape=[jax.ShapeDtypeStruct((n, w), dt) for w, dt in outs],
        scratch_shapes=[pltpu.VMEM((tm + 8, D_CONV), F32)],
        compiler_params=_params("arbitrary"),
        name="in_proj",
    )(x2, ada3, g_pre, pos2, w_pack, conv_w, g_conv, fq.reshape(1, LANES), fi.reshape(1, LANES))


def _dsa_body(topk, nbits, q_ref, qi_ref, wi_ref, k_ref, v_ref, ki_ref, g_ref, o_ref,
              keys_ref, keys_t_ref, qm_ref, wb_ref, m_ref, l_ref, acc_ref):
    i = pl.program_id(1)
    tq = q_ref.shape[0]
    tk = tq
    nch = tk // LANES
    nkb = i + 1
    nt = (((1,), (1,)), ((), ()))

    lane = lax.broadcasted_iota(I32, (tq, LANES), 1)
    for p in range(IDX_HEADS // 2):
        qp = qi_ref[:, p * LANES:(p + 1) * LANES].astype(F32)
        qm_ref[(2 * p) * tq:(2 * p + 1) * tq, :] = jnp.where(lane < IDX_DIM, qp, 0.0).astype(BF16)
        qm_ref[(2 * p + 1) * tq:(2 * p + 2) * tq, :] = jnp.where(lane >= IDX_DIM, qp, 0.0).astype(BF16)
    w = wi_ref[...]
    for hh in range(IDX_HEADS):
        wb_ref[hh * tq:(hh + 1) * tq, :] = jnp.broadcast_to(w[:, hh:hh + 1], (tq, LANES))

    row = lax.broadcasted_iota(I32, (tq, tk), 0)
    col = lax.broadcasted_iota(I32, (tq, tk), 1)

    def score_block(kb, carry):
        k0 = pl.multiple_of(kb * tk, tk)
        kib = ki_ref[pl.ds(k0, tk), :]
        acc = jnp.zeros((tq, tk), F32)
        for j in range(IDX_HEADS // 4):
            d = lax.dot_general(qm_ref[4 * j * tq:4 * (j + 1) * tq, :], kib, nt, preferred_element_type=F32)
            wbj = wb_ref[4 * j * tq:4 * (j + 1) * tq, :]
            e = jnp.maximum(d, 0.0) * jnp.concatenate([wbj] * nch, axis=1)
            acc = acc + ((e[0:tq] + e[tq:2 * tq]) + (e[2 * tq:3 * tq] + e[3 * tq:4 * tq]))
        bits = lax.bitcast_convert_type(acc, I32)
        key = jnp.where(bits < 0, bits ^ 0x7FFFFFFF, bits)
        key = jnp.where(k0 + col <= i * tq + row, key, INT_MIN)
        keys_ref[kb] = key
        keys_t_ref[kb] = key.T
        return carry

    lax.fori_loop(0, nkb, score_block, 0)

    def count_t(pred):
        def body(kb, part):
            ind = jnp.where(pred(keys_t_ref[kb]), 1.0, 0.0)
            for s0 in range(0, tk, 8):
                part = part + ind[s0:s0 + 8]
            return part
        part = lax.fori_loop(0, nkb, body, jnp.zeros((8, tq), F32))
        return jnp.sum(part, axis=0, keepdims=True)

    kf = float(topk)
    t0 = jnp.where(count_t(lambda kk: kk >= 0) >= kf, 0, INT_MIN)

    def value_bit(bi, t):
        cand = t + lax.shift_left(jnp.int32(1), 30 - bi)
        return jnp.where(count_t(lambda kk: kk >= cand) >= kf, cand, t)

    thr_t = lax.fori_loop(0, 31, value_bit, t0)
    n_ge = count_t(lambda kk: kk >= thr_t)
    surplus = jnp.where(thr_t > INT_MIN, n_ge - kf, 0.0)
    thr = jnp.broadcast_to(thr_t, (tq, tq)).T[:, :LANES]

    def count(pred):
        def body(kb, part):
            kk = keys_ref[kb]
            for c in range(nch):
                part = part + jnp.where(pred(kk[:, c * LANES:(c + 1) * LANES], kb * tk + c * LANES), 1.0, 0.0)
            return part
        part = lax.fori_loop(0, nkb, body, jnp.zeros((tq, LANES), F32))
        return jnp.broadcast_to(jnp.sum(part, axis=1, keepdims=True), (tq, LANES))

    zero = jnp.zeros((tq, LANES), I32)

    @pl.when(jnp.max(surplus) > 0.0)
    def _():
        need = kf - count(lambda kk, _: kk > thr)

        def index_bit(bi, p):
            cand = p + lax.shift_left(jnp.int32(1), nbits - 1 - bi)
            f = count(lambda kk, base: (kk == thr) & (base + lane < cand))
            return jnp.where(f <= need, cand, p)

        pend = lax.fori_loop(0, nbits, index_bit, zero)

        def drop(kb, carry):
            kk = keys_ref[kb]
            parts = []
            for c in range(nch):
                kc = kk[:, c * LANES:(c + 1) * LANES]
                parts.append(jnp.where((kc == thr) & (kb * tk + c * LANES + lane >= pend), INT_MIN, kc))
            keys_ref[kb] = jnp.concatenate(parts, axis=1)
            return carry

        lax.fori_loop(0, nkb, drop, 0)

    tsel = jnp.maximum(thr, INT_MIN + 1)

    m_ref[...] = jnp.full(m_ref.shape, NEG_BIG, F32)
    l_ref[...] = jnp.zeros(l_ref.shape, F32)
    acc_ref[...] = jnp.zeros(acc_ref.shape, F32)

    def attend(kb, carry):
        k0 = pl.multiple_of(kb * tk, tk)
        kk = keys_ref[kb]
        sel = jnp.concatenate([jnp.where(kk[:, c * LANES:(c + 1) * LANES] >= tsel, 0.0, NEG_BIG)
                               for c in range(nch)], axis=1)
        for g in range(N_KV_HEADS):
            kg = k_ref[pl.ds(k0, tk), g * LANES:(g + 1) * LANES]
            vg = v_ref[pl.ds(k0, tk), g * LANES:(g + 1) * LANES]
            qg = jnp.concatenate([q_ref[:, (KV_REP * g + r) * LANES:(KV_REP * g + r + 1) * LANES]
                                  for r in range(KV_REP)], axis=0)
            s = lax.dot_general(qg, kg, nt, preferred_element_type=F32)
            s = jnp.concatenate([s[r * tq:(r + 1) * tq] + sel for r in range(KV_REP)], axis=0)
            m_old = m_ref[g]
            m_new = jnp.maximum(m_old, jnp.max(s, axis=1, keepdims=True))
            alpha = jnp.exp(m_old - m_new)
            p = jnp.exp(s - jnp.concatenate([m_new] * nch, axis=1))
            l_ref[g] = alpha * l_ref[g] + jnp.sum(p, axis=1, keepdims=True)
            acc_ref[g] = alpha * acc_ref[g] + jnp.dot(p.astype(BF16), vg, preferred_element_type=F32)
            m_ref[g] = m_new
        return carry

    lax.fori_loop(0, nkb, attend, 0)

    heads = []
    for g in range(N_KV_HEADS):
        og = acc_ref[g] / l_ref[g]
        heads += [og[r * tq:(r + 1) * tq] for r in range(KV_REP)]
    o_ref[...] = _rms(jnp.concatenate(heads, axis=1), g_ref[...]).astype(BF16)


def _dsa(q, qi, wi, k, v, ki, g_attn):
    bsz, seq, _ = q.shape
    tq = min(ATT_TILE, seq)
    nq = seq // tq
    topk = min(TOPK_MAX, seq // 4)
    nbits = int(seq).bit_length()
    qblk = lambda w: pl.BlockSpec((None, tq, w), lambda b, i: (b, i, 0))
    full = lambda w: pl.BlockSpec((None, seq, w), lambda b, i: (b, 0, 0))
    return pl.pallas_call(
        functools.partial(_dsa_body, topk, nbits),
        grid=(bsz, nq),
        in_specs=[qblk(D_ATT), qblk(D_QI), qblk(LANES), full(D_KV), full(D_KV), full(LANES),
                  pl.BlockSpec((1, D_ATT), lambda b, i: (0, 0))],
        out_specs=qblk(D_ATT),
        out_shape=jax.ShapeDtypeStruct((bsz, seq, D_ATT), BF16),
        scratch_shapes=[pltpu.VMEM((nq, tq, tq), I32),
                        pltpu.VMEM((nq, tq, tq), I32),
                        pltpu.VMEM((IDX_HEADS * tq, LANES), BF16),
                        pltpu.VMEM((IDX_HEADS * tq, LANES), F32),
                        pltpu.VMEM((N_KV_HEADS, KV_REP * tq, LANES), F32),
                        pltpu.VMEM((N_KV_HEADS, KV_REP * tq, LANES), F32),
                        pltpu.VMEM((N_KV_HEADS, KV_REP * tq, LANES), F32)],
        compiler_params=_params("arbitrary", "arbitrary"),
        name="dsa",
    )(q, qi, wi, k, v, ki, g_attn)


def _split_bf16(x):
    hi = x.astype(BF16)
    return hi, (x - hi.astype(F32)).astype(BF16)


def _out_proj_body(yc_ref, ya_ref, x_ref, ada_ref, wo_ref, gpm_ref, gpf_ref, wr_ref, br_ref,
                   x1_ref, h2_ref, idx_ref, gate_ref):
    tm = x_ref.shape[0]
    mix = (jnp.dot(yc_ref[...], wo_ref[0:D_CONV, :], preferred_element_type=F32)
           + jnp.dot(ya_ref[...], wo_ref[D_CONV:, :], preferred_element_type=F32))
    x1 = x_ref[...] + ada_ref[2:3, :] * _rms(mix, gpm_ref[...])
    x1_ref[...] = x1
    h2 = _rms(x1, gpf_ref[...]) * (1.0 + ada_ref[4:5, :]) + ada_ref[3:4, :]
    half = h2.shape[1] // 2
    lo = lax.bitcast_convert_type(h2[:, :half].astype(BF16).astype(F32), jnp.uint32)
    hi = lax.bitcast_convert_type(h2[:, half:].astype(BF16).astype(F32), jnp.uint32)
    h2_ref[...] = hi | lax.shift_right_logical(lo, jnp.uint32(16))

    h_hi, h_lo = _split_bf16(h2)
    w_hi, w_lo = _split_bf16(wr_ref[...])
    logits = (jnp.dot(h_hi, w_hi, preferred_element_type=F32) + jnp.dot(h_hi, w_lo, preferred_element_type=F32)
              + jnp.dot(h_lo, w_hi, preferred_element_type=F32)) + br_ref[...]
    lane = lax.broadcasted_iota(I32, (tm, LANES), 1).astype(F32)
    cur = jnp.where(lane < N_EXPERTS, logits, -jnp.inf)
    vals, idxs = [], []
    for _ in range(TOP_K):
        m = jnp.max(cur, axis=1, keepdims=True)
        am = jnp.min(jnp.where(cur == m, lane, float(LANES)), axis=1, keepdims=True)
        vals.append(m)
        idxs.append(am)
        cur = jnp.where(lane == am, -jnp.inf, cur)
    es = [jnp.exp(vv - vals[0]) for vv in vals]
    tot = es[0] + es[1] + es[2] + es[3]
    idx_out = jnp.zeros((tm, LANES), I32)
    gate_out = jnp.zeros((tm, LANES), F32)
    for kk in range(TOP_K):
        idx_out = jnp.where(lane == kk, idxs[kk].astype(I32), idx_out)
        gate_out = jnp.where(lane == kk, es[kk] / tot, gate_out)
    idx_ref[...] = idx_out
    gate_ref[...] = gate_out


def _out_proj(yc, ya, x2, ada3, w_out, g_post_mix, g_pre_ffn, w_router, b_router, seq):
    n, d = x2.shape
    tm = min(ROW_TILE, seq)
    tiles_per_batch = seq // tm
    row = lambda i: (i, 0)
    fixed = lambda i: (0, 0)
    return pl.pallas_call(
        _out_proj_body,
        grid=(n // tm,),
        in_specs=[pl.BlockSpec((tm, D_CONV), row),
                  pl.BlockSpec((tm, D_ATT), row),
                  pl.BlockSpec((tm, d), row),
                  pl.BlockSpec((None, 6, d), lambda i: (i // tiles_per_batch, 0, 0)),
                  pl.BlockSpec((D_CONV + D_ATT, d), fixed, pipeline_mode=pl.Buffered(1)),
                  pl.BlockSpec((1, d), fixed),
                  pl.BlockSpec((1, d), fixed),
                  pl.BlockSpec((d, LANES), fixed),
                  pl.BlockSpec((1, LANES), fixed)],
        out_specs=[pl.BlockSpec((tm, d), row), pl.BlockSpec((tm, d // 2), row),
                   pl.BlockSpec((tm, LANES), row), pl.BlockSpec((tm, LANES), row)],
        out_shape=[jax.ShapeDtypeStruct((n, d), F32), jax.ShapeDtypeStruct((n, d // 2), jnp.uint32),
                   jax.ShapeDtypeStruct((n, LANES), I32), jax.ShapeDtypeStruct((n, LANES), F32)],
        compiler_params=_params("arbitrary"),
        name="out_proj",
    )(yc, ya, x2, ada3, w_out, g_post_mix, g_pre_ffn, w_router, b_router)


def _rank_body(idx_ref, rank_ref, cnt_ref, carry):
    i = pl.program_id(0)
    tm = idx_ref.shape[0]

    @pl.when(i == 0)
    def _():
        carry[...] = jnp.zeros(carry.shape, F32)

    lane = lax.broadcasted_iota(I32, (tm, LANES), 1)
    r = lax.broadcasted_iota(I32, (tm, tm), 0)
    c = lax.broadcasted_iota(I32, (tm, tm), 1)
    before = jnp.where(c < r, 1.0, 0.0).astype(BF16)
    idx = idx_ref[...]
    base = carry[0:1, :]
    out = jnp.zeros((tm, LANES), F32)
    for kk in range(TOP_K):
        onehot = jnp.where(lane == idx[:, kk:kk + 1], 1.0, 0.0)
        prefix = jnp.dot(before, onehot.astype(BF16), preferred_element_type=F32) + base
        rk = jnp.sum(onehot * prefix, axis=1, keepdims=True)
        out = jnp.where(lane == kk, rk, out)
        base = base + jnp.sum(onehot, axis=0, keepdims=True)
    rank_ref[...] = out.astype(I32)
    carry[0:1, :] = base
    cnt_ref[...] = jnp.broadcast_to(base, cnt_ref.shape).astype(I32)


def _rank(top_idx):
    n = top_idx.shape[0]
    tm = min(ROW_TILE, n)
    return pl.pallas_call(
        _rank_body,
        grid=(n // tm,),
        in_specs=[pl.BlockSpec((tm, LANES), lambda i: (i, 0))],
        out_specs=[pl.BlockSpec((tm, LANES), lambda i: (i, 0)), pl.BlockSpec((8, LANES), lambda i: (0, 0))],
        out_shape=[jax.ShapeDtypeStruct((n, LANES), I32), jax.ShapeDtypeStruct((8, LANES), I32)],
        scratch_shapes=[pltpu.VMEM((8, LANES), F32)],
        compiler_params=_params("arbitrary"),
        name="rank",
    )(top_idx)


def _dispatch_body(dest_ref, valid_ref, h_ref, o_ref, zeros_ref, sem, zsem):
    i = pl.program_id(0)
    tm = h_ref.shape[0]
    blk = zeros_ref.shape[0]
    nb = o_ref.shape[0] // blk

    @pl.when(i == 0)
    def _():
        zeros_ref[...] = jnp.zeros(zeros_ref.shape, zeros_ref.dtype)

        def for_blocks(fn):
            for jb in range(nb):
                @pl.when(valid_ref[jb] < blk)
                def _():
                    fn(pltpu.make_async_copy(zeros_ref, o_ref.at[pl.ds(jb * blk, blk)], zsem))

        for_blocks(lambda cp: cp.start())
        for_blocks(lambda cp: cp.wait())

    def row_copy(r, kk):
        dst = dest_ref[(i * tm + r) * TOP_K + kk]
        return pltpu.make_async_copy(h_ref.at[pl.ds(r, 1)], o_ref.at[pl.ds(dst, 1)], sem)

    def issue(rb, carry):
        for jj in range(DMA_UNROLL):
            for kk in range(TOP_K):
                row_copy(rb * DMA_UNROLL + jj, kk).start()
        return carry

    lax.fori_loop(0, tm // DMA_UNROLL, issue, 0)
    for kk in range(TOP_K):
        pltpu.make_async_copy(h_ref, o_ref.at[pl.ds(0, tm)], sem).wait()


def _dispatch(dest_flat, block_valid, h2p, n_rows):
    n, dw = h2p.shape
    tm = min(ROW_TILE, n)
    return pl.pallas_call(
        _dispatch_body,
        grid_spec=pltpu.PrefetchScalarGridSpec(
            num_scalar_prefetch=2,
            grid=(n // tm,),
            in_specs=[pl.BlockSpec((tm, dw), lambda i, dst, bv: (i, 0))],
            out_specs=pl.BlockSpec(memory_space=pl.ANY),
            scratch_shapes=[pltpu.VMEM((MOE_BLOCK, dw), h2p.dtype), pltpu.SemaphoreType.DMA,
                            pltpu.SemaphoreType.DMA]),
        out_shape=jax.ShapeDtypeStruct((n_rows, dw), h2p.dtype),
        compiler_params=_params("arbitrary"),
        name="dispatch",
    )(dest_flat, block_valid, h2p)


def _experts_body(be_ref, ns_ref, rev_ref, nu_ref, x_ref, wgu_ref, wd_ref, bgu_ref, bd_ref, y_ref,
                  xlo_ref, xhi_ref):
    j = pl.program_id(0)
    f = pl.program_id(1)
    half = xlo_ref.shape[1]
    nsub = ns_ref[j]

    @pl.when(f == 0)
    def _():
        w = x_ref[...]
        xlo_ref[...] = lax.bitcast_convert_type(lax.shift_left(w, jnp.uint32(16)), F32).astype(BF16)
        xhi_ref[...] = lax.bitcast_convert_type(w & jnp.uint32(0xFFFF0000), F32).astype(BF16)
        y_ref[...] = jnp.broadcast_to(bd_ref[...], y_ref.shape)

    def compute(m):
        gu = (jnp.dot(xlo_ref[0:m, :], wgu_ref[0:half, :].astype(BF16), preferred_element_type=F32)
              + jnp.dot(xhi_ref[0:m, :], wgu_ref[half:, :].astype(BF16), preferred_element_type=F32)) + bgu_ref[...]
        gate = jnp.minimum(gu, SWIGLU_LIMIT)
        glu = gate * jax.nn.sigmoid(SWIGLU_ALPHA * gate)
        up1 = jnp.clip(gu, -SWIGLU_LIMIT, SWIGLU_LIMIT) + 1.0
        n2 = gu.shape[1]
        rows2 = lax.broadcasted_iota(I32, (2 * LANES, LANES), 0)
        cols2 = lax.broadcasted_iota(I32, (2 * LANES, LANES), 1)
        sel = jnp.where(rows2 == 2 * cols2, 1.0, 0.0).astype(BF16)
        parts = []
        for c in range(n2 // LANES):
            sl = slice(c * LANES, (c + 1) * LANES)
            parts.append((glu[:, sl] * pltpu.roll(up1[:, sl], LANES - 1, 1)).astype(BF16))
        acts = [jnp.dot(jnp.concatenate(parts[2 * c:2 * c + 2], axis=1), sel, preferred_element_type=F32)
                for c in range(n2 // (2 * LANES))]
        act = jnp.concatenate(acts, axis=1).astype(BF16)
        y_ref[0:m, :] += jnp.dot(act, wd_ref[...].astype(BF16), preferred_element_type=F32)

    for ns in range(1, MOE_BLOCK // MOE_SUB + 1):
        @pl.when(nsub == ns)
        def _():
            compute(ns * MOE_SUB)


def _experts(block_e, block_nsub, block_rev, n_used, x_rows, w_gate_up, w_down, b_gate_up, b_down):
    p, dw = x_rows.shape
    d = 2 * dw
    dff = w_down.shape[1]
    nb = p // MOE_BLOCK
    nf = dff // FF_TILE

    def blk(j, f, be, ns, rev, nu):
        return jnp.minimum(j, nu[0] - 1)

    def fidx(j, f, be, ns, rev, nu):
        back = rev[jnp.minimum(j, nu[0] - 1)] == 1
        return jnp.where(j < nu[0], jnp.where(back, nf - 1 - f, f), jnp.where(back, 0, nf - 1))

    return pl.pallas_call(
        _experts_body,
        grid_spec=pltpu.PrefetchScalarGridSpec(
            num_scalar_prefetch=4,
            grid=(nb, nf),
            in_specs=[pl.BlockSpec((MOE_BLOCK, dw), lambda j, f, *s: (blk(j, f, *s), 0)),
                      pl.BlockSpec((None, d, 2 * FF_TILE), lambda j, f, *s: (s[0][j], 0, fidx(j, f, *s))),
                      pl.BlockSpec((None, FF_TILE, d), lambda j, f, *s: (s[0][j], fidx(j, f, *s), 0)),
                      pl.BlockSpec((None, 1, 2 * FF_TILE), lambda j, f, *s: (s[0][j], 0, fidx(j, f, *s))),
                      pl.BlockSpec((None, 1, d), lambda j, f, *s: (s[0][j], 0, 0))],
            out_specs=pl.BlockSpec((MOE_BLOCK, d), lambda j, f, *s: (j, 0)),
            scratch_shapes=[pltpu.VMEM((MOE_BLOCK, dw), BF16), pltpu.VMEM((MOE_BLOCK, dw), BF16)]),
        out_shape=jax.ShapeDtypeStruct((p, d), F32),
        compiler_params=_params("arbitrary", "arbitrary"),
        name="experts",
    )(block_e, block_nsub, block_rev, n_used, x_rows, w_gate_up, w_down, b_gate_up, b_down)


def _combine_body(tiles_per_batch, dest_ref, y_ref, gate_ref, x1_ref, ada_ref, g_ref, o_ref, buf, sem):
    i = pl.program_id(0)
    n_tiles = pl.num_programs(0)
    tm = x1_ref.shape[0]

    def row_copy(tile, slot, r, kk):
        src = dest_ref[(tile * tm + r) * TOP_K + kk]
        return pltpu.make_async_copy(y_ref.at[pl.ds(src, 1)], buf.at[slot, kk, pl.ds(r, 1)], sem.at[slot])

    def issue(tile, slot):
        def body(rb, carry):
            for jj in range(DMA_UNROLL):
                for kk in range(TOP_K):
                    row_copy(tile, slot, rb * DMA_UNROLL + jj, kk).start()
            return carry
        lax.fori_loop(0, tm // DMA_UNROLL, body, 0)

    def drain(tile, slot):
        for kk in range(TOP_K):
            pltpu.make_async_copy(y_ref.at[pl.ds(0, tm)], buf.at[slot, kk], sem.at[slot]).wait()

    slot = i % 2

    @pl.when(i == 0)
    def _():
        issue(0, 0)

    @pl.when(i + 1 < n_tiles)
    def _():
        issue(i + 1, 1 - slot)

    drain(i, slot)
    gates = gate_ref[...]
    y = jnp.zeros(x1_ref.shape, F32)
    for kk in range(TOP_K):
        y = y + gates[:, kk:kk + 1] * buf[slot, kk]
    o_ref[...] = x1_ref[...] + ada_ref[5:6, :] * _rms(y, g_ref[...])


def _combine(dest_flat, y_rows, gates, x1, ada3, g_post_ffn, seq):
    n, d = x1.shape
    tm = min(GATHER_TILE, seq)
    tiles_per_batch = seq // tm
    return pl.pallas_call(
        functools.partial(_combine_body, tiles_per_batch),
        grid_spec=pltpu.PrefetchScalarGridSpec(
            num_scalar_prefetch=1,
            grid=(n // tm,),
            in_specs=[pl.BlockSpec(memory_space=pl.ANY),
                      pl.BlockSpec((tm, LANES), lambda i, dst: (i, 0)),
                      pl.BlockSpec((tm, d), lambda i, dst: (i, 0)),
                      pl.BlockSpec((None, 6, d), lambda i, dst: (i // tiles_per_batch, 0, 0)),
                      pl.BlockSpec((1, d), lambda i, dst: (0, 0))],
            out_specs=pl.BlockSpec((tm, d), lambda i, dst: (i, 0)),
            scratch_shapes=[pltpu.VMEM((2, TOP_K, tm, d), F32), pltpu.SemaphoreType.DMA((2,))]),
        out_shape=jax.ShapeDtypeStruct((n, d), F32),
        compiler_params=_params("arbitrary"),
        name="combine",
    )(dest_flat, y_rows, gates, x1, ada3, g_post_ffn)


def _pack_w_in(w_in):
    d = w_in.shape[0]
    o_ki = C_QI + D_QI
    o_wi = o_ki + IDX_DIM
    ki = w_in[:, o_ki:o_wi]
    wi = w_in[:, o_wi:o_wi + IDX_HEADS]
    pad = jnp.zeros((d, LANES - IDX_HEADS), w_in.dtype)
    return jnp.concatenate([w_in[:, :o_ki], ki, ki, wi, pad], axis=1).astype(BF16)


def _layer(x, c, positions, w_ada, b_ada, g_pre_mix, g_post_mix, w_in, conv_w, g_conv_out, g_attn_out, w_out,
           g_pre_ffn, g_post_ffn, w_router, b_router, w_gate_up, b_gate_up, w_down, b_down):
    bsz, seq, d = x.shape
    n = bsz * seq
    x2 = x.reshape(n, d)
    ada3 = _ada(c, w_ada, b_ada).reshape(bsz, 6, d)

    yc, q, k, v, qi, ki, wi = _in_proj(
        x2, ada3, g_pre_mix.reshape(1, d), positions.reshape(n, 1).astype(I32), _pack_w_in(w_in),
        conv_w.reshape(CONV_WIDTH, D_CONV), g_conv_out.reshape(1, D_CONV), seq)
    b3 = lambda a: a.reshape(bsz, seq, a.shape[-1])
    ya = _dsa(b3(q), b3(qi), b3(wi), b3(k), b3(v), b3(ki), g_attn_out.reshape(1, D_ATT)).reshape(n, D_ATT)

    wr = jnp.zeros((d, LANES), F32).at[:, :N_EXPERTS].set(w_router)
    br = jnp.zeros((1, LANES), F32).at[0, :N_EXPERTS].set(b_router)
    x1, h2, top_idx, gates = _out_proj(yc, ya, x2, ada3, w_out.astype(BF16), g_post_mix.reshape(1, d),
                                       g_pre_ffn.reshape(1, d), wr, br, seq)

    rank, cnt = _rank(top_idx)
    counts = cnt[0, :N_EXPERTS]
    padded = (counts + MOE_BLOCK - 1) // MOE_BLOCK * MOE_BLOCK
    pad_ends = jnp.cumsum(padded)
    pad_starts = pad_ends - padded
    nb = -(-(n * TOP_K) // MOE_BLOCK) + N_EXPERTS
    dest = (pad_starts[top_idx[:, :TOP_K]] + rank[:, :TOP_K]).reshape(n * TOP_K).astype(I32)
    block_e = jnp.minimum(jnp.searchsorted(pad_ends, jnp.arange(nb, dtype=I32) * MOE_BLOCK, side='right'),
                          N_EXPERTS - 1).astype(I32)
    n_used = (pad_ends[-1:] // MOE_BLOCK).astype(I32)
    block_start = jnp.arange(nb, dtype=I32) * MOE_BLOCK
    block_valid = jnp.clip(counts[block_e] - (block_start - pad_starts[block_e]), 0, MOE_BLOCK)
    block_valid = jnp.where(block_start < pad_ends[-1], block_valid, 0).astype(I32)
    block_nsub = (block_valid + MOE_SUB - 1) // MOE_SUB
    block_e = jnp.where(block_start < pad_ends[-1], block_e, block_e[n_used[0] - 1])
    block_rev = ((block_start - pad_starts[block_e]) // MOE_BLOCK) % 2

    x_rows = _dispatch(dest, block_valid, h2, nb * MOE_BLOCK)
    y_rows = _experts(block_e, block_nsub, block_rev.astype(I32), n_used, x_rows, w_gate_up, w_down,
                      b_gate_up[:, None, :], b_down[:, None, :])
    out = _combine(dest, y_rows, gates, x1, ada3, g_post_ffn.reshape(1, d), seq)
    return out.reshape(bsz, seq, d)


def kernel(x, c, positions, w_ada, b_ada, g_pre_mix, g_post_mix, w_in, conv_w, g_conv_out, g_attn_out, w_out,
           g_pre_ffn, g_post_ffn, w_router, b_router, w_gate_up, b_gate_up, w_down, b_down):
    for l in range(w_ada.shape[0]):
        x = _layer(x, c, positions, w_ada[l], b_ada[l], g_pre_mix[l], g_post_mix[l], w_in[l], conv_w[l],
                   g_conv_out[l], g_attn_out[l], w_out[l], g_pre_ffn[l], g_post_ffn[l], w_router[l], b_router[l],
                   w_gate_up[l], b_gate_up[l], w_down[l], b_down[l])
    return x
```

```python
import functools

import numpy as np
import jax
import jax.numpy as jnp
from jax import lax
from jax.experimental import pallas as pl
from jax.experimental.pallas import tpu as pltpu

F32 = jnp.float32
BF16 = jnp.bfloat16
I32 = jnp.int32

EPS = 1e-6
LANES = 128
D_CONV = 1024
CONV_WIDTH = 3
N_HEADS = 8
N_KV_HEADS = 2
HEAD_DIM = 128
KV_REP = N_HEADS // N_KV_HEADS
ROPE_DIM = HEAD_DIM // 4
ROPE_THETA = 500000.0
IDX_HEADS = 16
IDX_DIM = 64
IDX_ROPE_DIM = IDX_DIM // 4
TOPK_MAX = 256
N_EXPERTS = 32
TOP_K = 4
SWIGLU_LIMIT = 7.0
SWIGLU_ALPHA = 1.702

D_ATT = N_HEADS * HEAD_DIM
D_KV = N_KV_HEADS * HEAD_DIM
D_QI = IDX_HEADS * IDX_DIM

INT_MIN = -2147483648
NEG_BIG = -1e30

ROW_TILE = 256
ATT_TILE = 512
MOE_BLOCK = 1024
MOE_SUB = 256
FF_TILE = 256
GATHER_TILE = 128
DMA_UNROLL = 8
VMEM_LIMIT = 56 * 1024 * 1024


def _params(*sem):
    return pltpu.CompilerParams(dimension_semantics=sem, vmem_limit_bytes=VMEM_LIMIT)


def _rms(x, g):
    return x * lax.rsqrt(jnp.mean(x * x, axis=-1, keepdims=True) + EPS) * g


def _ada_body(c_ref, w_ref, b_ref, o_ref):
    c = c_ref[...]
    s = (c * jax.nn.sigmoid(c)).astype(BF16)
    o_ref[...] = jnp.dot(s, w_ref[...].astype(BF16), preferred_element_type=F32) + b_ref[...]


def _ada(c, w, b):
    bsz, d = c.shape
    n = w.shape[1]
    tn = 1536
    rows = 8
    cp = jnp.zeros((rows, d), F32).at[:bsz].set(c)
    out = pl.pallas_call(
        _ada_body,
        grid=(n // tn,),
        in_specs=[pl.BlockSpec((rows, d), lambda j: (0, 0)),
                  pl.BlockSpec((d, tn), lambda j: (0, j)),
                  pl.BlockSpec((1, tn), lambda j: (0, j))],
        out_specs=pl.BlockSpec((rows, tn), lambda j: (0, j)),
        out_shape=jax.ShapeDtypeStruct((rows, n), F32),
        compiler_params=_params("arbitrary"),
        name="ada",
    )(cp, w, b.reshape(1, n))
    return out[:bsz]


C_BCU = 0
C_Q = 3 * D_CONV
C_K = C_Q + D_ATT
C_V = C_K + D_KV
C_QI = C_V + D_KV
C_KI = C_QI + D_QI


def _rope_tables(pos, inv_freq, width, rot):
    half = rot // 2
    rows = pos.shape[0]
    lane = lax.broadcasted_iota(I32, (rows, LANES), 1) & (width - 1)
    ang = pos * inv_freq
    cos = jnp.cos(ang)
    sin = jnp.sin(ang)
    c = jnp.where(lane < rot, cos, 1.0)
    a = jnp.where((lane >= half) & (lane < rot), sin, 0.0)
    b = jnp.where(lane < half, -sin, 0.0)
    return c, a, b, half


def _rope(x, tabs):
    c, a, b, half = tabs
    return x * c + pltpu.roll(x, half, 1) * a + pltpu.roll(x, LANES - half, 1) * b


def _in_proj_body(tiles_per_batch, x_ref, ada_ref, g_ref, pos_ref, w_ref, wt_ref, cw_ref, gc_ref, fq_ref, fi_ref,
                  yc_ref, q_ref, k_ref, v_ref, qi_ref, ki_ref, wi_ref, vbuf):
    i = pl.program_id(0)
    tm = x_ref.shape[0]
    h = _rms(x_ref[...], g_ref[...]) * (1.0 + ada_ref[1:2, :]) + ada_ref[0:1, :]
    hb = h.astype(BF16)

    def proj(lo, hi):
        return jnp.dot(hb, w_ref[:, lo:hi], preferred_element_type=F32)

    @pl.when(i % tiles_per_batch == 0)
    def _():
        vbuf[0:8, :] = jnp.zeros((8, D_CONV), F32)

    bcu = proj(C_BCU, C_Q)
    v0 = bcu[:, D_CONV:2 * D_CONV] * bcu[:, 2 * D_CONV:]
    vbuf[8:8 + tm, :] = v0
    v1 = vbuf[7:7 + tm, :]
    v2 = vbuf[6:6 + tm, :]
    y = cw_ref[0:1, :] * v2 + cw_ref[1:2, :] * v1 + cw_ref[2:3, :] * v0
    yc_ref[...] = _rms(bcu[:, :D_CONV] * y, gc_ref[...]).astype(BF16)
    vbuf[0:8, :] = vbuf[tm:tm + 8, :]

    pos = pos_ref[...].astype(F32)
    tq = _rope_tables(pos, fq_ref[...], HEAD_DIM, ROPE_DIM)
    ti = _rope_tables(pos, fi_ref[...], IDX_DIM, IDX_ROPE_DIM)
    scale = HEAD_DIM ** -0.5

    qf = proj(C_Q, C_K)
    for hh in range(N_HEADS):
        sl = slice(hh * LANES, (hh + 1) * LANES)
        q_ref[:, sl] = (_rope(qf[:, sl], tq) * scale).astype(BF16)
    kf = proj(C_K, C_V)
    for hh in range(N_KV_HEADS):
        sl = slice(hh * LANES, (hh + 1) * LANES)
        k_ref[:, sl] = _rope(kf[:, sl], tq).astype(BF16)
    v_ref[...] = proj(C_V, C_QI).astype(BF16)
    qif = proj(C_QI, C_KI)
    for hh in range(D_QI // LANES):
        sl = slice(hh * LANES, (hh + 1) * LANES)
        qi_ref[:, sl] = _rope(qif[:, sl], ti).astype(BF16)
    tail = jnp.dot(hb, wt_ref[...], preferred_element_type=F32)
    ki_ref[...] = _rope(tail[:, :LANES], ti).astype(BF16)
    wi_ref[...] = tail[:, LANES:]


def _in_proj(x2, ada3, g_pre, pos2, w_main, w_tail, conv_w, g_conv, seq):
    n, d = x2.shape
    tm = min(ROW_TILE, seq)
    tiles_per_batch = seq // tm
    half_q = ROPE_DIM // 2
    half_i = IDX_ROPE_DIM // 2
    lane = np.arange(LANES)
    fq = jnp.asarray(ROPE_THETA, F32) ** (-jnp.asarray(lane % half_q, F32) / half_q)
    fi = jnp.asarray(ROPE_THETA, F32) ** (-jnp.asarray(lane % half_i, F32) / half_i)
    row = lambda i: (i, 0)
    fixed = lambda i: (0, 0)
    outs = [(D_CONV, BF16), (D_ATT, BF16), (D_KV, BF16), (D_KV, BF16), (D_QI, BF16), (LANES, BF16), (LANES, F32)]
    return pl.pallas_call(
        functools.partial(_in_proj_body, tiles_per_batch),
        grid=(n // tm,),
        in_specs=[pl.BlockSpec((tm, d), row),
                  pl.BlockSpec((None, 6, d), lambda i: (i // tiles_per_batch, 0, 0)),
                  pl.BlockSpec((1, d), fixed),
                  pl.BlockSpec((tm, 1), row),
                  pl.BlockSpec((d, C_KI), fixed, pipeline_mode=pl.Buffered(1)),
                  pl.BlockSpec((d, 2 * LANES), fixed),
                  pl.BlockSpec((CONV_WIDTH, D_CONV), fixed),
                  pl.BlockSpec((1, D_CONV), fixed),
                  pl.BlockSpec((1, LANES), fixed),
                  pl.BlockSpec((1, LANES), fixed)],
        out_specs=[pl.BlockSpec((tm, w), row) for w, _ in outs],
        out_shape=[jax.ShapeDtypeStruct((n, w), dt) for w, dt in outs],
        scratch_shapes=[pltpu.VMEM((tm + 8, D_CONV), F32)],
        compiler_params=_params("arbitrary"),
        name="in_proj",
    )(x2, ada3, g_pre, pos2, w_main, w_tail, conv_w, g_conv, fq.reshape(1, LANES), fi.reshape(1, LANES))


def _dsa_body(topk, nbits, q_ref, qi_ref, wi_ref, k_ref, v_ref, ki_ref, g_ref, o_ref,
              keys_ref, keys_t_ref, qm_ref, wb_ref, m_ref, l_ref, acc_ref):
    i = pl.program_id(1)
    tq = q_ref.shape[0]
    tk = tq
    nch = tk // LANES
    nkb = i + 1
    nt = (((1,), (1,)), ((), ()))

    lane = lax.broadcasted_iota(I32, (tq, LANES), 1)
    for p in range(IDX_HEADS // 2):
        qp = qi_ref[:, p * LANES:(p + 1) * LANES].astype(F32)
        qm_ref[(2 * p) * tq:(2 * p + 1) * tq, :] = jnp.where(lane < IDX_DIM, qp, 0.0).astype(BF16)
        qm_ref[(2 * p + 1) * tq:(2 * p + 2) * tq, :] = jnp.where(lane >= IDX_DIM, qp, 0.0).astype(BF16)
    w = wi_ref[...]
    for hh in range(IDX_HEADS):
        wb_ref[hh * tq:(hh + 1) * tq, :] = jnp.broadcast_to(w[:, hh:hh + 1], (tq, LANES))

    row = lax.broadcasted_iota(I32, (tq, tk), 0)
    col = lax.broadcasted_iota(I32, (tq, tk), 1)

    def score_block(kb, carry):
        k0 = pl.multiple_of(kb * tk, tk)
        kib = ki_ref[pl.ds(k0, tk), :]
        acc = jnp.zeros((tq, tk), F32)
        for j in range(IDX_HEADS // 4):
            d = lax.dot_general(qm_ref[4 * j * tq:4 * (j + 1) * tq, :], kib, nt, preferred_element_type=F32)
            wbj = wb_ref[4 * j * tq:4 * (j + 1) * tq, :]
            e = jnp.maximum(d, 0.0) * jnp.concatenate([wbj] * nch, axis=1)
            acc = acc + ((e[0:tq] + e[tq:2 * tq]) + (e[2 * tq:3 * tq] + e[3 * tq:4 * tq]))
        bits = lax.bitcast_convert_type(acc, I32)
        key = jnp.where(bits < 0, bits ^ 0x7FFFFFFF, bits)
        key = jnp.where(k0 + col <= i * tq + row, key, INT_MIN)
        keys_ref[kb] = key
        keys_t_ref[kb] = key.T
        return carry

    lax.fori_loop(0, nkb, score_block, 0)

    def count_t(pred):
        def body(kb, part):
            ind = jnp.where(pred(keys_t_ref[kb]), 1.0, 0.0)
            for s0 in range(0, tk, 8):
                part = part + ind[s0:s0 + 8]
            return part
        part = lax.fori_loop(0, nkb, body, jnp.zeros((8, tq), F32))
        return jnp.sum(part, axis=0, keepdims=True)

    kf = float(topk)
    t0 = jnp.where(count_t(lambda kk: kk >= 0) >= kf, 0, INT_MIN)

    def value_bit(bi, t):
        cand = t + lax.shift_left(jnp.int32(1), 30 - bi)
        return jnp.where(count_t(lambda kk: kk >= cand) >= kf, cand, t)

    thr_t = lax.fori_loop(0, 31, value_bit, t0)
    n_ge = count_t(lambda kk: kk >= thr_t)
    surplus = jnp.where(thr_t > INT_MIN, n_ge - kf, 0.0)
    thr = jnp.broadcast_to(thr_t, (tq, tq)).T[:, :LANES]

    def count(pred):
        def body(kb, part):
            kk = keys_ref[kb]
            for c in range(nch):
                part = part + jnp.where(pred(kk[:, c * LANES:(c + 1) * LANES], kb * tk + c * LANES), 1.0, 0.0)
            return part
        part = lax.fori_loop(0, nkb, body, jnp.zeros((tq, LANES), F32))
        return jnp.broadcast_to(jnp.sum(part, axis=1, keepdims=True), (tq, LANES))

    zero = jnp.zeros((tq, LANES), I32)

    @pl.when(jnp.max(surplus) > 0.0)
    def _():
        need = kf - count(lambda kk, _: kk > thr)

        def index_bit(bi, p):
            cand = p + lax.shift_left(jnp.int32(1), nbits - 1 - bi)
            f = count(lambda kk, base: (kk == thr) & (base + lane < cand))
            return jnp.where(f <= need, cand, p)

        pend = lax.fori_loop(0, nbits, index_bit, zero)

        def drop(kb, carry):
            kk = keys_ref[kb]
            parts = []
            for c in range(nch):
                kc = kk[:, c * LANES:(c + 1) * LANES]
                parts.append(jnp.where((kc == thr) & (kb * tk + c * LANES + lane >= pend), INT_MIN, kc))
            keys_ref[kb] = jnp.concatenate(parts, axis=1)
            return carry

        lax.fori_loop(0, nkb, drop, 0)

    tsel = jnp.maximum(thr, INT_MIN + 1)

    m_ref[...] = jnp.full(m_ref.shape, NEG_BIG, F32)
    l_ref[...] = jnp.zeros(l_ref.shape, F32)
    acc_ref[...] = jnp.zeros(acc_ref.shape, F32)

    def attend(kb, carry):
        k0 = pl.multiple_of(kb * tk, tk)
        kk = keys_ref[kb]
        sel = jnp.concatenate([jnp.where(kk[:, c * LANES:(c + 1) * LANES] >= tsel, 0.0, NEG_BIG)
                               for c in range(nch)], axis=1)
        for g in range(N_KV_HEADS):
            kg = k_ref[pl.ds(k0, tk), g * LANES:(g + 1) * LANES]
            vg = v_ref[pl.ds(k0, tk), g * LANES:(g + 1) * LANES]
            qg = jnp.concatenate([q_ref[:, (KV_REP * g + r) * LANES:(KV_REP * g + r + 1) * LANES]
                                  for r in range(KV_REP)], axis=0)
            s = lax.dot_general(qg, kg, nt, preferred_element_type=F32)
            s = jnp.concatenate([s[r * tq:(r + 1) * tq] + sel for r in range(KV_REP)], axis=0)
            m_old = m_ref[g]
            m_new = jnp.maximum(m_old, jnp.max(s, axis=1, keepdims=True))
            alpha = jnp.exp(m_old - m_new)
            p = jnp.exp(s - jnp.concatenate([m_new] * nch, axis=1))
            l_ref[g] = alpha * l_ref[g] + jnp.sum(p, axis=1, keepdims=True)
            acc_ref[g] = alpha * acc_ref[g] + jnp.dot(p.astype(BF16), vg, preferred_element_type=F32)
            m_ref[g] = m_new
        return carry

    lax.fori_loop(0, nkb, attend, 0)

    heads = []
    for g in range(N_KV_HEADS):
        og = acc_ref[g] / l_ref[g]
        heads += [og[r * tq:(r + 1) * tq] for r in range(KV_REP)]
    o_ref[...] = _rms(jnp.concatenate(heads, axis=1), g_ref[...]).astype(BF16)


def _dsa(q, qi, wi, k, v, ki, g_attn):
    bsz, seq, _ = q.shape
    tq = min(ATT_TILE, seq)
    nq = seq // tq
    topk = min(TOPK_MAX, seq // 4)
    nbits = int(seq).bit_length()
    qblk = lambda w: pl.BlockSpec((None, tq, w), lambda b, i: (b, i, 0))
    full = lambda w: pl.BlockSpec((None, seq, w), lambda b, i: (b, 0, 0), pipeline_mode=pl.Buffered(1))
    return pl.pallas_call(
        functools.partial(_dsa_body, topk, nbits),
        grid=(bsz, nq),
        in_specs=[qblk(D_ATT), qblk(D_QI), qblk(LANES), full(D_KV), full(D_KV), full(LANES),
                  pl.BlockSpec((1, D_ATT), lambda b, i: (0, 0))],
        out_specs=qblk(D_ATT),
        out_shape=jax.ShapeDtypeStruct((bsz, seq, D_ATT), BF16),
        scratch_shapes=[pltpu.VMEM((nq, tq, tq), I32),
                        pltpu.VMEM((nq, tq, tq), I32),
                        pltpu.VMEM((IDX_HEADS * tq, LANES), BF16),
                        pltpu.VMEM((IDX_HEADS * tq, LANES), F32),
                        pltpu.VMEM((N_KV_HEADS, KV_REP * tq, LANES), F32),
                        pltpu.VMEM((N_KV_HEADS, KV_REP * tq, LANES), F32),
                        pltpu.VMEM((N_KV_HEADS, KV_REP * tq, LANES), F32)],
        compiler_params=_params("arbitrary", "arbitrary"),
        name="dsa",
    )(q, qi, wi, k, v, ki, g_attn)


def _split_bf16(x):
    hi = x.astype(BF16)
    return hi, (x - hi.astype(F32)).astype(BF16)


def _out_proj_body(yc_ref, ya_ref, x_ref, ada_ref, wo_ref, gpm_ref, gpf_ref, wr_ref, br_ref,
                   x1_ref, h2_ref, idx_ref, gate_ref):
    tm = x_ref.shape[0]
    mix = (jnp.dot(yc_ref[...], wo_ref[0:D_CONV, :], preferred_element_type=F32)
           + jnp.dot(ya_ref[...], wo_ref[D_CONV:, :], preferred_element_type=F32))
    x1 = x_ref[...] + ada_ref[2:3, :] * _rms(mix, gpm_ref[...])
    x1_ref[...] = x1
    h2 = _rms(x1, gpf_ref[...]) * (1.0 + ada_ref[4:5, :]) + ada_ref[3:4, :]
    half = h2.shape[1] // 2
    lo = lax.bitcast_convert_type(h2[:, :half].astype(BF16).astype(F32), jnp.uint32)
    hi = lax.bitcast_convert_type(h2[:, half:].astype(BF16).astype(F32), jnp.uint32)
    h2_ref[...] = hi | lax.shift_right_logical(lo, jnp.uint32(16))

    h_hi, h_lo = _split_bf16(h2)
    w_hi, w_lo = _split_bf16(wr_ref[...])
    logits = (jnp.dot(h_hi, w_hi, preferred_element_type=F32) + jnp.dot(h_hi, w_lo, preferred_element_type=F32)
              + jnp.dot(h_lo, w_hi, preferred_element_type=F32)) + br_ref[...]
    lane = lax.broadcasted_iota(I32, (tm, LANES), 1).astype(F32)
    cur = jnp.where(lane < N_EXPERTS, logits, -jnp.inf)
    vals, idxs = [], []
    for _ in range(TOP_K):
        m = jnp.max(cur, axis=1, keepdims=True)
        am = jnp.min(jnp.where(cur == m, lane, float(LANES)), axis=1, keepdims=True)
        vals.append(m)
        idxs.append(am)
        cur = jnp.where(lane == am, -jnp.inf, cur)
    es = [jnp.exp(vv - vals[0]) for vv in vals]
    tot = es[0] + es[1] + es[2] + es[3]
    idx_out = jnp.zeros((tm, LANES), I32)
    gate_out = jnp.zeros((tm, LANES), F32)
    for kk in range(TOP_K):
        idx_out = jnp.where(lane == kk, idxs[kk].astype(I32), idx_out)
        gate_out = jnp.where(lane == kk, es[kk] / tot, gate_out)
    idx_ref[...] = idx_out
    gate_ref[...] = gate_out


def _out_proj(yc, ya, x2, ada3, w_out, g_post_mix, g_pre_ffn, w_router, b_router, seq):
    n, d = x2.shape
    tm = min(ROW_TILE, seq)
    tiles_per_batch = seq // tm
    row = lambda i: (i, 0)
    fixed = lambda i: (0, 0)
    return pl.pallas_call(
        _out_proj_body,
        grid=(n // tm,),
        in_specs=[pl.BlockSpec((tm, D_CONV), row),
                  pl.BlockSpec((tm, D_ATT), row),
                  pl.BlockSpec((tm, d), row),
                  pl.BlockSpec((None, 6, d), lambda i: (i // tiles_per_batch, 0, 0)),
                  pl.BlockSpec((D_CONV + D_ATT, d), fixed, pipeline_mode=pl.Buffered(1)),
                  pl.BlockSpec((1, d), fixed),
                  pl.BlockSpec((1, d), fixed),
                  pl.BlockSpec((d, LANES), fixed),
                  pl.BlockSpec((1, LANES), fixed)],
        out_specs=[pl.BlockSpec((tm, d), row), pl.BlockSpec((tm, d // 2), row),
                   pl.BlockSpec((tm, LANES), row), pl.BlockSpec((tm, LANES), row)],
        out_shape=[jax.ShapeDtypeStruct((n, d), F32), jax.ShapeDtypeStruct((n, d // 2), jnp.uint32),
                   jax.ShapeDtypeStruct((n, LANES), I32), jax.ShapeDtypeStruct((n, LANES), F32)],
        compiler_params=_params("arbitrary"),
        name="out_proj",
    )(yc, ya, x2, ada3, w_out, g_post_mix, g_pre_ffn, w_router, b_router)


def _rank_body(idx_ref, rank_ref, cnt_ref, carry):
    i = pl.program_id(0)
    tm = idx_ref.shape[0]

    @pl.when(i == 0)
    def _():
        carry[...] = jnp.zeros(carry.shape, F32)

    lane = lax.broadcasted_iota(I32, (tm, LANES), 1)
    r = lax.broadcasted_iota(I32, (tm, tm), 0)
    c = lax.broadcasted_iota(I32, (tm, tm), 1)
    before = jnp.where(c < r, 1.0, 0.0).astype(BF16)
    idx = idx_ref[...]
    base = carry[0:1, :]
    out = jnp.zeros((tm, LANES), F32)
    for kk in range(TOP_K):
        onehot = jnp.where(lane == idx[:, kk:kk + 1], 1.0, 0.0)
        prefix = jnp.dot(before, onehot.astype(BF16), preferred_element_type=F32) + base
        rk = jnp.sum(onehot * prefix, axis=1, keepdims=True)
        out = jnp.where(lane == kk, rk, out)
        base = base + jnp.sum(onehot, axis=0, keepdims=True)
    rank_ref[...] = out.astype(I32)
    carry[0:1, :] = base
    cnt_ref[...] = jnp.broadcast_to(base, cnt_ref.shape).astype(I32)


def _rank(top_idx):
    n = top_idx.shape[0]
    tm = min(ROW_TILE, n)
    return pl.pallas_call(
        _rank_body,
        grid=(n // tm,),
        in_specs=[pl.BlockSpec((tm, LANES), lambda i: (i, 0))],
        out_specs=[pl.BlockSpec((tm, LANES), lambda i: (i, 0)), pl.BlockSpec((8, LANES), lambda i: (0, 0))],
        out_shape=[jax.ShapeDtypeStruct((n, LANES), I32), jax.ShapeDtypeStruct((8, LANES), I32)],
        scratch_shapes=[pltpu.VMEM((8, LANES), F32)],
        compiler_params=_params("arbitrary"),
        name="rank",
    )(top_idx)


def _dispatch_body(dest_ref, valid_ref, h_ref, o_ref, zeros_ref, sem, zsem):
    i = pl.program_id(0)
    tm = h_ref.shape[0]
    blk = zeros_ref.shape[0]
    nb = o_ref.shape[0] // blk

    @pl.when(i == 0)
    def _():
        zeros_ref[...] = jnp.zeros(zeros_ref.shape, zeros_ref.dtype)

        def for_blocks(fn):
            for jb in range(nb):
                @pl.when(valid_ref[jb] < blk)
                def _():
                    fn(pltpu.make_async_copy(zeros_ref, o_ref.at[pl.ds(jb * blk, blk)], zsem))

        for_blocks(lambda cp: cp.start())
        for_blocks(lambda cp: cp.wait())

    def row_copy(r, kk):
        dst = dest_ref[(i * tm + r) * TOP_K + kk]
        return pltpu.make_async_copy(h_ref.at[pl.ds(r, 1)], o_ref.at[pl.ds(dst, 1)], sem)

    def issue(rb, carry):
        for jj in range(DMA_UNROLL):
            for kk in range(TOP_K):
                row_copy(rb * DMA_UNROLL + jj, kk).start()
        return carry

    lax.fori_loop(0, tm // DMA_UNROLL, issue, 0)
    for kk in range(TOP_K):
        pltpu.make_async_copy(h_ref, o_ref.at[pl.ds(0, tm)], sem).wait()


def _dispatch(dest_flat, block_valid, h2p, n_rows):
    n, dw = h2p.shape
    tm = min(ROW_TILE, n)
    return pl.pallas_call(
        _dispatch_body,
        grid_spec=pltpu.PrefetchScalarGridSpec(
            num_scalar_prefetch=2,
            grid=(n // tm,),
            in_specs=[pl.BlockSpec((tm, dw), lambda i, dst, bv: (i, 0))],
            out_specs=pl.BlockSpec(memory_space=pl.ANY),
            scratch_shapes=[pltpu.VMEM((MOE_BLOCK, dw), h2p.dtype), pltpu.SemaphoreType.DMA,
                            pltpu.SemaphoreType.DMA]),
        out_shape=jax.ShapeDtypeStruct((n_rows, dw), h2p.dtype),
        compiler_params=_params("arbitrary"),
        name="dispatch",
    )(dest_flat, block_valid, h2p)


def _experts_body(be_ref, ns_ref, rev_ref, nu_ref, x_ref, wgu_ref, wd_ref, bgu_ref, bd_ref, y_ref,
                  xlo_ref, xhi_ref):
    j = pl.program_id(0)
    f = pl.program_id(1)
    half = xlo_ref.shape[1]
    nsub = ns_ref[j]

    @pl.when(f == 0)
    def _():
        w = x_ref[...]
        xlo_ref[...] = lax.bitcast_convert_type(lax.shift_left(w, jnp.uint32(16)), F32).astype(BF16)
        xhi_ref[...] = lax.bitcast_convert_type(w & jnp.uint32(0xFFFF0000), F32).astype(BF16)
        y_ref[...] = jnp.broadcast_to(bd_ref[...], y_ref.shape)

    def compute(m):
        gu = (jnp.dot(xlo_ref[0:m, :], wgu_ref[0:half, :].astype(BF16), preferred_element_type=F32)
              + jnp.dot(xhi_ref[0:m, :], wgu_ref[half:, :].astype(BF16), preferred_element_type=F32)) + bgu_ref[...]
        gate = jnp.minimum(gu, SWIGLU_LIMIT)
        glu = gate * jax.nn.sigmoid(SWIGLU_ALPHA * gate)
        up1 = jnp.clip(gu, -SWIGLU_LIMIT, SWIGLU_LIMIT) + 1.0
        n2 = gu.shape[1]
        rows2 = lax.broadcasted_iota(I32, (2 * LANES, LANES), 0)
        cols2 = lax.broadcasted_iota(I32, (2 * LANES, LANES), 1)
        sel = jnp.where(rows2 == 2 * cols2, 1.0, 0.0).astype(BF16)
        parts = []
        for c in range(n2 // LANES):
            sl = slice(c * LANES, (c + 1) * LANES)
            parts.append((glu[:, sl] * pltpu.roll(up1[:, sl], LANES - 1, 1)).astype(BF16))
        acts = [jnp.dot(jnp.concatenate(parts[2 * c:2 * c + 2], axis=1), sel, preferred_element_type=F32)
                for c in range(n2 // (2 * LANES))]
        act = jnp.concatenate(acts, axis=1).astype(BF16)
        y_ref[0:m, :] += jnp.dot(act, wd_ref[...].astype(BF16), preferred_element_type=F32)

    for ns in range(1, MOE_BLOCK // MOE_SUB + 1):
        @pl.when(nsub == ns)
        def _():
            compute(ns * MOE_SUB)


def _experts(block_e, block_nsub, block_rev, n_used, x_rows, w_gate_up, w_down, b_gate_up, b_down):
    p, dw = x_rows.shape
    d = 2 * dw
    dff = w_down.shape[1]
    nb = p // MOE_BLOCK
    nf = dff // FF_TILE

    def blk(j, f, be, ns, rev, nu):
        return jnp.minimum(j, nu[0] - 1)

    def fidx(j, f, be, ns, rev, nu):
        back = rev[jnp.minimum(j, nu[0] - 1)] == 1
        return jnp.where(j < nu[0], jnp.where(back, nf - 1 - f, f), jnp.where(back, 0, nf - 1))

    return pl.pallas_call(
        _experts_body,
        grid_spec=pltpu.PrefetchScalarGridSpec(
            num_scalar_prefetch=4,
            grid=(nb, nf),
            in_specs=[pl.BlockSpec((MOE_BLOCK, dw), lambda j, f, *s: (blk(j, f, *s), 0)),
                      pl.BlockSpec((None, d, 2 * FF_TILE), lambda j, f, *s: (s[0][j], 0, fidx(j, f, *s))),
                      pl.BlockSpec((None, FF_TILE, d), lambda j, f, *s: (s[0][j], fidx(j, f, *s), 0)),
                      pl.BlockSpec((None, 1, 2 * FF_TILE), lambda j, f, *s: (s[0][j], 0, fidx(j, f, *s))),
                      pl.BlockSpec((None, 1, d), lambda j, f, *s: (s[0][j], 0, 0))],
            out_specs=pl.BlockSpec((MOE_BLOCK, d), lambda j, f, *s: (j, 0)),
            scratch_shapes=[pltpu.VMEM((MOE_BLOCK, dw), BF16), pltpu.VMEM((MOE_BLOCK, dw), BF16)]),
        out_shape=jax.ShapeDtypeStruct((p, d), F32),
        compiler_params=_params("arbitrary", "arbitrary"),
        name="experts",
    )(block_e, block_nsub, block_rev, n_used, x_rows, w_gate_up, w_down, b_gate_up, b_down)


def _combine_body(tiles_per_batch, dest_ref, y_ref, gate_ref, x1_ref, ada_ref, g_ref, o_ref, buf, sem):
    i = pl.program_id(0)
    n_tiles = pl.num_programs(0)
    tm = x1_ref.shape[0]

    def row_copy(tile, slot, r, kk):
        src = dest_ref[(tile * tm + r) * TOP_K + kk]
        return pltpu.make_async_copy(y_ref.at[pl.ds(src, 1)], buf.at[slot, kk, pl.ds(r, 1)], sem.at[slot])

    def issue(tile, slot):
        def body(rb, carry):
            for jj in range(DMA_UNROLL):
                for kk in range(TOP_K):
                    row_copy(tile, slot, rb * DMA_UNROLL + jj, kk).start()
            return carry
        lax.fori_loop(0, tm // DMA_UNROLL, body, 0)

    def drain(tile, slot):
        for kk in range(TOP_K):
            pltpu.make_async_copy(y_ref.at[pl.ds(0, tm)], buf.at[slot, kk], sem.at[slot]).wait()

    slot = i % 2

    @pl.when(i == 0)
    def _():
        issue(0, 0)

    @pl.when(i + 1 < n_tiles)
    def _():
        issue(i + 1, 1 - slot)

    drain(i, slot)
    gates = gate_ref[...]
    y = jnp.zeros(x1_ref.shape, F32)
    for kk in range(TOP_K):
        y = y + gates[:, kk:kk + 1] * buf[slot, kk]
    o_ref[...] = x1_ref[...] + ada_ref[5:6, :] * _rms(y, g_ref[...])


def _combine(dest_flat, y_rows, gates, x1, ada3, g_post_ffn, seq):
    n, d = x1.shape
    tm = min(GATHER_TILE, seq)
    tiles_per_batch = seq // tm
    return pl.pallas_call(
        functools.partial(_combine_body, tiles_per_batch),
        grid_spec=pltpu.PrefetchScalarGridSpec(
            num_scalar_prefetch=1,
            grid=(n // tm,),
            in_specs=[pl.BlockSpec(memory_space=pl.ANY),
                      pl.BlockSpec((tm, LANES), lambda i, dst: (i, 0)),
                      pl.BlockSpec((tm, d), lambda i, dst: (i, 0)),
                      pl.BlockSpec((None, 6, d), lambda i, dst: (i // tiles_per_batch, 0, 0)),
                      pl.BlockSpec((1, d), lambda i, dst: (0, 0))],
            out_specs=pl.BlockSpec((tm, d), lambda i, dst: (i, 0)),
            scratch_shapes=[pltpu.VMEM((2, TOP_K, tm, d), F32), pltpu.SemaphoreType.DMA((2,))]),
        out_shape=jax.ShapeDtypeStruct((n, d), F32),
        compiler_params=_params("arbitrary"),
        name="combine",
    )(dest_flat, y_rows, gates, x1, ada3, g_post_ffn)


def _tail_w_in(w_in):
    d = w_in.shape[0]
    o_wi = C_KI + IDX_DIM
    ki = w_in[:, C_KI:o_wi]
    wi = w_in[:, o_wi:o_wi + IDX_HEADS]
    pad = jnp.zeros((d, LANES - IDX_HEADS), w_in.dtype)
    return jnp.concatenate([ki, ki, wi, pad], axis=1)


def _layer(x, c, positions, w_ada, b_ada, g_pre_mix, g_post_mix, w_in, conv_w, g_conv_out, g_attn_out, w_out,
           g_pre_ffn, g_post_ffn, w_router, b_router, w_gate_up, b_gate_up, w_down, b_down):
    bsz, seq, d = x.shape
    n = bsz * seq
    x2 = x.reshape(n, d)
    ada3 = _ada(c, w_ada, b_ada).reshape(bsz, 6, d)
    w_in_b = w_in.astype(BF16)

    yc, q, k, v, qi, ki, wi = _in_proj(
        x2, ada3, g_pre_mix.reshape(1, d), positions.reshape(n, 1).astype(I32), w_in_b, _tail_w_in(w_in_b),
        conv_w.reshape(CONV_WIDTH, D_CONV), g_conv_out.reshape(1, D_CONV), seq)
    b3 = lambda a: a.reshape(bsz, seq, a.shape[-1])
    ya = _dsa(b3(q), b3(qi), b3(wi), b3(k), b3(v), b3(ki), g_attn_out.reshape(1, D_ATT)).reshape(n, D_ATT)

    wr = jnp.zeros((d, LANES), F32).at[:, :N_EXPERTS].set(w_router)
    br = jnp.zeros((1, LANES), F32).at[0, :N_EXPERTS].set(b_router)
    x1, h2, top_idx, gates = _out_proj(yc, ya, x2, ada3, w_out.astype(BF16), g_post_mix.reshape(1, d),
                                       g_pre_ffn.reshape(1, d), wr, br, seq)

    rank, cnt = _rank(top_idx)
    counts = cnt[0, :N_EXPERTS]
    padded = (counts + MOE_BLOCK - 1) // MOE_BLOCK * MOE_BLOCK
    pad_ends = jnp.cumsum(padded)
    pad_starts = pad_ends - padded
    nb = -(-(n * TOP_K) // MOE_BLOCK) + N_EXPERTS
    dest = (pad_starts[top_idx[:, :TOP_K]] + rank[:, :TOP_K]).reshape(n * TOP_K).astype(I32)
    block_e = jnp.minimum(jnp.searchsorted(pad_ends, jnp.arange(nb, dtype=I32) * MOE_BLOCK, side='right'),
                          N_EXPERTS - 1).astype(I32)
    n_used = (pad_ends[-1:] // MOE_BLOCK).astype(I32)
    block_start = jnp.arange(nb, dtype=I32) * MOE_BLOCK
    block_valid = jnp.clip(counts[block_e] - (block_start - pad_starts[block_e]), 0, MOE_BLOCK)
    block_valid = jnp.where(block_start < pad_ends[-1], block_valid, 0).astype(I32)
    block_nsub = (block_valid + MOE_SUB - 1) // MOE_SUB
    block_e = jnp.where(block_start < pad_ends[-1], block_e, block_e[n_used[0] - 1])
    block_rev = ((block_start - pad_starts[block_e]) // MOE_BLOCK) % 2

    x_rows = _dispatch(dest, block_valid, h2, nb * MOE_BLOCK)
    y_rows = _experts(block_e, block_nsub, block_rev.astype(I32), n_used, x_rows, w_gate_up, w_down,
                      b_gate_up[:, None, :], b_down[:, None, :])
    out = _combine(dest, y_rows, gates, x1, ada3, g_post_ffn.reshape(1, d), seq)
    return out.reshape(bsz, seq, d)


def kernel(x, c, positions, w_ada, b_ada, g_pre_mix, g_post_mix, w_in, conv_w, g_conv_out, g_attn_out, w_out,
           g_pre_ffn, g_post_ffn, w_router, b_router, w_gate_up, b_gate_up, w_down, b_down):
    for l in range(w_ada.shape[0]):
        x = _layer(x, c, positions, w_ada[l], b_ada[l], g_pre_mix[l], g_post_mix[l], w_in[l], conv_w[l],
                   g_conv_out[l], g_attn_out[l], w_out[l], g_pre_ffn[l], g_post_ffn[l], w_router[l], b_router[l],
                   w_gate_up[l], b_gate_up[l], w_down[l], b_down[l])
    return x
```

```python
import functools

import numpy as np
import jax
import jax.numpy as jnp
from jax import lax
from jax.experimental import pallas as pl
from jax.experimental.pallas import tpu as pltpu

F32 = jnp.float32
BF16 = jnp.bfloat16
I32 = jnp.int32

EPS = 1e-6
LANES = 128
D_CONV = 1024
CONV_WIDTH = 3
N_HEADS = 8
N_KV_HEADS = 2
HEAD_DIM = 128
KV_REP = N_HEADS // N_KV_HEADS
ROPE_DIM = HEAD_DIM // 4
ROPE_THETA = 500000.0
IDX_HEADS = 16
IDX_DIM = 64
IDX_ROPE_DIM = IDX_DIM // 4
TOPK_MAX = 256
N_EXPERTS = 32
TOP_K = 4
SWIGLU_LIMIT = 7.0
SWIGLU_ALPHA = 1.702

D_ATT = N_HEADS * HEAD_DIM
D_KV = N_KV_HEADS * HEAD_DIM
D_QI = IDX_HEADS * IDX_DIM

INT_MIN = -2147483648
NEG_BIG = -1e30

ROW_TILE = 512
ATT_TILE = 512
MOE_BLOCK = 512
MOE_SUB = 256
FF_TILE = 512
GATHER_TILE = 128
DMA_UNROLL = 8
VMEM_LIMIT = 56 * 1024 * 1024


def _params(*sem):
    return pltpu.CompilerParams(dimension_semantics=sem, vmem_limit_bytes=VMEM_LIMIT)


def _rms(x, g):
    return x * lax.rsqrt(jnp.mean(x * x, axis=-1, keepdims=True) + EPS) * g


def _ada_body(c_ref, w_ref, b_ref, o_ref):
    c = c_ref[...]
    s = (c * jax.nn.sigmoid(c)).astype(BF16)
    o_ref[...] = jnp.dot(s, w_ref[...].astype(BF16), preferred_element_type=F32) + b_ref[...]


def _ada(c, w, b):
    bsz, d = c.shape
    n = w.shape[1]
    tn = 1536
    rows = 8
    cp = jnp.zeros((rows, d), F32).at[:bsz].set(c)
    out = pl.pallas_call(
        _ada_body,
        grid=(n // tn,),
        in_specs=[pl.BlockSpec((rows, d), lambda j: (0, 0)),
                  pl.BlockSpec((d, tn), lambda j: (0, j)),
                  pl.BlockSpec((1, tn), lambda j: (0, j))],
        out_specs=pl.BlockSpec((rows, tn), lambda j: (0, j)),
        out_shape=jax.ShapeDtypeStruct((rows, n), F32),
        compiler_params=_params("arbitrary"),
        name="ada",
    )(cp, w, b.reshape(1, n))
    return out[:bsz]


C_BCU = 0
C_Q = 3 * D_CONV
C_K = C_Q + D_ATT
C_V = C_K + D_KV
C_QI = C_V + D_KV
C_KI = C_QI + D_QI


def _rope_tables(pos, inv_freq, width, rot):
    half = rot // 2
    rows = pos.shape[0]
    lane = lax.broadcasted_iota(I32, (rows, LANES), 1) & (width - 1)
    ang = pos * inv_freq
    cos = jnp.cos(ang)
    sin = jnp.sin(ang)
    c = jnp.where(lane < rot, cos, 1.0)
    a = jnp.where((lane >= half) & (lane < rot), sin, 0.0)
    b = jnp.where(lane < half, -sin, 0.0)
    return c, a, b, half


def _rope(x, tabs):
    c, a, b, half = tabs
    return x * c + pltpu.roll(x, half, 1) * a + pltpu.roll(x, LANES - half, 1) * b


def _in_proj_body(tiles_per_batch, x_ref, ada_ref, g_ref, pos_ref, w_ref, wt_ref, cw_ref, gc_ref, fq_ref, fi_ref,
                  yc_ref, q_ref, k_ref, v_ref, qi_ref, ki_ref, wi_ref, vbuf):
    i = pl.program_id(0)
    tm = x_ref.shape[0]
    h = _rms(x_ref[...], g_ref[...]) * (1.0 + ada_ref[1:2, :]) + ada_ref[0:1, :]
    hb = h.astype(BF16)

    def proj(lo, hi):
        return jnp.dot(hb, w_ref[:, lo:hi], preferred_element_type=F32)

    @pl.when(i % tiles_per_batch == 0)
    def _():
        vbuf[0:8, :] = jnp.zeros((8, D_CONV), F32)

    bcu = proj(C_BCU, C_Q)
    v0 = bcu[:, D_CONV:2 * D_CONV] * bcu[:, 2 * D_CONV:]
    vbuf[8:8 + tm, :] = v0
    v1 = vbuf[7:7 + tm, :]
    v2 = vbuf[6:6 + tm, :]
    y = cw_ref[0:1, :] * v2 + cw_ref[1:2, :] * v1 + cw_ref[2:3, :] * v0
    yc_ref[...] = _rms(bcu[:, :D_CONV] * y, gc_ref[...]).astype(BF16)
    vbuf[0:8, :] = vbuf[tm:tm + 8, :]

    pos = pos_ref[...].astype(F32)
    tq = _rope_tables(pos, fq_ref[...], HEAD_DIM, ROPE_DIM)
    ti = _rope_tables(pos, fi_ref[...], IDX_DIM, IDX_ROPE_DIM)
    scale = HEAD_DIM ** -0.5

    qf = proj(C_Q, C_K)
    for hh in range(N_HEADS):
        sl = slice(hh * LANES, (hh + 1) * LANES)
        q_ref[:, sl] = (_rope(qf[:, sl], tq) * scale).astype(BF16)
    kf = proj(C_K, C_V)
    for hh in range(N_KV_HEADS):
        sl = slice(hh * LANES, (hh + 1) * LANES)
        k_ref[:, sl] = _rope(kf[:, sl], tq).astype(BF16)
    v_ref[...] = proj(C_V, C_QI).astype(BF16)
    qif = proj(C_QI, C_KI)
    for hh in range(D_QI // LANES):
        sl = slice(hh * LANES, (hh + 1) * LANES)
        qi_ref[:, sl] = _rope(qif[:, sl], ti).astype(BF16)
    tail = jnp.dot(hb, wt_ref[...], preferred_element_type=F32)
    ki_ref[...] = _rope(tail[:, :LANES], ti).astype(BF16)
    wi_ref[...] = tail[:, LANES:]


def _in_proj(x2, ada3, g_pre, pos2, w_main, w_tail, conv_w, g_conv, seq):
    n, d = x2.shape
    tm = min(ROW_TILE, seq)
    tiles_per_batch = seq // tm
    half_q = ROPE_DIM // 2
    half_i = IDX_ROPE_DIM // 2
    lane = np.arange(LANES)
    fq = jnp.asarray(ROPE_THETA, F32) ** (-jnp.asarray(lane % half_q, F32) / half_q)
    fi = jnp.asarray(ROPE_THETA, F32) ** (-jnp.asarray(lane % half_i, F32) / half_i)
    row = lambda i: (i, 0)
    fixed = lambda i: (0, 0)
    outs = [(D_CONV, BF16), (D_ATT, BF16), (D_KV, BF16), (D_KV, BF16), (D_QI, BF16), (LANES, BF16), (LANES, F32)]
    return pl.pallas_call(
        functools.partial(_in_proj_body, tiles_per_batch),
        grid=(n // tm,),
        in_specs=[pl.BlockSpec((tm, d), row),
                  pl.BlockSpec((None, 6, d), lambda i: (i // tiles_per_batch, 0, 0)),
                  pl.BlockSpec((1, d), fixed),
                  pl.BlockSpec((tm, 1), row),
                  pl.BlockSpec((d, C_KI), fixed, pipeline_mode=pl.Buffered(1)),
                  pl.BlockSpec((d, 2 * LANES), fixed),
                  pl.BlockSpec((CONV_WIDTH, D_CONV), fixed),
                  pl.BlockSpec((1, D_CONV), fixed),
                  pl.BlockSpec((1, LANES), fixed),
                  pl.BlockSpec((1, LANES), fixed)],
        out_specs=[pl.BlockSpec((tm, w), row) for w, _ in outs],
        out_shape=[jax.ShapeDtypeStruct((n, w), dt) for w, dt in outs],
        scratch_shapes=[pltpu.VMEM((tm + 8, D_CONV), F32)],
        compiler_params=_params("arbitrary"),
        name="in_proj",
    )(x2, ada3, g_pre, pos2, w_main, w_tail, conv_w, g_conv, fq.reshape(1, LANES), fi.reshape(1, LANES))


def _dsa_body(topk, nbits, q_ref, qi_ref, wi_ref, k_ref, v_ref, ki_ref, g_ref, o_ref,
              keys_ref, keys_t_ref, qm_ref, wb_ref, m_ref, l_ref, acc_ref):
    i = pl.program_id(1)
    tq = q_ref.shape[0]
    tk = tq
    nch = tk // LANES
    nkb = i + 1
    nt = (((1,), (1,)), ((), ()))

    lane = lax.broadcasted_iota(I32, (tq, LANES), 1)
    for p in range(IDX_HEADS // 2):
        qp = qi_ref[:, p * LANES:(p + 1) * LANES].astype(F32)
        qm_ref[(2 * p) * tq:(2 * p + 1) * tq, :] = jnp.where(lane < IDX_DIM, qp, 0.0).astype(BF16)
        qm_ref[(2 * p + 1) * tq:(2 * p + 2) * tq, :] = jnp.where(lane >= IDX_DIM, qp, 0.0).astype(BF16)
    w = wi_ref[...]
    for hh in range(IDX_HEADS):
        wb_ref[hh * tq:(hh + 1) * tq, :] = jnp.broadcast_to(w[:, hh:hh + 1], (tq, LANES))

    row = lax.broadcasted_iota(I32, (tq, tk), 0)
    col = lax.broadcasted_iota(I32, (tq, tk), 1)

    def score_block(kb, carry):
        k0 = pl.multiple_of(kb * tk, tk)
        kib = ki_ref[pl.ds(k0, tk), :]
        acc = jnp.zeros((tq, tk), F32)
        for j in range(IDX_HEADS // 4):
            d = lax.dot_general(qm_ref[4 * j * tq:4 * (j + 1) * tq, :], kib, nt, preferred_element_type=F32)
            wbj = wb_ref[4 * j * tq:4 * (j + 1) * tq, :]
            e = jnp.maximum(d, 0.0) * jnp.concatenate([wbj] * nch, axis=1)
            acc = acc + ((e[0:tq] + e[tq:2 * tq]) + (e[2 * tq:3 * tq] + e[3 * tq:4 * tq]))
        bits = lax.bitcast_convert_type(acc, I32)
        key = jnp.where(bits < 0, bits ^ 0x7FFFFFFF, bits)
        key = jnp.where(k0 + col <= i * tq + row, key, INT_MIN)
        keys_ref[kb] = key
        keys_t_ref[kb] = key.T
        return carry

    lax.fori_loop(0, nkb, score_block, 0)

    def count_t(pred):
        def body(kb, part):
            ind = jnp.where(pred(keys_t_ref[kb]), 1.0, 0.0)
            for s0 in range(0, tk, 8):
                part = part + ind[s0:s0 + 8]
            return part
        part = lax.fori_loop(0, nkb, body, jnp.zeros((8, tq), F32))
        return jnp.sum(part, axis=0, keepdims=True)

    kf = float(topk)
    c0 = count_t(lambda kk: kk >= 0)
    t0 = jnp.where(c0 >= kf, 0, INT_MIN)
    n0 = jnp.where(c0 >= kf, c0, -1.0)

    def value_bit(carry):
        bi, t, n_t = carry
        cand = t + lax.shift_left(jnp.int32(1), 30 - bi)
        cnt = count_t(lambda kk: kk >= cand)
        take = cnt >= kf
        return bi + 1, jnp.where(take, cand, t), jnp.where(take, cnt, n_t)

    def unresolved(carry):
        bi, _, n_t = carry
        return (bi < 31) & (jnp.max(jnp.abs(n_t - kf)) > 0.0)

    _, thr_t, n_ge = lax.while_loop(unresolved, value_bit, (jnp.int32(0), t0, n0))
    surplus = jnp.where(thr_t > INT_MIN, n_ge - kf, 0.0)
    thr = jnp.broadcast_to(thr_t, (tq, tq)).T[:, :LANES]

    def count(pred):
        def body(kb, part):
            kk = keys_ref[kb]
            for c in range(nch):
                part = part + jnp.where(pred(kk[:, c * LANES:(c + 1) * LANES], kb * tk + c * LANES), 1.0, 0.0)
            return part
        part = lax.fori_loop(0, nkb, body, jnp.zeros((tq, LANES), F32))
        return jnp.broadcast_to(jnp.sum(part, axis=1, keepdims=True), (tq, LANES))

    zero = jnp.zeros((tq, LANES), I32)

    @pl.when(jnp.max(surplus) > 0.0)
    def _():
        need = kf - count(lambda kk, _: kk > thr)

        def index_bit(bi, p):
            cand = p + lax.shift_left(jnp.int32(1), nbits - 1 - bi)
            f = count(lambda kk, base: (kk == thr) & (base + lane < cand))
            return jnp.where(f <= need, cand, p)

        pend = lax.fori_loop(0, nbits, index_bit, zero)

        def drop(kb, carry):
            kk = keys_ref[kb]
            parts = []
            for c in range(nch):
                kc = kk[:, c * LANES:(c + 1) * LANES]
                parts.append(jnp.where((kc == thr) & (kb * tk + c * LANES + lane >= pend), INT_MIN, kc))
            keys_ref[kb] = jnp.concatenate(parts, axis=1)
            return carry

        lax.fori_loop(0, nkb, drop, 0)

    tsel = jnp.maximum(thr, INT_MIN + 1)

    m_ref[...] = jnp.full(m_ref.shape, NEG_BIG, F32)
    l_ref[...] = jnp.zeros(l_ref.shape, F32)
    acc_ref[...] = jnp.zeros(acc_ref.shape, F32)

    def attend(kb, carry):
        k0 = pl.multiple_of(kb * tk, tk)
        kk = keys_ref[kb]
        sel = jnp.concatenate([jnp.where(kk[:, c * LANES:(c + 1) * LANES] >= tsel, 0.0, NEG_BIG)
                               for c in range(nch)], axis=1)
        for g in range(N_KV_HEADS):
            kg = k_ref[pl.ds(k0, tk), g * LANES:(g + 1) * LANES]
            vg = v_ref[pl.ds(k0, tk), g * LANES:(g + 1) * LANES]
            qg = jnp.concatenate([q_ref[:, (KV_REP * g + r) * LANES:(KV_REP * g + r + 1) * LANES]
                                  for r in range(KV_REP)], axis=0)
            s = lax.dot_general(qg, kg, nt, preferred_element_type=F32)
            s = jnp.concatenate([s[r * tq:(r + 1) * tq] + sel for r in range(KV_REP)], axis=0)
            m_old = m_ref[g]
            m_new = jnp.maximum(m_old, jnp.max(s, axis=1, keepdims=True))
            alpha = jnp.exp(m_old - m_new)
            p = jnp.exp(s - jnp.concatenate([m_new] * nch, axis=1))
            l_ref[g] = alpha * l_ref[g] + jnp.sum(p, axis=1, keepdims=True)
            acc_ref[g] = alpha * acc_ref[g] + jnp.dot(p.astype(BF16), vg, preferred_element_type=F32)
            m_ref[g] = m_new
        return carry

    lax.fori_loop(0, nkb, attend, 0)

    heads = []
    for g in range(N_KV_HEADS):
        og = acc_ref[g] / l_ref[g]
        heads += [og[r * tq:(r + 1) * tq] for r in range(KV_REP)]
    o_ref[...] = _rms(jnp.concatenate(heads, axis=1), g_ref[...]).astype(BF16)


def _dsa(q, qi, wi, k, v, ki, g_attn):
    bsz, seq, _ = q.shape
    tq = min(ATT_TILE, seq)
    nq = seq // tq
    topk = min(TOPK_MAX, seq // 4)
    nbits = int(seq).bit_length()
    qblk = lambda w: pl.BlockSpec((None, tq, w), lambda b, i: (b, i, 0))
    full = lambda w: pl.BlockSpec((None, seq, w), lambda b, i: (b, 0, 0), pipeline_mode=pl.Buffered(1))
    return pl.pallas_call(
        functools.partial(_dsa_body, topk, nbits),
        grid=(bsz, nq),
        in_specs=[qblk(D_ATT), qblk(D_QI), qblk(LANES), full(D_KV), full(D_KV), full(LANES),
                  pl.BlockSpec((1, D_ATT), lambda b, i: (0, 0))],
        out_specs=qblk(D_ATT),
        out_shape=jax.ShapeDtypeStruct((bsz, seq, D_ATT), BF16),
        scratch_shapes=[pltpu.VMEM((nq, tq, tq), I32),
                        pltpu.VMEM((nq, tq, tq), I32),
                        pltpu.VMEM((IDX_HEADS * tq, LANES), BF16),
                        pltpu.VMEM((IDX_HEADS * tq, LANES), F32),
                        pltpu.VMEM((N_KV_HEADS, KV_REP * tq, LANES), F32),
                        pltpu.VMEM((N_KV_HEADS, KV_REP * tq, LANES), F32),
                        pltpu.VMEM((N_KV_HEADS, KV_REP * tq, LANES), F32)],
        compiler_params=_params("arbitrary", "arbitrary"),
        name="dsa",
    )(q, qi, wi, k, v, ki, g_attn)


def _split_bf16(x):
    hi = x.astype(BF16)
    return hi, (x - hi.astype(F32)).astype(BF16)


def _out_proj_body(yc_ref, ya_ref, x_ref, ada_ref, wo_ref, gpm_ref, gpf_ref, wr_ref, br_ref,
                   x1_ref, h2_ref, idx_ref, gate_ref):
    tm = x_ref.shape[0]
    mix = (jnp.dot(yc_ref[...], wo_ref[0:D_CONV, :], preferred_element_type=F32)
           + jnp.dot(ya_ref[...], wo_ref[D_CONV:, :], preferred_element_type=F32))
    x1 = x_ref[...] + ada_ref[2:3, :] * _rms(mix, gpm_ref[...])
    x1_ref[...] = x1
    h2 = _rms(x1, gpf_ref[...]) * (1.0 + ada_ref[4:5, :]) + ada_ref[3:4, :]
    half = h2.shape[1] // 2
    lo = lax.bitcast_convert_type(h2[:, :half].astype(BF16).astype(F32), jnp.uint32)
    hi = lax.bitcast_convert_type(h2[:, half:].astype(BF16).astype(F32), jnp.uint32)
    h2_ref[...] = hi | lax.shift_right_logical(lo, jnp.uint32(16))

    h_hi, h_lo = _split_bf16(h2)
    w_hi, w_lo = _split_bf16(wr_ref[...])
    logits = (jnp.dot(h_hi, w_hi, preferred_element_type=F32) + jnp.dot(h_hi, w_lo, preferred_element_type=F32)
              + jnp.dot(h_lo, w_hi, preferred_element_type=F32)) + br_ref[...]
    lane = lax.broadcasted_iota(I32, (tm, LANES), 1).astype(F32)
    cur = jnp.where(lane < N_EXPERTS, logits, -jnp.inf)
    vals, idxs = [], []
    for _ in range(TOP_K):
        m = jnp.max(cur, axis=1, keepdims=True)
        am = jnp.min(jnp.where(cur == m, lane, float(LANES)), axis=1, keepdims=True)
        vals.append(m)
        idxs.append(am)
        cur = jnp.where(lane == am, -jnp.inf, cur)
    es = [jnp.exp(vv - vals[0]) for vv in vals]
    tot = es[0] + es[1] + es[2] + es[3]
    idx_out = jnp.zeros((tm, LANES), I32)
    gate_out = jnp.zeros((tm, LANES), F32)
    for kk in range(TOP_K):
        idx_out = jnp.where(lane == kk, idxs[kk].astype(I32), idx_out)
        gate_out = jnp.where(lane == kk, es[kk] / tot, gate_out)
    idx_ref[...] = idx_out
    gate_ref[...] = gate_out


def _out_proj(yc, ya, x2, ada3, w_out, g_post_mix, g_pre_ffn, w_router, b_router, seq):
    n, d = x2.shape
    tm = min(ROW_TILE, seq)
    tiles_per_batch = seq // tm
    row = lambda i: (i, 0)
    fixed = lambda i: (0, 0)
    return pl.pallas_call(
        _out_proj_body,
        grid=(n // tm,),
        in_specs=[pl.BlockSpec((tm, D_CONV), row),
                  pl.BlockSpec((tm, D_ATT), row),
                  pl.BlockSpec((tm, d), row),
                  pl.BlockSpec((None, 6, d), lambda i: (i // tiles_per_batch, 0, 0)),
                  pl.BlockSpec((D_CONV + D_ATT, d), fixed, pipeline_mode=pl.Buffered(1)),
                  pl.BlockSpec((1, d), fixed),
                  pl.BlockSpec((1, d), fixed),
                  pl.BlockSpec((d, LANES), fixed),
                  pl.BlockSpec((1, LANES), fixed)],
        out_specs=[pl.BlockSpec((tm, d), row), pl.BlockSpec((tm, d // 2), row),
                   pl.BlockSpec((tm, LANES), row), pl.BlockSpec((tm, LANES), row)],
        out_shape=[jax.ShapeDtypeStruct((n, d), F32), jax.ShapeDtypeStruct((n, d // 2), jnp.uint32),
                   jax.ShapeDtypeStruct((n, LANES), I32), jax.ShapeDtypeStruct((n, LANES), F32)],
        compiler_params=_params("arbitrary"),
        name="out_proj",
    )(yc, ya, x2, ada3, w_out, g_post_mix, g_pre_ffn, w_router, b_router)


def _rank_body(idx_ref, rank_ref, cnt_ref, carry):
    i = pl.program_id(0)
    tm = idx_ref.shape[0]

    @pl.when(i == 0)
    def _():
        carry[...] = jnp.zeros(carry.shape, F32)

    lane = lax.broadcasted_iota(I32, (tm, LANES), 1)
    r = lax.broadcasted_iota(I32, (tm, tm), 0)
    c = lax.broadcasted_iota(I32, (tm, tm), 1)
    before = jnp.where(c < r, 1.0, 0.0).astype(BF16)
    idx = idx_ref[...]
    base = carry[0:1, :]
    out = jnp.zeros((tm, LANES), F32)
    for kk in range(TOP_K):
        onehot = jnp.where(lane == idx[:, kk:kk + 1], 1.0, 0.0)
        prefix = jnp.dot(before, onehot.astype(BF16), preferred_element_type=F32) + base
        rk = jnp.sum(onehot * prefix, axis=1, keepdims=True)
        out = jnp.where(lane == kk, rk, out)
        base = base + jnp.sum(onehot, axis=0, keepdims=True)
    rank_ref[...] = out.astype(I32)
    carry[0:1, :] = base
    cnt_ref[...] = jnp.broadcast_to(base, cnt_ref.shape).astype(I32)


def _rank(top_idx):
    n = top_idx.shape[0]
    tm = min(ROW_TILE, n)
    return pl.pallas_call(
        _rank_body,
        grid=(n // tm,),
        in_specs=[pl.BlockSpec((tm, LANES), lambda i: (i, 0))],
        out_specs=[pl.BlockSpec((tm, LANES), lambda i: (i, 0)), pl.BlockSpec((8, LANES), lambda i: (0, 0))],
        out_shape=[jax.ShapeDtypeStruct((n, LANES), I32), jax.ShapeDtypeStruct((8, LANES), I32)],
        scratch_shapes=[pltpu.VMEM((8, LANES), F32)],
        compiler_params=_params("arbitrary"),
        name="rank",
    )(top_idx)


def _dispatch_body(dest_ref, valid_ref, h_ref, o_ref, zeros_ref, sem, zsem):
    i = pl.program_id(0)
    tm = h_ref.shape[0]
    blk = zeros_ref.shape[0]
    nb = o_ref.shape[0] // blk

    @pl.when(i == 0)
    def _():
        zeros_ref[...] = jnp.zeros(zeros_ref.shape, zeros_ref.dtype)

        def for_blocks(fn):
            for jb in range(nb):
                @pl.when(valid_ref[jb] < blk)
                def _():
                    fn(pltpu.make_async_copy(zeros_ref, o_ref.at[pl.ds(jb * blk, blk)], zsem))

        for_blocks(lambda cp: cp.start())
        for_blocks(lambda cp: cp.wait())

    def row_copy(r, kk):
        dst = dest_ref[(i * tm + r) * TOP_K + kk]
        return pltpu.make_async_copy(h_ref.at[pl.ds(r, 1)], o_ref.at[pl.ds(dst, 1)], sem)

    def issue(rb, carry):
        for jj in range(DMA_UNROLL):
            for kk in range(TOP_K):
                row_copy(rb * DMA_UNROLL + jj, kk).start()
        return carry

    lax.fori_loop(0, tm // DMA_UNROLL, issue, 0)
    for kk in range(TOP_K):
        pltpu.make_async_copy(h_ref, o_ref.at[pl.ds(0, tm)], sem).wait()


def _dispatch(dest_flat, block_valid, h2p, n_rows):
    n, dw = h2p.shape
    tm = min(ROW_TILE, n)
    return pl.pallas_call(
        _dispatch_body,
        grid_spec=pltpu.PrefetchScalarGridSpec(
            num_scalar_prefetch=2,
            grid=(n // tm,),
            in_specs=[pl.BlockSpec((tm, dw), lambda i, dst, bv: (i, 0))],
            out_specs=pl.BlockSpec(memory_space=pl.ANY),
            scratch_shapes=[pltpu.VMEM((MOE_BLOCK, dw), h2p.dtype), pltpu.SemaphoreType.DMA,
                            pltpu.SemaphoreType.DMA]),
        out_shape=jax.ShapeDtypeStruct((n_rows, dw), h2p.dtype),
        compiler_params=_params("arbitrary"),
        name="dispatch",
    )(dest_flat, block_valid, h2p)


def _experts_body(be_ref, ns_ref, rev_ref, nu_ref, x_ref, wgu_ref, wd_ref, bgu_ref, bd_ref, y_ref,
                  xlo_ref, xhi_ref):
    j = pl.program_id(0)
    f = pl.program_id(1)
    half = xlo_ref.shape[1]
    nsub = ns_ref[j]

    @pl.when(f == 0)
    def _():
        y_ref[...] = jnp.broadcast_to(bd_ref[...], y_ref.shape)

    @pl.when((f == 0) & (nsub > 0))
    def _():
        w = x_ref[...]
        xlo_ref[...] = lax.bitcast_convert_type(lax.shift_left(w, jnp.uint32(16)), F32).astype(BF16)
        xhi_ref[...] = lax.bitcast_convert_type(w & jnp.uint32(0xFFFF0000), F32).astype(BF16)

    def compute(m):
        gu = (jnp.dot(xlo_ref[0:m, :], wgu_ref[0:half, :].astype(BF16), preferred_element_type=F32)
              + jnp.dot(xhi_ref[0:m, :], wgu_ref[half:, :].astype(BF16), preferred_element_type=F32)) + bgu_ref[...]
        gate = jnp.minimum(gu, SWIGLU_LIMIT)
        glu = gate * jax.nn.sigmoid(SWIGLU_ALPHA * gate)
        up1 = jnp.clip(gu, -SWIGLU_LIMIT, SWIGLU_LIMIT) + 1.0
        n2 = gu.shape[1]
        rows2 = lax.broadcasted_iota(I32, (2 * LANES, LANES), 0)
        cols2 = lax.broadcasted_iota(I32, (2 * LANES, LANES), 1)
        sel = jnp.where(rows2 == 2 * cols2, 1.0, 0.0).astype(BF16)
        parts = []
        for c in range(n2 // LANES):
            sl = slice(c * LANES, (c + 1) * LANES)
            parts.append((glu[:, sl] * pltpu.roll(up1[:, sl], LANES - 1, 1)).astype(BF16))
        acts = [jnp.dot(jnp.concatenate(parts[2 * c:2 * c + 2], axis=1), sel, preferred_element_type=F32)
                for c in range(n2 // (2 * LANES))]
        act = jnp.concatenate(acts, axis=1).astype(BF16)
        y_ref[0:m, :] += jnp.dot(act, wd_ref[...].astype(BF16), preferred_element_type=F32)

    for ns in range(1, MOE_BLOCK // MOE_SUB + 1):
        @pl.when(nsub == ns)
        def _():
            compute(ns * MOE_SUB)


def _experts(block_e, block_nsub, block_rev, n_used, x_rows, w_gate_up, w_down, b_gate_up, b_down):
    p, dw = x_rows.shape
    d = 2 * dw
    dff = w_down.shape[1]
    nb = p // MOE_BLOCK
    nf = dff // FF_TILE

    def blk(j, f, be, ns, rev, nu):
        return jnp.minimum(j, nu[0] - 1)

    def fidx(j, f, be, ns, rev, nu):
        back = rev[jnp.minimum(j, nu[0] - 1)] == 1
        return jnp.where(j < nu[0], jnp.where(back, nf - 1 - f, f), jnp.where(back, 0, nf - 1))

    return pl.pallas_call(
        _experts_body,
        grid_spec=pltpu.PrefetchScalarGridSpec(
            num_scalar_prefetch=4,
            grid=(nb, nf),
            in_specs=[pl.BlockSpec((MOE_BLOCK, dw), lambda j, f, *s: (blk(j, f, *s), 0)),
                      pl.BlockSpec((None, d, 2 * FF_TILE), lambda j, f, *s: (s[0][j], 0, fidx(j, f, *s))),
                      pl.BlockSpec((None, FF_TILE, d), lambda j, f, *s: (s[0][j], fidx(j, f, *s), 0)),
                      pl.BlockSpec((None, 1, 2 * FF_TILE), lambda j, f, *s: (s[0][j], 0, fidx(j, f, *s))),
                      pl.BlockSpec((None, 1, d), lambda j, f, *s: (s[0][j], 0, 0))],
            out_specs=pl.BlockSpec((MOE_BLOCK, d), lambda j, f, *s: (j, 0)),
            scratch_shapes=[pltpu.VMEM((MOE_BLOCK, dw), BF16), pltpu.VMEM((MOE_BLOCK, dw), BF16)]),
        out_shape=jax.ShapeDtypeStruct((p, d), F32),
        compiler_params=_params("arbitrary", "arbitrary"),
        name="experts",
    )(block_e, block_nsub, block_rev, n_used, x_rows, w_gate_up, w_down, b_gate_up, b_down)


def _combine_body(tiles_per_batch, dest_ref, y_ref, gate_ref, x1_ref, ada_ref, g_ref, o_ref, buf, sem):
    i = pl.program_id(0)
    n_tiles = pl.num_programs(0)
    tm = x1_ref.shape[0]

    def row_copy(tile, slot, r, kk):
        src = dest_ref[(tile * tm + r) * TOP_K + kk]
        return pltpu.make_async_copy(y_ref.at[pl.ds(src, 1)], buf.at[slot, kk, pl.ds(r, 1)], sem.at[slot])

    def issue(tile, slot):
        def body(rb, carry):
            for jj in range(DMA_UNROLL):
                for kk in range(TOP_K):
                    row_copy(tile, slot, rb * DMA_UNROLL + jj, kk).start()
            return carry
        lax.fori_loop(0, tm // DMA_UNROLL, body, 0)

    def drain(tile, slot):
        for kk in range(TOP_K):
            pltpu.make_async_copy(y_ref.at[pl.ds(0, tm)], buf.at[slot, kk], sem.at[slot]).wait()

    slot = i % 2

    @pl.when(i == 0)
    def _():
        issue(0, 0)

    @pl.when(i + 1 < n_tiles)
    def _():
        issue(i + 1, 1 - slot)

    drain(i, slot)
    gates = gate_ref[...]
    y = jnp.zeros(x1_ref.shape, F32)
    for kk in range(TOP_K):
        y = y + gates[:, kk:kk + 1] * buf[slot, kk]
    o_ref[...] = x1_ref[...] + ada_ref[5:6, :] * _rms(y, g_ref[...])


def _combine(dest_flat, y_rows, gates, x1, ada3, g_post_ffn, seq):
    n, d = x1.shape
    tm = min(GATHER_TILE, seq)
    tiles_per_batch = seq // tm
    return pl.pallas_call(
        functools.partial(_combine_body, tiles_per_batch),
        grid_spec=pltpu.PrefetchScalarGridSpec(
            num_scalar_prefetch=1,
            grid=(n // tm,),
            in_specs=[pl.BlockSpec(memory_space=pl.ANY),
                      pl.BlockSpec((tm, LANES), lambda i, dst: (i, 0)),
                      pl.BlockSpec((tm, d), lambda i, dst: (i, 0)),
                      pl.BlockSpec((None, 6, d), lambda i, dst: (i // tiles_per_batch, 0, 0)),
                      pl.BlockSpec((1, d), lambda i, dst: (0, 0))],
            out_specs=pl.BlockSpec((tm, d), lambda i, dst: (i, 0)),
            scratch_shapes=[pltpu.VMEM((2, TOP_K, tm, d), F32), pltpu.SemaphoreType.DMA((2,))]),
        out_shape=jax.ShapeDtypeStruct((n, d), F32),
        compiler_params=_params("arbitrary"),
        name="combine",
    )(dest_flat, y_rows, gates, x1, ada3, g_post_ffn)


def _tail_w_in(w_in):
    d = w_in.shape[0]
    o_wi = C_KI + IDX_DIM
    ki = w_in[:, C_KI:o_wi]
    wi = w_in[:, o_wi:o_wi + IDX_HEADS]
    pad = jnp.zeros((d, LANES - IDX_HEADS), w_in.dtype)
    return jnp.concatenate([ki, ki, wi, pad], axis=1)


def _layer(x, c, positions, w_ada, b_ada, g_pre_mix, g_post_mix, w_in, conv_w, g_conv_out, g_attn_out, w_out,
           g_pre_ffn, g_post_ffn, w_router, b_router, w_gate_up, b_gate_up, w_down, b_down):
    bsz, seq, d = x.shape
    n = bsz * seq
    x2 = x.reshape(n, d)
    ada3 = _ada(c, w_ada, b_ada).reshape(bsz, 6, d)
    w_in_b = w_in.astype(BF16)

    yc, q, k, v, qi, ki, wi = _in_proj(
        x2, ada3, g_pre_mix.reshape(1, d), positions.reshape(n, 1).astype(I32), w_in_b, _tail_w_in(w_in_b),
        conv_w.reshape(CONV_WIDTH, D_CONV), g_conv_out.reshape(1, D_CONV), seq)
    b3 = lambda a: a.reshape(bsz, seq, a.shape[-1])
    ya = _dsa(b3(q), b3(qi), b3(wi), b3(k), b3(v), b3(ki), g_attn_out.reshape(1, D_ATT)).reshape(n, D_ATT)

    wr = jnp.zeros((d, LANES), F32).at[:, :N_EXPERTS].set(w_router)
    br = jnp.zeros((1, LANES), F32).at[0, :N_EXPERTS].set(b_router)
    x1, h2, top_idx, gates = _out_proj(yc, ya, x2, ada3, w_out.astype(BF16), g_post_mix.reshape(1, d),
                                       g_pre_ffn.reshape(1, d), wr, br, seq)

    rank, cnt = _rank(top_idx)
    counts = cnt[0, :N_EXPERTS]
    padded = (counts + MOE_BLOCK - 1) // MOE_BLOCK * MOE_BLOCK
    pad_ends = jnp.cumsum(padded)
    pad_starts = pad_ends - padded
    nb = -(-(n * TOP_K) // MOE_BLOCK) + N_EXPERTS
    dest = (pad_starts[top_idx[:, :TOP_K]] + rank[:, :TOP_K]).reshape(n * TOP_K).astype(I32)
    block_e = jnp.minimum(jnp.searchsorted(pad_ends, jnp.arange(nb, dtype=I32) * MOE_BLOCK, side='right'),
                          N_EXPERTS - 1).astype(I32)
    n_used = (pad_ends[-1:] // MOE_BLOCK).astype(I32)
    block_start = jnp.arange(nb, dtype=I32) * MOE_BLOCK
    block_valid = jnp.clip(counts[block_e] - (block_start - pad_starts[block_e]), 0, MOE_BLOCK)
    block_valid = jnp.where(block_start < pad_ends[-1], block_valid, 0).astype(I32)
    block_nsub = (block_valid + MOE_SUB - 1) // MOE_SUB
    block_e = jnp.where(block_start < pad_ends[-1], block_e, block_e[n_used[0] - 1])
    block_rev = ((block_start - pad_starts[block_e]) // MOE_BLOCK) % 2

    x_rows = _dispatch(dest, block_valid, h2, nb * MOE_BLOCK)
    y_rows = _experts(block_e, block_nsub, block_rev.astype(I32), n_used, x_rows, w_gate_up, w_down,
                      b_gate_up[:, None, :], b_down[:, None, :])
    out = _combine(dest, y_rows, gates, x1, ada3, g_post_ffn.reshape(1, d), seq)
    return out.reshape(bsz, seq, d)


def kernel(x, c, positions, w_ada, b_ada, g_pre_mix, g_post_mix, w_in, conv_w, g_conv_out, g_attn_out, w_out,
           g_pre_ffn, g_post_ffn, w_router, b_router, w_gate_up, b_gate_up, w_down, b_down):
    for l in range(w_ada.shape[0]):
        x = _layer(x, c, positions, w_ada[l], b_ada[l], g_pre_mix[l], g_post_mix[l], w_in[l], conv_w[l],
                   g_conv_out[l], g_attn_out[l], w_out[l], g_pre_ffn[l], g_post_ffn[l], w_router[l], b_router[l],
                   w_gate_up[l], b_gate_up[l], w_down[l], b_down[l])
    return x
```

```python
import functools

import numpy as np
import jax
import jax.numpy as jnp
from jax import lax
from jax.experimental import pallas as pl
from jax.experimental.pallas import tpu as pltpu

F32 = jnp.float32
BF16 = jnp.bfloat16
I32 = jnp.int32

EPS = 1e-6
LANES = 128
D_CONV = 1024
CONV_WIDTH = 3
N_HEADS = 8
N_KV_HEADS = 2
HEAD_DIM = 128
KV_REP = N_HEADS // N_KV_HEADS
ROPE_DIM = HEAD_DIM // 4
ROPE_THETA = 500000.0
IDX_HEADS = 16
IDX_DIM = 64
IDX_ROPE_DIM = IDX_DIM // 4
TOPK_MAX = 256
N_EXPERTS = 32
TOP_K = 4
SWIGLU_LIMIT = 7.0
SWIGLU_ALPHA = 1.702

D_ATT = N_HEADS * HEAD_DIM
D_KV = N_KV_HEADS * HEAD_DIM
D_QI = IDX_HEADS * IDX_DIM

INT_MIN = -2147483648
NEG_BIG = -1e30

ROW_TILE = 512
ATT_TILE = 512
MOE_BLOCK = 512
MOE_SUB = 256
FF_TILE = 512
GATHER_TILE = 128
DMA_UNROLL = 8
VMEM_LIMIT = 56 * 1024 * 1024


def _params(*sem):
    return pltpu.CompilerParams(dimension_semantics=sem, vmem_limit_bytes=VMEM_LIMIT)


def _rms(x, g):
    return x * lax.rsqrt(jnp.mean(x * x, axis=-1, keepdims=True) + EPS) * g


def _ada_body(c_ref, w_ref, b_ref, o_ref):
    c = c_ref[...]
    s = (c * jax.nn.sigmoid(c)).astype(BF16)
    o_ref[...] = jnp.dot(s, w_ref[...].astype(BF16), preferred_element_type=F32) + b_ref[...]


def _ada(c, w, b):
    bsz, d = c.shape
    n = w.shape[1]
    tn = 1536
    rows = 8
    cp = jnp.zeros((rows, d), F32).at[:bsz].set(c)
    out = pl.pallas_call(
        _ada_body,
        grid=(n // tn,),
        in_specs=[pl.BlockSpec((rows, d), lambda j: (0, 0)),
                  pl.BlockSpec((d, tn), lambda j: (0, j)),
                  pl.BlockSpec((1, tn), lambda j: (0, j))],
        out_specs=pl.BlockSpec((rows, tn), lambda j: (0, j)),
        out_shape=jax.ShapeDtypeStruct((rows, n), F32),
        compiler_params=_params("arbitrary"),
        name="ada",
    )(cp, w, b.reshape(1, n))
    return out[:bsz]


C_BCU = 0
C_Q = 3 * D_CONV
C_K = C_Q + D_ATT
C_V = C_K + D_KV
C_QI = C_V + D_KV
C_KI = C_QI + D_QI


def _rope_tables(pos, inv_freq, width, rot):
    half = rot // 2
    rows = pos.shape[0]
    lane = lax.broadcasted_iota(I32, (rows, LANES), 1) & (width - 1)
    ang = pos * inv_freq
    cos = jnp.cos(ang)
    sin = jnp.sin(ang)
    c = jnp.where(lane < rot, cos, 1.0)
    a = jnp.where((lane >= half) & (lane < rot), sin, 0.0)
    b = jnp.where(lane < half, -sin, 0.0)
    return c, a, b, half


def _rope(x, tabs):
    c, a, b, half = tabs
    return x * c + pltpu.roll(x, half, 1) * a + pltpu.roll(x, LANES - half, 1) * b


def _in_proj_body(tiles_per_batch, x_ref, ada_ref, g_ref, pos_ref, w_ref, wt_ref, cw_ref, gc_ref, fq_ref, fi_ref,
                  yc_ref, q_ref, k_ref, v_ref, qi_ref, ki_ref, wi_ref, vbuf):
    i = pl.program_id(0)
    tm = x_ref.shape[0]
    h = _rms(x_ref[...], g_ref[...]) * (1.0 + ada_ref[1:2, :]) + ada_ref[0:1, :]
    hb = h.astype(BF16)

    def proj(lo, hi):
        return jnp.dot(hb, w_ref[:, lo:hi], preferred_element_type=F32)

    @pl.when(i % tiles_per_batch == 0)
    def _():
        vbuf[0:8, :] = jnp.zeros((8, D_CONV), F32)

    bcu = proj(C_BCU, C_Q)
    v0 = bcu[:, D_CONV:2 * D_CONV] * bcu[:, 2 * D_CONV:]
    vbuf[8:8 + tm, :] = v0
    v1 = vbuf[7:7 + tm, :]
    v2 = vbuf[6:6 + tm, :]
    y = cw_ref[0:1, :] * v2 + cw_ref[1:2, :] * v1 + cw_ref[2:3, :] * v0
    yc_ref[...] = _rms(bcu[:, :D_CONV] * y, gc_ref[...]).astype(BF16)
    vbuf[0:8, :] = vbuf[tm:tm + 8, :]

    pos = pos_ref[...].astype(F32)
    tq = _rope_tables(pos, fq_ref[...], HEAD_DIM, ROPE_DIM)
    ti = _rope_tables(pos, fi_ref[...], IDX_DIM, IDX_ROPE_DIM)
    scale = HEAD_DIM ** -0.5

    qf = proj(C_Q, C_K)
    for hh in range(N_HEADS):
        sl = slice(hh * LANES, (hh + 1) * LANES)
        q_ref[:, sl] = (_rope(qf[:, sl], tq) * scale).astype(BF16)
    kf = proj(C_K, C_V)
    for hh in range(N_KV_HEADS):
        sl = slice(hh * LANES, (hh + 1) * LANES)
        k_ref[:, sl] = _rope(kf[:, sl], tq).astype(BF16)
    v_ref[...] = proj(C_V, C_QI).astype(BF16)
    qif = proj(C_QI, C_KI)
    for hh in range(D_QI // LANES):
        sl = slice(hh * LANES, (hh + 1) * LANES)
        qi_ref[:, sl] = _rope(qif[:, sl], ti).astype(BF16)
    tail = jnp.dot(hb, wt_ref[...], preferred_element_type=F32)
    ki_ref[...] = _rope(tail[:, :LANES], ti).astype(BF16)
    wi_ref[...] = tail[:, LANES:]


def _in_proj(x2, ada3, g_pre, pos2, w_main, w_tail, conv_w, g_conv, seq):
    n, d = x2.shape
    tm = min(ROW_TILE, seq)
    tiles_per_batch = seq // tm
    half_q = ROPE_DIM // 2
    half_i = IDX_ROPE_DIM // 2
    lane = np.arange(LANES)
    fq = jnp.asarray(ROPE_THETA, F32) ** (-jnp.asarray(lane % half_q, F32) / half_q)
    fi = jnp.asarray(ROPE_THETA, F32) ** (-jnp.asarray(lane % half_i, F32) / half_i)
    row = lambda i: (i, 0)
    fixed = lambda i: (0, 0)
    outs = [(D_CONV, BF16), (D_ATT, BF16), (D_KV, BF16), (D_KV, BF16), (D_QI, BF16), (LANES, BF16), (LANES, F32)]
    return pl.pallas_call(
        functools.partial(_in_proj_body, tiles_per_batch),
        grid=(n // tm,),
        in_specs=[pl.BlockSpec((tm, d), row),
                  pl.BlockSpec((None, 6, d), lambda i: (i // tiles_per_batch, 0, 0)),
                  pl.BlockSpec((1, d), fixed),
                  pl.BlockSpec((tm, 1), row),
                  pl.BlockSpec((d, C_KI), fixed, pipeline_mode=pl.Buffered(1)),
                  pl.BlockSpec((d, 2 * LANES), fixed),
                  pl.BlockSpec((CONV_WIDTH, D_CONV), fixed),
                  pl.BlockSpec((1, D_CONV), fixed),
                  pl.BlockSpec((1, LANES), fixed),
                  pl.BlockSpec((1, LANES), fixed)],
        out_specs=[pl.BlockSpec((tm, w), row) for w, _ in outs],
        out_shape=[jax.ShapeDtypeStruct((n, w), dt) for w, dt in outs],
        scratch_shapes=[pltpu.VMEM((tm + 8, D_CONV), F32)],
        compiler_params=_params("arbitrary"),
        name="in_proj",
    )(x2, ada3, g_pre, pos2, w_main, w_tail, conv_w, g_conv, fq.reshape(1, LANES), fi.reshape(1, LANES))


PLANE_KEYS = 256


def _bit_transpose32(words):
    a = list(words)
    j, m = 16, 0x0000FFFF
    while j:
        k = 0
        while k < 32:
            t = ((a[k] >> jnp.uint32(j)) ^ a[k + j]) & jnp.uint32(m)
            a[k] = a[k] ^ (t << jnp.uint32(j))
            a[k + j] = a[k + j] ^ t
            k = (k + j + 1) & ~j
        j >>= 1
        m ^= (m << j) & 0xFFFFFFFF
    return a

def _dsa_body(topk, nbits, q_ref, qi_ref, wi_ref, k_ref, v_ref, ki_ref, g_ref, o_ref,
              keys_ref, planes_ref, cand_ref, qm_ref, wb_ref, m_ref, l_ref, acc_ref):
    i = pl.program_id(1)
    tq = q_ref.shape[0]
    tk = tq
    nch = tk // LANES
    ngrp = tk // PLANE_KEYS
    nkb = i + 1
    nt = (((1,), (1,)), ((), ()))

    lane = lax.broadcasted_iota(I32, (tq, LANES), 1)
    for p in range(IDX_HEADS // 2):
        qp = qi_ref[:, p * LANES:(p + 1) * LANES].astype(F32)
        qm_ref[(2 * p) * tq:(2 * p + 1) * tq, :] = jnp.where(lane < IDX_DIM, qp, 0.0).astype(BF16)
        qm_ref[(2 * p + 1) * tq:(2 * p + 2) * tq, :] = jnp.where(lane >= IDX_DIM, qp, 0.0).astype(BF16)
    w = wi_ref[...]
    for hh in range(IDX_HEADS):
        wb_ref[hh * tq:(hh + 1) * tq, :] = jnp.broadcast_to(w[:, hh:hh + 1], (tq, LANES))

    row = lax.broadcasted_iota(I32, (tq, tk), 0)
    col = lax.broadcasted_iota(I32, (tq, tk), 1)

    def score_block(kb, carry):
        k0 = pl.multiple_of(kb * tk, tk)
        kib = ki_ref[pl.ds(k0, tk), :]
        acc = jnp.zeros((tq, tk), F32)
        for j in range(IDX_HEADS // 4):
            d = lax.dot_general(qm_ref[4 * j * tq:4 * (j + 1) * tq, :], kib, nt, preferred_element_type=F32)
            wbj = wb_ref[4 * j * tq:4 * (j + 1) * tq, :]
            e = jnp.maximum(d, 0.0) * jnp.concatenate([wbj] * nch, axis=1)
            acc = acc + ((e[0:tq] + e[tq:2 * tq]) + (e[2 * tq:3 * tq] + e[3 * tq:4 * tq]))
        bits = lax.bitcast_convert_type(acc, I32)
        key = jnp.where(bits < 0, bits ^ 0x7FFFFFFF, bits)
        key = jnp.where(k0 + col <= i * tq + row, key, INT_MIN)
        keys_ref[kb] = key
        ku = lax.bitcast_convert_type(key.T, jnp.uint32) ^ jnp.uint32(0x80000000)
        for ch in range(ngrp):
            base = ch * PLANE_KEYS
            planes = _bit_transpose32([ku[base + 8 * w:base + 8 * w + 8, :] for w in range(32)])
            for b in range(32):
                planes_ref[kb, ch, b] = planes[b]
            cand_ref[kb, ch] = jnp.full((8, tq), 0xFFFFFFFF, jnp.uint32)
        return carry

    lax.fori_loop(0, nkb, score_block, 0)

    def popcount_sum(word_fn):
        def body(kb, part):
            for ch in range(ngrp):
                part = part + lax.population_count(word_fn(kb, ch))
            return part
        part = lax.fori_loop(0, nkb, body, jnp.zeros((8, tq), jnp.uint32))
        return jnp.sum(part.astype(I32).astype(F32), axis=0, keepdims=True)

    def select_bit(bi, carry):
        t_u, need = carry
        b = 31 - bi
        ones = popcount_sum(lambda kb, ch: cand_ref[kb, ch] & planes_ref[kb, ch, b])
        take = ones >= need
        flip = jnp.where(take, jnp.uint32(0), jnp.uint32(0xFFFFFFFF))

        def narrow(kb, c):
            for ch in range(ngrp):
                cand_ref[kb, ch] = cand_ref[kb, ch] & (planes_ref[kb, ch, b] ^ flip)
            return c

        lax.fori_loop(0, nkb, narrow, 0)
        bit = lax.shift_left(jnp.uint32(1), jnp.asarray(b).astype(jnp.uint32))
        return jnp.where(take, t_u | bit, t_u), jnp.where(take, need, need - ones)

    kf = float(topk)
    t_u, need_t = lax.fori_loop(0, 32, select_bit,
                                (jnp.zeros((1, tq), jnp.uint32), jnp.full((1, tq), kf, F32)))
    thr_t = lax.bitcast_convert_type(t_u ^ jnp.uint32(0x80000000), I32)
    n_tie = popcount_sum(lambda kb, ch: cand_ref[kb, ch])
    surplus = jnp.where(thr_t > INT_MIN, n_tie - need_t, 0.0)
    thr = jnp.broadcast_to(thr_t, (tq, tq)).T[:, :LANES]

    def count(pred):
        def body(kb, part):
            kk = keys_ref[kb]
            for c in range(nch):
                part = part + jnp.where(pred(kk[:, c * LANES:(c + 1) * LANES], kb * tk + c * LANES), 1.0, 0.0)
            return part
        part = lax.fori_loop(0, nkb, body, jnp.zeros((tq, LANES), F32))
        return jnp.broadcast_to(jnp.sum(part, axis=1, keepdims=True), (tq, LANES))

    zero = jnp.zeros((tq, LANES), I32)

    @pl.when(jnp.max(surplus) > 0.0)
    def _():
        need = kf - count(lambda kk, _: kk > thr)

        def index_bit(bi, p):
            cand = p + lax.shift_left(jnp.int32(1), nbits - 1 - bi)
            f = count(lambda kk, base: (kk == thr) & (base + lane < cand))
            return jnp.where(f <= need, cand, p)

        pend = lax.fori_loop(0, nbits, index_bit, zero)

        def drop(kb, carry):
            kk = keys_ref[kb]
            parts = []
            for c in range(nch):
                kc = kk[:, c * LANES:(c + 1) * LANES]
                parts.append(jnp.where((kc == thr) & (kb * tk + c * LANES + lane >= pend), INT_MIN, kc))
            keys_ref[kb] = jnp.concatenate(parts, axis=1)
            return carry

        lax.fori_loop(0, nkb, drop, 0)

    tsel = jnp.maximum(thr, INT_MIN + 1)

    m_ref[...] = jnp.full(m_ref.shape, NEG_BIG, F32)
    l_ref[...] = jnp.zeros(l_ref.shape, F32)
    acc_ref[...] = jnp.zeros(acc_ref.shape, F32)

    def attend(kb, carry):
        k0 = pl.multiple_of(kb * tk, tk)
        kk = keys_ref[kb]
        sel = jnp.concatenate([jnp.where(kk[:, c * LANES:(c + 1) * LANES] >= tsel, 0.0, NEG_BIG)
                               for c in range(nch)], axis=1)
        for g in range(N_KV_HEADS):
            kg = k_ref[pl.ds(k0, tk), g * LANES:(g + 1) * LANES]
            vg = v_ref[pl.ds(k0, tk), g * LANES:(g + 1) * LANES]
            qg = jnp.concatenate([q_ref[:, (KV_REP * g + r) * LANES:(KV_REP * g + r + 1) * LANES]
                                  for r in range(KV_REP)], axis=0)
            s = lax.dot_general(qg, kg, nt, preferred_element_type=F32)
            s = jnp.concatenate([s[r * tq:(r + 1) * tq] + sel for r in range(KV_REP)], axis=0)
            m_old = m_ref[g]
            m_new = jnp.maximum(m_old, jnp.max(s, axis=1, keepdims=True))
            alpha = jnp.exp(m_old - m_new)
            p = jnp.exp(s - jnp.concatenate([m_new] * nch, axis=1))
            l_ref[g] = alpha * l_ref[g] + jnp.sum(p, axis=1, keepdims=True)
            acc_ref[g] = alpha * acc_ref[g] + jnp.dot(p.astype(BF16), vg, preferred_element_type=F32)
            m_ref[g] = m_new
        return carry

    lax.fori_loop(0, nkb, attend, 0)

    heads = []
    for g in range(N_KV_HEADS):
        og = acc_ref[g] / l_ref[g]
        heads += [og[r * tq:(r + 1) * tq] for r in range(KV_REP)]
    o_ref[...] = _rms(jnp.concatenate(heads, axis=1), g_ref[...]).astype(BF16)


def _dsa(q, qi, wi, k, v, ki, g_attn):
    bsz, seq, _ = q.shape
    tq = min(ATT_TILE, seq)
    nq = seq // tq
    topk = min(TOPK_MAX, seq // 4)
    nbits = int(seq).bit_length()
    qblk = lambda w: pl.BlockSpec((None, tq, w), lambda b, i: (b, i, 0))
    full = lambda w: pl.BlockSpec((None, seq, w), lambda b, i: (b, 0, 0), pipeline_mode=pl.Buffered(1))
    return pl.pallas_call(
        functools.partial(_dsa_body, topk, nbits),
        grid=(bsz, nq),
        in_specs=[qblk(D_ATT), qblk(D_QI), qblk(LANES), full(D_KV), full(D_KV), full(LANES),
                  pl.BlockSpec((1, D_ATT), lambda b, i: (0, 0))],
        out_specs=qblk(D_ATT),
        out_shape=jax.ShapeDtypeStruct((bsz, seq, D_ATT), BF16),
        scratch_shapes=[pltpu.VMEM((nq, tq, tq), I32),
                        pltpu.VMEM((nq, tq // PLANE_KEYS, 32, 8, tq), jnp.uint32),
                        pltpu.VMEM((nq, tq // PLANE_KEYS, 8, tq), jnp.uint32),
                        pltpu.VMEM((IDX_HEADS * tq, LANES), BF16),
                        pltpu.VMEM((IDX_HEADS * tq, LANES), F32),
                        pltpu.VMEM((N_KV_HEADS, KV_REP * tq, LANES), F32),
                        pltpu.VMEM((N_KV_HEADS, KV_REP * tq, LANES), F32),
                        pltpu.VMEM((N_KV_HEADS, KV_REP * tq, LANES), F32)],
        compiler_params=_params("arbitrary", "arbitrary"),
        name="dsa",
    )(q, qi, wi, k, v, ki, g_attn)


def _split_bf16(x):
    hi = x.astype(BF16)
    return hi, (x - hi.astype(F32)).astype(BF16)


def _out_proj_body(yc_ref, ya_ref, x_ref, ada_ref, wo_ref, gpm_ref, gpf_ref, wr_ref, br_ref,
                   x1_ref, h2_ref, idx_ref, gate_ref):
    tm = x_ref.shape[0]
    mix = (jnp.dot(yc_ref[...], wo_ref[0:D_CONV, :], preferred_element_type=F32)
           + jnp.dot(ya_ref[...], wo_ref[D_CONV:, :], preferred_element_type=F32))
    x1 = x_ref[...] + ada_ref[2:3, :] * _rms(mix, gpm_ref[...])
    x1_ref[...] = x1
    h2 = _rms(x1, gpf_ref[...]) * (1.0 + ada_ref[4:5, :]) + ada_ref[3:4, :]
    half = h2.shape[1] // 2
    lo = lax.bitcast_convert_type(h2[:, :half].astype(BF16).astype(F32), jnp.uint32)
    hi = lax.bitcast_convert_type(h2[:, half:].astype(BF16).astype(F32), jnp.uint32)
    h2_ref[...] = hi | lax.shift_right_logical(lo, jnp.uint32(16))

    h_hi, h_lo = _split_bf16(h2)
    w_hi, w_lo = _split_bf16(wr_ref[...])
    logits = (jnp.dot(h_hi, w_hi, preferred_element_type=F32) + jnp.dot(h_hi, w_lo, preferred_element_type=F32)
              + jnp.dot(h_lo, w_hi, preferred_element_type=F32)) + br_ref[...]
    lane = lax.broadcasted_iota(I32, (tm, LANES), 1).astype(F32)
    cur = jnp.where(lane < N_EXPERTS, logits, -jnp.inf)
    vals, idxs = [], []
    for _ in range(TOP_K):
        m = jnp.max(cur, axis=1, keepdims=True)
        am = jnp.min(jnp.where(cur == m, lane, float(LANES)), axis=1, keepdims=True)
        vals.append(m)
        idxs.append(am)
        cur = jnp.where(lane == am, -jnp.inf, cur)
    es = [jnp.exp(vv - vals[0]) for vv in vals]
    tot = es[0] + es[1] + es[2] + es[3]
    idx_out = jnp.zeros((tm, LANES), I32)
    gate_out = jnp.zeros((tm, LANES), F32)
    for kk in range(TOP_K):
        idx_out = jnp.where(lane == kk, idxs[kk].astype(I32), idx_out)
        gate_out = jnp.where(lane == kk, es[kk] / tot, gate_out)
    idx_ref[...] = idx_out
    gate_ref[...] = gate_out


def _out_proj(yc, ya, x2, ada3, w_out, g_post_mix, g_pre_ffn, w_router, b_router, seq):
    n, d = x2.shape
    tm = min(ROW_TILE, seq)
    tiles_per_batch = seq // tm
    row = lambda i: (i, 0)
    fixed = lambda i: (0, 0)
    return pl.pallas_call(
        _out_proj_body,
        grid=(n // tm,),
        in_specs=[pl.BlockSpec((tm, D_CONV), row),
                  pl.BlockSpec((tm, D_ATT), row),
                  pl.BlockSpec((tm, d), row),
                  pl.BlockSpec((None, 6, d), lambda i: (i // tiles_per_batch, 0, 0)),
                  pl.BlockSpec((D_CONV + D_ATT, d), fixed, pipeline_mode=pl.Buffered(1)),
                  pl.BlockSpec((1, d), fixed),
                  pl.BlockSpec((1, d), fixed),
                  pl.BlockSpec((d, LANES), fixed),
                  pl.BlockSpec((1, LANES), fixed)],
        out_specs=[pl.BlockSpec((tm, d), row), pl.BlockSpec((tm, d // 2), row),
                   pl.BlockSpec((tm, LANES), row), pl.BlockSpec((tm, LANES), row)],
        out_shape=[jax.ShapeDtypeStruct((n, d), F32), jax.ShapeDtypeStruct((n, d // 2), jnp.uint32),
                   jax.ShapeDtypeStruct((n, LANES), I32), jax.ShapeDtypeStruct((n, LANES), F32)],
        compiler_params=_params("arbitrary"),
        name="out_proj",
    )(yc, ya, x2, ada3, w_out, g_post_mix, g_pre_ffn, w_router, b_router)


def _rank_body(idx_ref, rank_ref, cnt_ref, carry):
    i = pl.program_id(0)
    tm = idx_ref.shape[0]

    @pl.when(i == 0)
    def _():
        carry[...] = jnp.zeros(carry.shape, F32)

    lane = lax.broadcasted_iota(I32, (tm, LANES), 1)
    r = lax.broadcasted_iota(I32, (tm, tm), 0)
    c = lax.broadcasted_iota(I32, (tm, tm), 1)
    before = jnp.where(c < r, 1.0, 0.0).astype(BF16)
    idx = idx_ref[...]
    base = carry[0:1, :]
    out = jnp.zeros((tm, LANES), F32)
    for kk in range(TOP_K):
        onehot = jnp.where(lane == idx[:, kk:kk + 1], 1.0, 0.0)
        prefix = jnp.dot(before, onehot.astype(BF16), preferred_element_type=F32) + base
        rk = jnp.sum(onehot * prefix, axis=1, keepdims=True)
        out = jnp.where(lane == kk, rk, out)
        base = base + jnp.sum(onehot, axis=0, keepdims=True)
    rank_ref[...] = out.astype(I32)
    carry[0:1, :] = base
    cnt_ref[...] = jnp.broadcast_to(base, cnt_ref.shape).astype(I32)


def _rank(top_idx):
    n = top_idx.shape[0]
    tm = min(ROW_TILE, n)
    return pl.pallas_call(
        _rank_body,
        grid=(n // tm,),
        in_specs=[pl.BlockSpec((tm, LANES), lambda i: (i, 0))],
        out_specs=[pl.BlockSpec((tm, LANES), lambda i: (i, 0)), pl.BlockSpec((8, LANES), lambda i: (0, 0))],
        out_shape=[jax.ShapeDtypeStruct((n, LANES), I32), jax.ShapeDtypeStruct((8, LANES), I32)],
        scratch_shapes=[pltpu.VMEM((8, LANES), F32)],
        compiler_params=_params("arbitrary"),
        name="rank",
    )(top_idx)


def _dispatch_body(dest_ref, valid_ref, h_ref, o_ref, zeros_ref, sem, zsem):
    i = pl.program_id(0)
    tm = h_ref.shape[0]
    blk = zeros_ref.shape[0]
    nb = o_ref.shape[0] // blk

    @pl.when(i == 0)
    def _():
        zeros_ref[...] = jnp.zeros(zeros_ref.shape, zeros_ref.dtype)

        def for_blocks(fn):
            for jb in range(nb):
                @pl.when(valid_ref[jb] < blk)
                def _():
                    fn(pltpu.make_async_copy(zeros_ref, o_ref.at[pl.ds(jb * blk, blk)], zsem))

        for_blocks(lambda cp: cp.start())
        for_blocks(lambda cp: cp.wait())

    def row_copy(r, kk):
        dst = dest_ref[(i * tm + r) * TOP_K + kk]
        return pltpu.make_async_copy(h_ref.at[pl.ds(r, 1)], o_ref.at[pl.ds(dst, 1)], sem)

    def issue(rb, carry):
        for jj in range(DMA_UNROLL):
            for kk in range(TOP_K):
                row_copy(rb * DMA_UNROLL + jj, kk).start()
        return carry

    lax.fori_loop(0, tm // DMA_UNROLL, issue, 0)
    for kk in range(TOP_K):
        pltpu.make_async_copy(h_ref, o_ref.at[pl.ds(0, tm)], sem).wait()


def _dispatch(dest_flat, block_valid, h2p, n_rows):
    n, dw = h2p.shape
    tm = min(ROW_TILE, n)
    return pl.pallas_call(
        _dispatch_body,
        grid_spec=pltpu.PrefetchScalarGridSpec(
            num_scalar_prefetch=2,
            grid=(n // tm,),
            in_specs=[pl.BlockSpec((tm, dw), lambda i, dst, bv: (i, 0))],
            out_specs=pl.BlockSpec(memory_space=pl.ANY),
            scratch_shapes=[pltpu.VMEM((MOE_BLOCK, dw), h2p.dtype), pltpu.SemaphoreType.DMA,
                            pltpu.SemaphoreType.DMA]),
        out_shape=jax.ShapeDtypeStruct((n_rows, dw), h2p.dtype),
        compiler_params=_params("arbitrary"),
        name="dispatch",
    )(dest_flat, block_valid, h2p)


def _experts_body(be_ref, ns_ref, rev_ref, nu_ref, x_ref, wgu_ref, wd_ref, bgu_ref, bd_ref, y_ref,
                  xlo_ref, xhi_ref):
    j = pl.program_id(0)
    f = pl.program_id(1)
    half = xlo_ref.shape[1]
    nsub = ns_ref[j]

    @pl.when(f == 0)
    def _():
        y_ref[...] = jnp.broadcast_to(bd_ref[...], y_ref.shape)

    @pl.when((f == 0) & (nsub > 0))
    def _():
        w = x_ref[...]
        xlo_ref[...] = lax.bitcast_convert_type(lax.shift_left(w, jnp.uint32(16)), F32).astype(BF16)
        xhi_ref[...] = lax.bitcast_convert_type(w & jnp.uint32(0xFFFF0000), F32).astype(BF16)

    def compute(m):
        gu = (jnp.dot(xlo_ref[0:m, :], wgu_ref[0:half, :].astype(BF16), preferred_element_type=F32)
              + jnp.dot(xhi_ref[0:m, :], wgu_ref[half:, :].astype(BF16), preferred_element_type=F32)) + bgu_ref[...]
        gate = jnp.minimum(gu, SWIGLU_LIMIT)
        glu = gate * jax.nn.sigmoid(SWIGLU_ALPHA * gate)
        up1 = jnp.clip(gu, -SWIGLU_LIMIT, SWIGLU_LIMIT) + 1.0
        n2 = gu.shape[1]
        rows2 = lax.broadcasted_iota(I32, (2 * LANES, LANES), 0)
        cols2 = lax.broadcasted_iota(I32, (2 * LANES, LANES), 1)
        sel = jnp.where(rows2 == 2 * cols2, 1.0, 0.0).astype(BF16)
        parts = []
        for c in range(n2 // LANES):
            sl = slice(c * LANES, (c + 1) * LANES)
            parts.append((glu[:, sl] * pltpu.roll(up1[:, sl], LANES - 1, 1)).astype(BF16))
        acts = [jnp.dot(jnp.concatenate(parts[2 * c:2 * c + 2], axis=1), sel, preferred_element_type=F32)
                for c in range(n2 // (2 * LANES))]
        act = jnp.concatenate(acts, axis=1).astype(BF16)
        y_ref[0:m, :] += jnp.dot(act, wd_ref[...].astype(BF16), preferred_element_type=F32)

    for ns in range(1, MOE_BLOCK // MOE_SUB + 1):
        @pl.when(nsub == ns)
        def _():
            compute(ns * MOE_SUB)


def _experts(block_e, block_nsub, block_rev, n_used, x_rows, w_gate_up, w_down, b_gate_up, b_down):
    p, dw = x_rows.shape
    d = 2 * dw
    dff = w_down.shape[1]
    nb = p // MOE_BLOCK
    nf = dff // FF_TILE

    def blk(j, f, be, ns, rev, nu):
        return jnp.minimum(j, nu[0] - 1)

    def fidx(j, f, be, ns, rev, nu):
        back = rev[jnp.minimum(j, nu[0] - 1)] == 1
        return jnp.where(j < nu[0], jnp.where(back, nf - 1 - f, f), jnp.where(back, 0, nf - 1))

    return pl.pallas_call(
        _experts_body,
        grid_spec=pltpu.PrefetchScalarGridSpec(
            num_scalar_prefetch=4,
            grid=(nb, nf),
            in_specs=[pl.BlockSpec((MOE_BLOCK, dw), lambda j, f, *s: (blk(j, f, *s), 0)),
                      pl.BlockSpec((None, d, 2 * FF_TILE), lambda j, f, *s: (s[0][j], 0, fidx(j, f, *s))),
                      pl.BlockSpec((None, FF_TILE, d), lambda j, f, *s: (s[0][j], fidx(j, f, *s), 0)),
                      pl.BlockSpec((None, 1, 2 * FF_TILE), lambda j, f, *s: (s[0][j], 0, fidx(j, f, *s))),
                      pl.BlockSpec((None, 1, d), lambda j, f, *s: (s[0][j], 0, 0))],
            out_specs=pl.BlockSpec((MOE_BLOCK, d), lambda j, f, *s: (j, 0)),
            scratch_shapes=[pltpu.VMEM((MOE_BLOCK, dw), BF16), pltpu.VMEM((MOE_BLOCK, dw), BF16)]),
        out_shape=jax.ShapeDtypeStruct((p, d), F32),
        compiler_params=_params("arbitrary", "arbitrary"),
        name="experts",
    )(block_e, block_nsub, block_rev, n_used, x_rows, w_gate_up, w_down, b_gate_up, b_down)


def _combine_body(tiles_per_batch, dest_ref, y_ref, gate_ref, x1_ref, ada_ref, g_ref, o_ref, buf, sem):
    i = pl.program_id(0)
    n_tiles = pl.num_programs(0)
    tm = x1_ref.shape[0]

    def row_copy(tile, slot, r, kk):
        src = dest_ref[(tile * tm + r) * TOP_K + kk]
        return pltpu.make_async_copy(y_ref.at[pl.ds(src, 1)], buf.at[slot, kk, pl.ds(r, 1)], sem.at[slot])

    def issue(tile, slot):
        def body(rb, carry):
            for jj in range(DMA_UNROLL):
                for kk in range(TOP_K):
                    row_copy(tile, slot, rb * DMA_UNROLL + jj, kk).start()
            return carry
        lax.fori_loop(0, tm // DMA_UNROLL, body, 0)

    def drain(tile, slot):
        for kk in range(TOP_K):
            pltpu.make_async_copy(y_ref.at[pl.ds(0, tm)], buf.at[slot, kk], sem.at[slot]).wait()

    slot = i % 2

    @pl.when(i == 0)
    def _():
        issue(0, 0)

    @pl.when(i + 1 < n_tiles)
    def _():
        issue(i + 1, 1 - slot)

    drain(i, slot)
    gates = gate_ref[...]
    y = jnp.zeros(x1_ref.shape, F32)
    for kk in range(TOP_K):
        y = y + gates[:, kk:kk + 1] * buf[slot, kk]
    o_ref[...] = x1_ref[...] + ada_ref[5:6, :] * _rms(y, g_ref[...])


def _combine(dest_flat, y_rows, gates, x1, ada3, g_post_ffn, seq):
    n, d = x1.shape
    tm = min(GATHER_TILE, seq)
    tiles_per_batch = seq // tm
    return pl.pallas_call(
        functools.partial(_combine_body, tiles_per_batch),
        grid_spec=pltpu.PrefetchScalarGridSpec(
            num_scalar_prefetch=1,
            grid=(n // tm,),
            in_specs=[pl.BlockSpec(memory_space=pl.ANY),
                      pl.BlockSpec((tm, LANES), lambda i, dst: (i, 0)),
                      pl.BlockSpec((tm, d), lambda i, dst: (i, 0)),
                      pl.BlockSpec((None, 6, d), lambda i, dst: (i // tiles_per_batch, 0, 0)),
                      pl.BlockSpec((1, d), lambda i, dst: (0, 0))],
            out_specs=pl.BlockSpec((tm, d), lambda i, dst: (i, 0)),
            scratch_shapes=[pltpu.VMEM((2, TOP_K, tm, d), F32), pltpu.SemaphoreType.DMA((2,))]),
        out_shape=jax.ShapeDtypeStruct((n, d), F32),
        compiler_params=_params("arbitrary"),
        name="combine",
    )(dest_flat, y_rows, gates, x1, ada3, g_post_ffn)


def _tail_w_in(w_in):
    d = w_in.shape[0]
    o_wi = C_KI + IDX_DIM
    ki = w_in[:, C_KI:o_wi]
    wi = w_in[:, o_wi:o_wi + IDX_HEADS]
    pad = jnp.zeros((d, LANES - IDX_HEADS), w_in.dtype)
    return jnp.concatenate([ki, ki, wi, pad], axis=1)


def _layer(x, c, positions, w_ada, b_ada, g_pre_mix, g_post_mix, w_in, conv_w, g_conv_out, g_attn_out, w_out,
           g_pre_ffn, g_post_ffn, w_router, b_router, w_gate_up, b_gate_up, w_down, b_down):
    bsz, seq, d = x.shape
    n = bsz * seq
    x2 = x.reshape(n, d)
    ada3 = _ada(c, w_ada, b_ada).reshape(bsz, 6, d)
    w_in_b = w_in.astype(BF16)

    yc, q, k, v, qi, ki, wi = _in_proj(
        x2, ada3, g_pre_mix.reshape(1, d), positions.reshape(n, 1).astype(I32), w_in_b, _tail_w_in(w_in_b),
        conv_w.reshape(CONV_WIDTH, D_CONV), g_conv_out.reshape(1, D_CONV), seq)
    b3 = lambda a: a.reshape(bsz, seq, a.shape[-1])
    ya = _dsa(b3(q), b3(qi), b3(wi), b3(k), b3(v), b3(ki), g_attn_out.reshape(1, D_ATT)).reshape(n, D_ATT)

    wr = jnp.zeros((d, LANES), F32).at[:, :N_EXPERTS].set(w_router)
    br = jnp.zeros((1, LANES), F32).at[0, :N_EXPERTS].set(b_router)
    x1, h2, top_idx, gates = _out_proj(yc, ya, x2, ada3, w_out.astype(BF16), g_post_mix.reshape(1, d),
                                       g_pre_ffn.reshape(1, d), wr, br, seq)

    rank, cnt = _rank(top_idx)
    counts = cnt[0, :N_EXPERTS]
    padded = (counts + MOE_BLOCK - 1) // MOE_BLOCK * MOE_BLOCK
    pad_ends = jnp.cumsum(padded)
    pad_starts = pad_ends - padded
    nb = -(-(n * TOP_K) // MOE_BLOCK) + N_EXPERTS
    dest = (pad_starts[top_idx[:, :TOP_K]] + rank[:, :TOP_K]).reshape(n * TOP_K).astype(I32)
    block_e = jnp.minimum(jnp.searchsorted(pad_ends, jnp.arange(nb, dtype=I32) * MOE_BLOCK, side='right'),
                          N_EXPERTS - 1).astype(I32)
    n_used = (pad_ends[-1:] // MOE_BLOCK).astype(I32)
    block_start = jnp.arange(nb, dtype=I32) * MOE_BLOCK
    block_valid = jnp.clip(counts[block_e] - (block_start - pad_starts[block_e]), 0, MOE_BLOCK)
    block_valid = jnp.where(block_start < pad_ends[-1], block_valid, 0).astype(I32)
    block_nsub = (block_valid + MOE_SUB - 1) // MOE_SUB
    block_e = jnp.where(block_start < pad_ends[-1], block_e, block_e[n_used[0] - 1])
    block_rev = ((block_start - pad_starts[block_e]) // MOE_BLOCK) % 2

    x_rows = _dispatch(dest, block_valid, h2, nb * MOE_BLOCK)
    y_rows = _experts(block_e, block_nsub, block_rev.astype(I32), n_used, x_rows, w_gate_up, w_down,
                      b_gate_up[:, None, :], b_down[:, None, :])
    out = _combine(dest, y_rows, gates, x1, ada3, g_post_ffn.reshape(1, d), seq)
    return out.reshape(bsz, seq, d)


def kernel(x, c, positions, w_ada, b_ada, g_pre_mix, g_post_mix, w_in, conv_w, g_conv_out, g_attn_out, w_out,
           g_pre_ffn, g_post_ffn, w_router, b_router, w_gate_up, b_gate_up, w_down, b_down):
    for l in range(w_ada.shape[0]):
        x = _layer(x, c, positions, w_ada[l], b_ada[l], g_pre_mix[l], g_post_mix[l], w_in[l], conv_w[l],
                   g_conv_out[l], g_attn_out[l], w_out[l], g_pre_ffn[l], g_post_ffn[l], w_router[l], b_router[l],
                   w_gate_up[l], b_gate_up[l], w_down[l], b_down[l])
    return x
```

```python
import functools

import numpy as np
import jax
import jax.numpy as jnp
from jax import lax
from jax.experimental import pallas as pl
from jax.experimental.pallas import tpu as pltpu

F32 = jnp.float32
BF16 = jnp.bfloat16
I32 = jnp.int32

EPS = 1e-6
LANES = 128
D_CONV = 1024
CONV_WIDTH = 3
N_HEADS = 8
N_KV_HEADS = 2
HEAD_DIM = 128
KV_REP = N_HEADS // N_KV_HEADS
ROPE_DIM = HEAD_DIM // 4
ROPE_THETA = 500000.0
IDX_HEADS = 16
IDX_DIM = 64
IDX_ROPE_DIM = IDX_DIM // 4
TOPK_MAX = 256
N_EXPERTS = 32
TOP_K = 4
SWIGLU_LIMIT = 7.0
SWIGLU_ALPHA = 1.702

D_ATT = N_HEADS * HEAD_DIM
D_KV = N_KV_HEADS * HEAD_DIM
D_QI = IDX_HEADS * IDX_DIM

INT_MIN = -2147483648
NEG_BIG = -1e30

ROW_TILE = 512
ATT_TILE = 512
MOE_BLOCK = 512
MOE_SUB = 256
MOE_ROWS = 512
FF_TILE = 512
GATHER_TILE = 128
DMA_UNROLL = 8
VMEM_LIMIT = 56 * 1024 * 1024


def _params(*sem):
    return pltpu.CompilerParams(dimension_semantics=sem, vmem_limit_bytes=VMEM_LIMIT)


def _rms(x, g):
    return x * lax.rsqrt(jnp.mean(x * x, axis=-1, keepdims=True) + EPS) * g


def _ada_body(c_ref, w_ref, b_ref, o_ref):
    c = c_ref[...]
    s = (c * jax.nn.sigmoid(c)).astype(BF16)
    o_ref[...] = jnp.dot(s, w_ref[...].astype(BF16), preferred_element_type=F32) + b_ref[...]


def _ada(c, w, b):
    bsz, d = c.shape
    n = w.shape[1]
    tn = 1536
    rows = 8
    cp = jnp.zeros((rows, d), F32).at[:bsz].set(c)
    out = pl.pallas_call(
        _ada_body,
        grid=(n // tn,),
        in_specs=[pl.BlockSpec((rows, d), lambda j: (0, 0)),
                  pl.BlockSpec((d, tn), lambda j: (0, j)),
                  pl.BlockSpec((1, tn), lambda j: (0, j))],
        out_specs=pl.BlockSpec((rows, tn), lambda j: (0, j)),
        out_shape=jax.ShapeDtypeStruct((rows, n), F32),
        compiler_params=_params("arbitrary"),
        name="ada",
    )(cp, w, b.reshape(1, n))
    return out[:bsz]


C_BCU = 0
C_Q = 3 * D_CONV
C_K = C_Q + D_ATT
C_V = C_K + D_KV
C_QI = C_V + D_KV
C_KI = C_QI + D_QI


def _rope_tables(pos, inv_freq, width, rot):
    half = rot // 2
    rows = pos.shape[0]
    lane = lax.broadcasted_iota(I32, (rows, LANES), 1) & (width - 1)
    ang = pos * inv_freq
    cos = jnp.cos(ang)
    sin = jnp.sin(ang)
    c = jnp.where(lane < rot, cos, 1.0)
    a = jnp.where((lane >= half) & (lane < rot), sin, 0.0)
    b = jnp.where(lane < half, -sin, 0.0)
    return c, a, b, half


def _rope(x, tabs):
    c, a, b, half = tabs
    return x * c + pltpu.roll(x, half, 1) * a + pltpu.roll(x, LANES - half, 1) * b


def _in_proj_body(tiles_per_batch, x_ref, ada_ref, g_ref, pos_ref, w_ref, wt_ref, cw_ref, gc_ref, fq_ref, fi_ref,
                  yc_ref, q_ref, k_ref, v_ref, qi_ref, ki_ref, wi_ref, vbuf):
    i = pl.program_id(0)
    tm = x_ref.shape[0]
    h = _rms(x_ref[...], g_ref[...]) * (1.0 + ada_ref[1:2, :]) + ada_ref[0:1, :]
    hb = h.astype(BF16)

    def proj(lo, hi):
        return jnp.dot(hb, w_ref[:, lo:hi], preferred_element_type=F32)

    @pl.when(i % tiles_per_batch == 0)
    def _():
        vbuf[0:8, :] = jnp.zeros((8, D_CONV), F32)

    bcu = proj(C_BCU, C_Q)
    v0 = bcu[:, D_CONV:2 * D_CONV] * bcu[:, 2 * D_CONV:]
    vbuf[8:8 + tm, :] = v0
    v1 = vbuf[7:7 + tm, :]
    v2 = vbuf[6:6 + tm, :]
    y = cw_ref[0:1, :] * v2 + cw_ref[1:2, :] * v1 + cw_ref[2:3, :] * v0
    yc_ref[...] = _rms(bcu[:, :D_CONV] * y, gc_ref[...]).astype(BF16)
    vbuf[0:8, :] = vbuf[tm:tm + 8, :]

    pos = pos_ref[...].astype(F32)
    tq = _rope_tables(pos, fq_ref[...], HEAD_DIM, ROPE_DIM)
    ti = _rope_tables(pos, fi_ref[...], IDX_DIM, IDX_ROPE_DIM)
    scale = HEAD_DIM ** -0.5

    qf = proj(C_Q, C_K)
    for hh in range(N_HEADS):
        sl = slice(hh * LANES, (hh + 1) * LANES)
        q_ref[:, sl] = (_rope(qf[:, sl], tq) * scale).astype(BF16)
    kf = proj(C_K, C_V)
    for hh in range(N_KV_HEADS):
        sl = slice(hh * LANES, (hh + 1) * LANES)
        k_ref[:, sl] = _rope(kf[:, sl], tq).astype(BF16)
    v_ref[...] = proj(C_V, C_QI).astype(BF16)
    qif = proj(C_QI, C_KI)
    for hh in range(D_QI // LANES):
        sl = slice(hh * LANES, (hh + 1) * LANES)
        qi_ref[:, sl] = _rope(qif[:, sl], ti).astype(BF16)
    tail = jnp.dot(hb, wt_ref[...], preferred_element_type=F32)
    ki_ref[...] = _rope(tail[:, :LANES], ti).astype(BF16)
    wi_ref[...] = tail[:, LANES:]


def _in_proj(x2, ada3, g_pre, pos2, w_main, w_tail, conv_w, g_conv, seq):
    n, d = x2.shape
    tm = min(ROW_TILE, seq)
    tiles_per_batch = seq // tm
    half_q = ROPE_DIM // 2
    half_i = IDX_ROPE_DIM // 2
    lane = np.arange(LANES)
    fq = jnp.asarray(ROPE_THETA, F32) ** (-jnp.asarray(lane % half_q, F32) / half_q)
    fi = jnp.asarray(ROPE_THETA, F32) ** (-jnp.asarray(lane % half_i, F32) / half_i)
    row = lambda i: (i, 0)
    fixed = lambda i: (0, 0)
    outs = [(D_CONV, BF16), (D_ATT, BF16), (D_KV, BF16), (D_KV, BF16), (D_QI, BF16), (LANES, BF16), (LANES, F32)]
    return pl.pallas_call(
        functools.partial(_in_proj_body, tiles_per_batch),
        grid=(n // tm,),
        in_specs=[pl.BlockSpec((tm, d), row),
                  pl.BlockSpec((None, 6, d), lambda i: (i // tiles_per_batch, 0, 0)),
                  pl.BlockSpec((1, d), fixed),
                  pl.BlockSpec((tm, 1), row),
                  pl.BlockSpec((d, C_KI), fixed, pipeline_mode=pl.Buffered(1)),
                  pl.BlockSpec((d, 2 * LANES), fixed),
                  pl.BlockSpec((CONV_WIDTH, D_CONV), fixed),
                  pl.BlockSpec((1, D_CONV), fixed),
                  pl.BlockSpec((1, LANES), fixed),
                  pl.BlockSpec((1, LANES), fixed)],
        out_specs=[pl.BlockSpec((tm, w), row) for w, _ in outs],
        out_shape=[jax.ShapeDtypeStruct((n, w), dt) for w, dt in outs],
        scratch_shapes=[pltpu.VMEM((tm + 8, D_CONV), F32)],
        compiler_params=_params("arbitrary"),
        name="in_proj",
    )(x2, ada3, g_pre, pos2, w_main, w_tail, conv_w, g_conv, fq.reshape(1, LANES), fi.reshape(1, LANES))


MAX_BISECT = 320

def _dsa_body(topk, nbits, q_ref, qi_ref, wi_ref, k_ref, v_ref, ki_ref, g_ref, o_ref,
              keys_ref, keys_t_ref, qm_ref, wb_ref, m_ref, l_ref, acc_ref):
    i = pl.program_id(1)
    tq = q_ref.shape[0]
    tk = tq
    nch = tk // LANES
    nkb = i + 1
    nt = (((1,), (1,)), ((), ()))

    lane = lax.broadcasted_iota(I32, (tq, LANES), 1)
    for p in range(IDX_HEADS // 2):
        qp = qi_ref[:, p * LANES:(p + 1) * LANES].astype(F32)
        qm_ref[(2 * p) * tq:(2 * p + 1) * tq, :] = jnp.where(lane < IDX_DIM, qp, 0.0).astype(BF16)
        qm_ref[(2 * p + 1) * tq:(2 * p + 2) * tq, :] = jnp.where(lane >= IDX_DIM, qp, 0.0).astype(BF16)
    w = wi_ref[...]
    for hh in range(IDX_HEADS):
        wb_ref[hh * tq:(hh + 1) * tq, :] = jnp.broadcast_to(w[:, hh:hh + 1], (tq, LANES))

    row = lax.broadcasted_iota(I32, (tq, tk), 0)
    col = lax.broadcasted_iota(I32, (tq, tk), 1)

    def score_block(kb, carry):
        k0 = pl.multiple_of(kb * tk, tk)
        kib = ki_ref[pl.ds(k0, tk), :]
        acc = jnp.zeros((tq, tk), F32)
        for j in range(IDX_HEADS // 4):
            d = lax.dot_general(qm_ref[4 * j * tq:4 * (j + 1) * tq, :], kib, nt, preferred_element_type=F32)
            wbj = wb_ref[4 * j * tq:4 * (j + 1) * tq, :]
            e = jnp.maximum(d, 0.0) * jnp.concatenate([wbj] * nch, axis=1)
            acc = acc + ((e[0:tq] + e[tq:2 * tq]) + (e[2 * tq:3 * tq] + e[3 * tq:4 * tq]))
        sc = jnp.where(k0 + col <= i * tq + row, acc, -jnp.inf)
        keys_ref[kb] = sc
        keys_t_ref[kb] = sc.T
        return carry

    lax.fori_loop(0, nkb, score_block, 0)

    def fold_t(elem_fn, merge, init):
        def body(kb, part):
            e = elem_fn(keys_t_ref[kb])
            for s0 in range(0, tk, 8):
                part = merge(part, e[s0:s0 + 8])
            return part
        return lax.fori_loop(0, nkb, body, jnp.full((8, tq), init, F32))

    def count_t(pred):
        return jnp.sum(fold_t(lambda s: jnp.where(pred(s), 1.0, 0.0), jnp.add, 0.0), axis=0, keepdims=True)

    kf = float(topk)
    lo0 = jnp.min(fold_t(lambda s: jnp.where(s > -jnp.inf, s, jnp.inf), jnp.minimum, jnp.inf),
                  axis=0, keepdims=True)
    hi0 = jnp.max(fold_t(lambda s: s, jnp.maximum, -jnp.inf), axis=0, keepdims=True)

    def bisect(carry):
        it, lo, hi, _ = carry
        mid = 0.5 * lo + 0.5 * hi
        keep = count_t(lambda s: s >= mid) >= kf
        moving = jnp.max(jnp.where((mid > lo) & (mid < hi), 1.0, 0.0))
        return it + 1, jnp.where(keep, mid, lo), jnp.where(keep, hi, mid), moving

    def unresolved(carry):
        return (carry[0] < MAX_BISECT) & (carry[3] > 0.0)

    _, thr_t, _, _ = lax.while_loop(unresolved, bisect, (jnp.int32(0), lo0, hi0, jnp.float32(1.0)))
    surplus = count_t(lambda s: s >= thr_t) - kf
    thr = jnp.broadcast_to(thr_t, (tq, tq)).T[:, :LANES]

    def count(pred):
        def body(kb, part):
            kk = keys_ref[kb]
            for c in range(nch):
                part = part + jnp.where(pred(kk[:, c * LANES:(c + 1) * LANES], kb * tk + c * LANES), 1.0, 0.0)
            return part
        part = lax.fori_loop(0, nkb, body, jnp.zeros((tq, LANES), F32))
        return jnp.broadcast_to(jnp.sum(part, axis=1, keepdims=True), (tq, LANES))

    zero = jnp.zeros((tq, LANES), I32)

    @pl.when(jnp.max(surplus) > 0.0)
    def _():
        need = kf - count(lambda kk, _: kk > thr)

        def index_bit(bi, p):
            cand = p + lax.shift_left(jnp.int32(1), nbits - 1 - bi)
            f = count(lambda kk, base: (kk == thr) & (base + lane < cand))
            return jnp.where(f <= need, cand, p)

        pend = lax.fori_loop(0, nbits, index_bit, zero)

        def drop(kb, carry):
            kk = keys_ref[kb]
            parts = []
            for c in range(nch):
                kc = kk[:, c * LANES:(c + 1) * LANES]
                parts.append(jnp.where((kc == thr) & (kb * tk + c * LANES + lane >= pend), -jnp.inf, kc))
            keys_ref[kb] = jnp.concatenate(parts, axis=1)
            return carry

        lax.fori_loop(0, nkb, drop, 0)

    tsel = thr

    m_ref[...] = jnp.full(m_ref.shape, NEG_BIG, F32)
    l_ref[...] = jnp.zeros(l_ref.shape, F32)
    acc_ref[...] = jnp.zeros(acc_ref.shape, F32)

    def attend(kb, carry):
        k0 = pl.multiple_of(kb * tk, tk)
        kk = keys_ref[kb]
        sel = jnp.concatenate([jnp.where(kk[:, c * LANES:(c + 1) * LANES] >= tsel, 0.0, NEG_BIG)
                               for c in range(nch)], axis=1)
        for g in range(N_KV_HEADS):
            kg = k_ref[pl.ds(k0, tk), g * LANES:(g + 1) * LANES]
            vg = v_ref[pl.ds(k0, tk), g * LANES:(g + 1) * LANES]
            qg = jnp.concatenate([q_ref[:, (KV_REP * g + r) * LANES:(KV_REP * g + r + 1) * LANES]
                                  for r in range(KV_REP)], axis=0)
            s = lax.dot_general(qg, kg, nt, preferred_element_type=F32)
            s = jnp.concatenate([s[r * tq:(r + 1) * tq] + sel for r in range(KV_REP)], axis=0)
            m_old = m_ref[g]
            m_new = jnp.maximum(m_old, jnp.max(s, axis=1, keepdims=True))
            alpha = jnp.exp(m_old - m_new)
            p = jnp.exp(s - jnp.concatenate([m_new] * nch, axis=1))
            l_ref[g] = alpha * l_ref[g] + jnp.sum(p, axis=1, keepdims=True)
            acc_ref[g] = alpha * acc_ref[g] + jnp.dot(p.astype(BF16), vg, preferred_element_type=F32)
            m_ref[g] = m_new
        return carry

    lax.fori_loop(0, nkb, attend, 0)

    heads = []
    for g in range(N_KV_HEADS):
        og = acc_ref[g] / l_ref[g]
        heads += [og[r * tq:(r + 1) * tq] for r in range(KV_REP)]
    o_ref[...] = _rms(jnp.concatenate(heads, axis=1), g_ref[...]).astype(BF16)


def _dsa(q, qi, wi, k, v, ki, g_attn):
    bsz, seq, _ = q.shape
    tq = min(ATT_TILE, seq)
    nq = seq // tq
    topk = min(TOPK_MAX, seq // 4)
    nbits = int(seq).bit_length()
    qblk = lambda w: pl.BlockSpec((None, tq, w), lambda b, i: (b, i, 0))
    full = lambda w: pl.BlockSpec((None, seq, w), lambda b, i: (b, 0, 0), pipeline_mode=pl.Buffered(1))
    return pl.pallas_call(
        functools.partial(_dsa_body, topk, nbits),
        grid=(bsz, nq),
        in_specs=[qblk(D_ATT), qblk(D_QI), qblk(LANES), full(D_KV), full(D_KV), full(LANES),
                  pl.BlockSpec((1, D_ATT), lambda b, i: (0, 0))],
        out_specs=qblk(D_ATT),
        out_shape=jax.ShapeDtypeStruct((bsz, seq, D_ATT), BF16),
        scratch_shapes=[pltpu.VMEM((nq, tq, tq), F32),
                        pltpu.VMEM((nq, tq, tq), F32),
                        pltpu.VMEM((IDX_HEADS * tq, LANES), BF16),
                        pltpu.VMEM((IDX_HEADS * tq, LANES), F32),
                        pltpu.VMEM((N_KV_HEADS, KV_REP * tq, LANES), F32),
                        pltpu.VMEM((N_KV_HEADS, KV_REP * tq, LANES), F32),
                        pltpu.VMEM((N_KV_HEADS, KV_REP * tq, LANES), F32)],
        compiler_params=_params("arbitrary", "arbitrary"),
        name="dsa",
    )(q, qi, wi, k, v, ki, g_attn)


def _split_bf16(x):
    hi = x.astype(BF16)
    return hi, (x - hi.astype(F32)).astype(BF16)


def _out_proj_body(yc_ref, ya_ref, x_ref, ada_ref, wo_ref, gpm_ref, gpf_ref, wr_ref, br_ref,
                   x1_ref, h2_ref, idx_ref, gate_ref):
    tm = x_ref.shape[0]
    mix = (jnp.dot(yc_ref[...], wo_ref[0:D_CONV, :], preferred_element_type=F32)
           + jnp.dot(ya_ref[...], wo_ref[D_CONV:, :], preferred_element_type=F32))
    x1 = x_ref[...] + ada_ref[2:3, :] * _rms(mix, gpm_ref[...])
    x1_ref[...] = x1
    h2 = _rms(x1, gpf_ref[...]) * (1.0 + ada_ref[4:5, :]) + ada_ref[3:4, :]
    h2_ref[...] = h2

    h_hi, h_lo = _split_bf16(h2)
    w_hi, w_lo = _split_bf16(wr_ref[...])
    logits = (jnp.dot(h_hi, w_hi, preferred_element_type=F32) + jnp.dot(h_hi, w_lo, preferred_element_type=F32)
              + jnp.dot(h_lo, w_hi, preferred_element_type=F32)) + br_ref[...]
    lane = lax.broadcasted_iota(I32, (tm, LANES), 1).astype(F32)
    cur = jnp.where(lane < N_EXPERTS, logits, -jnp.inf)
    vals, idxs = [], []
    for _ in range(TOP_K):
        m = jnp.max(cur, axis=1, keepdims=True)
        am = jnp.min(jnp.where(cur == m, lane, float(LANES)), axis=1, keepdims=True)
        vals.append(m)
        idxs.append(am)
        cur = jnp.where(lane == am, -jnp.inf, cur)
    es = [jnp.exp(vv - vals[0]) for vv in vals]
    tot = es[0] + es[1] + es[2] + es[3]
    idx_out = jnp.zeros((tm, LANES), I32)
    gate_out = jnp.zeros((tm, LANES), F32)
    for kk in range(TOP_K):
        idx_out = jnp.where(lane == kk, idxs[kk].astype(I32), idx_out)
        gate_out = jnp.where(lane == kk, es[kk] / tot, gate_out)
    idx_ref[...] = idx_out
    gate_ref[...] = gate_out


def _out_proj(yc, ya, x2, ada3, w_out, g_post_mix, g_pre_ffn, w_router, b_router, seq):
    n, d = x2.shape
    tm = min(ROW_TILE, seq)
    tiles_per_batch = seq // tm
    row = lambda i: (i, 0)
    fixed = lambda i: (0, 0)
    return pl.pallas_call(
        _out_proj_body,
        grid=(n // tm,),
        in_specs=[pl.BlockSpec((tm, D_CONV), row),
                  pl.BlockSpec((tm, D_ATT), row),
                  pl.BlockSpec((tm, d), row),
                  pl.BlockSpec((None, 6, d), lambda i: (i // tiles_per_batch, 0, 0)),
                  pl.BlockSpec((D_CONV + D_ATT, d), fixed, pipeline_mode=pl.Buffered(1)),
                  pl.BlockSpec((1, d), fixed),
                  pl.BlockSpec((1, d), fixed),
                  pl.BlockSpec((d, LANES), fixed),
                  pl.BlockSpec((1, LANES), fixed)],
        out_specs=[pl.BlockSpec((tm, d), row), pl.BlockSpec((tm, d), row),
                   pl.BlockSpec((tm, LANES), row), pl.BlockSpec((tm, LANES), row)],
        out_shape=[jax.ShapeDtypeStruct((n, d), F32), jax.ShapeDtypeStruct((n, d), F32),
                   jax.ShapeDtypeStruct((n, LANES), I32), jax.ShapeDtypeStruct((n, LANES), F32)],
        compiler_params=_params("arbitrary"),
        name="out_proj",
    )(yc, ya, x2, ada3, w_out, g_post_mix, g_pre_ffn, w_router, b_router)


def _rank_body(idx_ref, rank_ref, cnt_ref, carry):
    i = pl.program_id(0)
    tm = idx_ref.shape[0]

    @pl.when(i == 0)
    def _():
        carry[...] = jnp.zeros(carry.shape, F32)

    lane = lax.broadcasted_iota(I32, (tm, LANES), 1)
    r = lax.broadcasted_iota(I32, (tm, tm), 0)
    c = lax.broadcasted_iota(I32, (tm, tm), 1)
    before = jnp.where(c < r, 1.0, 0.0).astype(BF16)
    idx = idx_ref[...]
    base = carry[0:1, :]
    out = jnp.zeros((tm, LANES), F32)
    for kk in range(TOP_K):
        onehot = jnp.where(lane == idx[:, kk:kk + 1], 1.0, 0.0)
        prefix = jnp.dot(before, onehot.astype(BF16), preferred_element_type=F32) + base
        rk = jnp.sum(onehot * prefix, axis=1, keepdims=True)
        out = jnp.where(lane == kk, rk, out)
        base = base + jnp.sum(onehot, axis=0, keepdims=True)
    rank_ref[...] = out.astype(I32)
    carry[0:1, :] = base
    cnt_ref[...] = jnp.broadcast_to(base, cnt_ref.shape).astype(I32)


def _rank(top_idx):
    n = top_idx.shape[0]
    tm = min(ROW_TILE, n)
    return pl.pallas_call(
        _rank_body,
        grid=(n // tm,),
        in_specs=[pl.BlockSpec((tm, LANES), lambda i: (i, 0))],
        out_specs=[pl.BlockSpec((tm, LANES), lambda i: (i, 0)), pl.BlockSpec((8, LANES), lambda i: (0, 0))],
        out_shape=[jax.ShapeDtypeStruct((n, LANES), I32), jax.ShapeDtypeStruct((8, LANES), I32)],
        scratch_shapes=[pltpu.VMEM((8, LANES), F32)],
        compiler_params=_params("arbitrary"),
        name="rank",
    )(top_idx)


def _dest_body(idx_ref, rank_ref, start_ref, o_ref):
    tm = idx_ref.shape[0]
    lane = lax.broadcasted_iota(I32, (tm, LANES), 1)
    idx = idx_ref[...]
    rank = rank_ref[...]
    out = jnp.zeros((tm, LANES), I32)
    for kk in range(TOP_K):
        start = jnp.sum(jnp.where(lane == idx[:, kk:kk + 1], start_ref[...], 0.0), axis=1, keepdims=True)
        out = jnp.where(lane == kk, start.astype(I32) + rank[:, kk:kk + 1], out)
    o_ref[...] = out


def _dest(top_idx, rank, starts):
    n = top_idx.shape[0]
    tm = min(2 * ROW_TILE, n)
    row = lambda i: (i, 0)
    return pl.pallas_call(
        _dest_body,
        grid=(n // tm,),
        in_specs=[pl.BlockSpec((tm, LANES), row), pl.BlockSpec((tm, LANES), row),
                  pl.BlockSpec((1, LANES), lambda i: (0, 0))],
        out_specs=pl.BlockSpec((tm, LANES), row),
        out_shape=jax.ShapeDtypeStruct((n, LANES), I32),
        compiler_params=_params("arbitrary"),
        name="dest",
    )(top_idx, rank, starts)


def _dispatch_body(dest_ref, valid_ref, h_ref, o_ref, zeros_ref, sem, zsem):
    i = pl.program_id(0)
    tm = h_ref.shape[0]
    blk = zeros_ref.shape[0]
    nb = o_ref.shape[0] // blk

    @pl.when(i == 0)
    def _():
        zeros_ref[...] = jnp.zeros(zeros_ref.shape, zeros_ref.dtype)

        def for_blocks(fn):
            for jb in range(nb):
                @pl.when(valid_ref[jb] < blk)
                def _():
                    fn(pltpu.make_async_copy(zeros_ref, o_ref.at[pl.ds(jb * blk, blk)], zsem))

        for_blocks(lambda cp: cp.start())
        for_blocks(lambda cp: cp.wait())

    def row_copy(r, kk):
        dst = dest_ref[(i * tm + r) * TOP_K + kk]
        return pltpu.make_async_copy(h_ref.at[pl.ds(r, 1)], o_ref.at[pl.ds(dst, 1)], sem)

    def issue(rb, carry):
        for jj in range(DMA_UNROLL):
            for kk in range(TOP_K):
                row_copy(rb * DMA_UNROLL + jj, kk).start()
        return carry

    lax.fori_loop(0, tm // DMA_UNROLL, issue, 0)
    for kk in range(TOP_K):
        pltpu.make_async_copy(h_ref, o_ref.at[pl.ds(0, tm)], sem).wait()


def _dispatch(dest_flat, block_valid, h2p, n_rows):
    n, dw = h2p.shape
    tm = min(ROW_TILE, n)
    return pl.pallas_call(
        _dispatch_body,
        grid_spec=pltpu.PrefetchScalarGridSpec(
            num_scalar_prefetch=2,
            grid=(n // tm,),
            in_specs=[pl.BlockSpec((tm, dw), lambda i, dst, bv: (i, 0))],
            out_specs=pl.BlockSpec(memory_space=pl.ANY),
            scratch_shapes=[pltpu.VMEM((MOE_BLOCK, dw), h2p.dtype), pltpu.SemaphoreType.DMA,
                            pltpu.SemaphoreType.DMA]),
        out_shape=jax.ShapeDtypeStruct((n_rows, dw), h2p.dtype),
        compiler_params=_params("arbitrary"),
        name="dispatch",
    )(dest_flat, block_valid, h2p)


def _experts_body(be_ref, ns_ref, rev_ref, nu_ref, x_ref, wgu_ref, wd_ref, bgu_ref, bd_ref, y_ref, xb_ref):
    j = pl.program_id(0)
    f = pl.program_id(1)
    nsub = ns_ref[j]

    @pl.when(f == 0)
    def _():
        y_ref[...] = jnp.broadcast_to(bd_ref[...], y_ref.shape)

    @pl.when((f == 0) & (nsub > 0))
    def _():
        xb_ref[...] = x_ref[...].astype(BF16)

    def compute(r0, r1):
        gu = jnp.dot(xb_ref[r0:r1, :], wgu_ref[...].astype(BF16), preferred_element_type=F32) + bgu_ref[...]
        gate = jnp.minimum(gu, SWIGLU_LIMIT)
        glu = gate * jax.nn.sigmoid(SWIGLU_ALPHA * gate)
        up1 = jnp.clip(gu, -SWIGLU_LIMIT, SWIGLU_LIMIT) + 1.0
        n2 = gu.shape[1]
        rows2 = lax.broadcasted_iota(I32, (2 * LANES, LANES), 0)
        cols2 = lax.broadcasted_iota(I32, (2 * LANES, LANES), 1)
        sel = jnp.where(rows2 == 2 * cols2, 1.0, 0.0).astype(BF16)
        parts = []
        for c in range(n2 // LANES):
            sl = slice(c * LANES, (c + 1) * LANES)
            parts.append((glu[:, sl] * pltpu.roll(up1[:, sl], LANES - 1, 1)).astype(BF16))
        acts = [jnp.dot(jnp.concatenate(parts[2 * c:2 * c + 2], axis=1), sel, preferred_element_type=F32)
                for c in range(n2 // (2 * LANES))]
        act = jnp.concatenate(acts, axis=1).astype(BF16)
        y_ref[r0:r1, :] += jnp.dot(act, wd_ref[...].astype(BF16), preferred_element_type=F32)

    for ns in range(1, MOE_BLOCK // MOE_SUB + 1):
        @pl.when(nsub == ns)
        def _():
            for r0 in range(0, ns * MOE_SUB, MOE_ROWS):
                compute(r0, min(r0 + MOE_ROWS, ns * MOE_SUB))


def _experts(block_e, block_nsub, block_rev, n_used, x_rows, w_gate_up, w_down, b_gate_up, b_down):
    p, d = x_rows.shape
    dw = d
    dff = w_down.shape[1]
    nb = p // MOE_BLOCK
    nf = dff // FF_TILE

    def blk(j, f, be, ns, rev, nu):
        return jnp.minimum(j, nu[0] - 1)

    def fidx(j, f, be, ns, rev, nu):
        back = rev[jnp.minimum(j, nu[0] - 1)] == 1
        return jnp.where(j < nu[0], jnp.where(back, nf - 1 - f, f), jnp.where(back, 0, nf - 1))

    return pl.pallas_call(
        _experts_body,
        grid_spec=pltpu.PrefetchScalarGridSpec(
            num_scalar_prefetch=4,
            grid=(nb, nf),
            in_specs=[pl.BlockSpec((MOE_BLOCK, dw), lambda j, f, *s: (blk(j, f, *s), 0)),
                      pl.BlockSpec((None, d, 2 * FF_TILE), lambda j, f, *s: (s[0][j], 0, fidx(j, f, *s))),
                      pl.BlockSpec((None, FF_TILE, d), lambda j, f, *s: (s[0][j], fidx(j, f, *s), 0)),
                      pl.BlockSpec((None, 1, 2 * FF_TILE), lambda j, f, *s: (s[0][j], 0, fidx(j, f, *s))),
                      pl.BlockSpec((None, 1, d), lambda j, f, *s: (s[0][j], 0, 0))],
            out_specs=pl.BlockSpec((MOE_BLOCK, d), lambda j, f, *s: (j, 0)),
            scratch_shapes=[pltpu.VMEM((MOE_BLOCK, d), BF16)]),
        out_shape=jax.ShapeDtypeStruct((p, d), F32),
        compiler_params=_params("arbitrary", "arbitrary"),
        name="experts",
    )(block_e, block_nsub, block_rev, n_used, x_rows, w_gate_up, w_down, b_gate_up, b_down)


def _combine_body(tiles_per_batch, dest_ref, y_ref, gate_ref, x1_ref, ada_ref, g_ref, o_ref, buf, sem):
    i = pl.program_id(0)
    n_tiles = pl.num_programs(0)
    tm = x1_ref.shape[0]

    def row_copy(tile, slot, r, kk):
        src = dest_ref[(tile * tm + r) * TOP_K + kk]
        return pltpu.make_async_copy(y_ref.at[pl.ds(src, 1)], buf.at[slot, kk, pl.ds(r, 1)], sem.at[slot])

    def issue(tile, slot):
        def body(rb, carry):
            for jj in range(DMA_UNROLL):
                for kk in range(TOP_K):
                    row_copy(tile, slot, rb * DMA_UNROLL + jj, kk).start()
            return carry
        lax.fori_loop(0, tm // DMA_UNROLL, body, 0)

    def drain(tile, slot):
        for kk in range(TOP_K):
            pltpu.make_async_copy(y_ref.at[pl.ds(0, tm)], buf.at[slot, kk], sem.at[slot]).wait()

    slot = i % 2

    @pl.when(i == 0)
    def _():
        issue(0, 0)

    @pl.when(i + 1 < n_tiles)
    def _():
        issue(i + 1, 1 - slot)

    drain(i, slot)
    gates = gate_ref[...]
    y = jnp.zeros(x1_ref.shape, F32)
    for kk in range(TOP_K):
        y = y + gates[:, kk:kk + 1] * buf[slot, kk]
    o_ref[...] = x1_ref[...] + ada_ref[5:6, :] * _rms(y, g_ref[...])


def _combine(dest_flat, y_rows, gates, x1, ada3, g_post_ffn, seq):
    n, d = x1.shape
    tm = min(GATHER_TILE, seq)
    tiles_per_batch = seq // tm
    return pl.pallas_call(
        functools.partial(_combine_body, tiles_per_batch),
        grid_spec=pltpu.PrefetchScalarGridSpec(
            num_scalar_prefetch=1,
            grid=(n // tm,),
            in_specs=[pl.BlockSpec(memory_space=pl.ANY),
                      pl.BlockSpec((tm, LANES), lambda i, dst: (i, 0)),
                      pl.BlockSpec((tm, d), lambda i, dst: (i, 0)),
                      pl.BlockSpec((None, 6, d), lambda i, dst: (i // tiles_per_batch, 0, 0)),
                      pl.BlockSpec((1, d), lambda i, dst: (0, 0))],
            out_specs=pl.BlockSpec((tm, d), lambda i, dst: (i, 0)),
            scratch_shapes=[pltpu.VMEM((2, TOP_K, tm, d), F32), pltpu.SemaphoreType.DMA((2,))]),
        out_shape=jax.ShapeDtypeStruct((n, d), F32),
        compiler_params=_params("arbitrary"),
        name="combine",
    )(dest_flat, y_rows, gates, x1, ada3, g_post_ffn)


def _tail_w_in(w_in):
    d = w_in.shape[0]
    o_wi = C_KI + IDX_DIM
    ki = w_in[:, C_KI:o_wi]
    wi = w_in[:, o_wi:o_wi + IDX_HEADS]
    pad = jnp.zeros((d, LANES - IDX_HEADS), w_in.dtype)
    return jnp.concatenate([ki, ki, wi, pad], axis=1)


def _layer(x, c, positions, w_ada, b_ada, g_pre_mix, g_post_mix, w_in, conv_w, g_conv_out, g_attn_out, w_out,
           g_pre_ffn, g_post_ffn, w_router, b_router, w_gate_up, b_gate_up, w_down, b_down):
    bsz, seq, d = x.shape
    n = bsz * seq
    x2 = x.reshape(n, d)
    ada3 = _ada(c, w_ada, b_ada).reshape(bsz, 6, d)
    w_in_b = w_in.astype(BF16)

    yc, q, k, v, qi, ki, wi = _in_proj(
        x2, ada3, g_pre_mix.reshape(1, d), positions.reshape(n, 1).astype(I32), w_in_b, _tail_w_in(w_in_b),
        conv_w.reshape(CONV_WIDTH, D_CONV), g_conv_out.reshape(1, D_CONV), seq)
    b3 = lambda a: a.reshape(bsz, seq, a.shape[-1])
    ya = _dsa(b3(q), b3(qi), b3(wi), b3(k), b3(v), b3(ki), g_attn_out.reshape(1, D_ATT)).reshape(n, D_ATT)

    wr = jnp.zeros((d, LANES), F32).at[:, :N_EXPERTS].set(w_router)
    br = jnp.zeros((1, LANES), F32).at[0, :N_EXPERTS].set(b_router)
    x1, h2, top_idx, gates = _out_proj(yc, ya, x2, ada3, w_out.astype(BF16), g_post_mix.reshape(1, d),
                                       g_pre_ffn.reshape(1, d), wr, br, seq)

    rank, cnt = _rank(top_idx)
    counts = cnt[0, :N_EXPERTS]
    padded = (counts + MOE_BLOCK - 1) // MOE_BLOCK * MOE_BLOCK
    pad_ends = jnp.cumsum(padded)
    pad_starts = pad_ends - padded
    nb = -(-(n * TOP_K) // MOE_BLOCK) + N_EXPERTS
    starts_row = jnp.zeros((1, LANES), F32).at[0, :N_EXPERTS].set(pad_starts.astype(F32))
    dest = _dest(top_idx, rank, starts_row)[:, :TOP_K].reshape(n * TOP_K)
    block_start = jnp.arange(nb, dtype=I32) * MOE_BLOCK
    block_e = jnp.minimum(jnp.sum((pad_ends[None, :] <= block_start[:, None]).astype(I32), axis=1),
                          N_EXPERTS - 1)
    n_used = (pad_ends[-1:] // MOE_BLOCK).astype(I32)
    block_valid = jnp.clip(counts[block_e] - (block_start - pad_starts[block_e]), 0, MOE_BLOCK)
    block_valid = jnp.where(block_start < pad_ends[-1], block_valid, 0).astype(I32)
    block_nsub = (block_valid + MOE_SUB - 1) // MOE_SUB
    block_e = jnp.where(block_start < pad_ends[-1], block_e, block_e[n_used[0] - 1])
    block_rev = ((block_start - pad_starts[block_e]) // MOE_BLOCK) % 2

    x_rows = _dispatch(dest, block_valid, h2, nb * MOE_BLOCK)
    y_rows = _experts(block_e, block_nsub, block_rev.astype(I32), n_used, x_rows, w_gate_up, w_down,
                      b_gate_up[:, None, :], b_down[:, None, :])
    out = _combine(dest, y_rows, gates, x1, ada3, g_post_ffn.reshape(1, d), seq)
    return out.reshape(bsz, seq, d)


def kernel(x, c, positions, w_ada, b_ada, g_pre_mix, g_post_mix, w_in, conv_w, g_conv_out, g_attn_out, w_out,
           g_pre_ffn, g_post_ffn, w_router, b_router, w_gate_up, b_gate_up, w_down, b_down):
    for l in range(w_ada.shape[0]):
        x = _layer(x, c, positions, w_ada[l], b_ada[l], g_pre_mix[l], g_post_mix[l], w_in[l], conv_w[l],
                   g_conv_out[l], g_attn_out[l], w_out[l], g_pre_ffn[l], g_post_ffn[l], w_router[l], b_router[l],
                   w_gate_up[l], b_gate_up[l], w_down[l], b_down[l])
    return x
```

```python
import functools

import numpy as np
import jax
import jax.numpy as jnp
from jax import lax
from jax.experimental import pallas as pl
from jax.experimental.pallas import tpu as pltpu

F32 = jnp.float32
BF16 = jnp.bfloat16
I32 = jnp.int32

EPS = 1e-6
LANES = 128
D_CONV = 1024
CONV_WIDTH = 3
N_HEADS = 8
N_KV_HEADS = 2
HEAD_DIM = 128
KV_REP = N_HEADS // N_KV_HEADS
ROPE_DIM = HEAD_DIM // 4
ROPE_THETA = 500000.0
IDX_HEADS = 16
IDX_DIM = 64
IDX_ROPE_DIM = IDX_DIM // 4
TOPK_MAX = 256
N_EXPERTS = 32
TOP_K = 4
SWIGLU_LIMIT = 7.0
SWIGLU_ALPHA = 1.702

D_ATT = N_HEADS * HEAD_DIM
D_KV = N_KV_HEADS * HEAD_DIM
D_QI = IDX_HEADS * IDX_DIM

INT_MIN = -2147483648
NEG_BIG = -1e30

ROW_TILE = 512
ATT_TILE = 512
MOE_BLOCK = 512
MOE_SUB = 128
MOE_ROWS = 512
FF_TILE = 512
GATHER_TILE = 128
DMA_UNROLL = 8
VMEM_LIMIT = 56 * 1024 * 1024


def _params(*sem):
    return pltpu.CompilerParams(dimension_semantics=sem, vmem_limit_bytes=VMEM_LIMIT)


def _rms(x, g):
    return x * lax.rsqrt(jnp.mean(x * x, axis=-1, keepdims=True) + EPS) * g


def _ada_body(c_ref, w_ref, b_ref, o_ref):
    c = c_ref[...]
    s = (c * jax.nn.sigmoid(c)).astype(BF16)
    o_ref[...] = jnp.dot(s, w_ref[...].astype(BF16), preferred_element_type=F32) + b_ref[...]


def _ada(c, w, b):
    bsz, d = c.shape
    n = w.shape[1]
    tn = 1536
    rows = 8
    cp = jnp.zeros((rows, d), F32).at[:bsz].set(c)
    out = pl.pallas_call(
        _ada_body,
        grid=(n // tn,),
        in_specs=[pl.BlockSpec((rows, d), lambda j: (0, 0)),
                  pl.BlockSpec((d, tn), lambda j: (0, j)),
                  pl.BlockSpec((1, tn), lambda j: (0, j))],
        out_specs=pl.BlockSpec((rows, tn), lambda j: (0, j)),
        out_shape=jax.ShapeDtypeStruct((rows, n), F32),
        compiler_params=_params("arbitrary"),
        name="ada",
    )(cp, w, b.reshape(1, n))
    return out[:bsz]


C_BCU = 0
C_Q = 3 * D_CONV
C_K = C_Q + D_ATT
C_V = C_K + D_KV
C_QI = C_V + D_KV
C_KI = C_QI + D_QI


def _rope_tables(pos, inv_freq, width, rot):
    half = rot // 2
    rows = pos.shape[0]
    lane = lax.broadcasted_iota(I32, (rows, LANES), 1) & (width - 1)
    ang = pos * inv_freq
    cos = jnp.cos(ang)
    sin = jnp.sin(ang)
    c = jnp.where(lane < rot, cos, 1.0)
    a = jnp.where((lane >= half) & (lane < rot), sin, 0.0)
    b = jnp.where(lane < half, -sin, 0.0)
    return c, a, b, half


def _rope(x, tabs):
    c, a, b, half = tabs
    return x * c + pltpu.roll(x, half, 1) * a + pltpu.roll(x, LANES - half, 1) * b


def _in_proj_body(tiles_per_batch, x_ref, ada_ref, g_ref, pos_ref, w_ref, wt_ref, cw_ref, gc_ref, fq_ref, fi_ref,
                  yc_ref, q_ref, k_ref, v_ref, qi_ref, ki_ref, wi_ref, vbuf):
    i = pl.program_id(0)
    tm = x_ref.shape[0]
    h = _rms(x_ref[...], g_ref[...]) * (1.0 + ada_ref[1:2, :]) + ada_ref[0:1, :]
    hb = h.astype(BF16)

    def proj(lo, hi):
        return jnp.dot(hb, w_ref[:, lo:hi], preferred_element_type=F32)

    @pl.when(i % tiles_per_batch == 0)
    def _():
        vbuf[0:8, :] = jnp.zeros((8, D_CONV), F32)

    bcu = proj(C_BCU, C_Q)
    v0 = bcu[:, D_CONV:2 * D_CONV] * bcu[:, 2 * D_CONV:]
    vbuf[8:8 + tm, :] = v0
    v1 = vbuf[7:7 + tm, :]
    v2 = vbuf[6:6 + tm, :]
    y = cw_ref[0:1, :] * v2 + cw_ref[1:2, :] * v1 + cw_ref[2:3, :] * v0
    yc_ref[...] = _rms(bcu[:, :D_CONV] * y, gc_ref[...]).astype(BF16)
    vbuf[0:8, :] = vbuf[tm:tm + 8, :]

    pos = pos_ref[...].astype(F32)
    tq = _rope_tables(pos, fq_ref[...], HEAD_DIM, ROPE_DIM)
    ti = _rope_tables(pos, fi_ref[...], IDX_DIM, IDX_ROPE_DIM)
    scale = HEAD_DIM ** -0.5

    qf = proj(C_Q, C_K)
    for hh in range(N_HEADS):
        sl = slice(hh * LANES, (hh + 1) * LANES)
        q_ref[:, sl] = (_rope(qf[:, sl], tq) * scale).astype(BF16)
    kf = proj(C_K, C_V)
    for hh in range(N_KV_HEADS):
        sl = slice(hh * LANES, (hh + 1) * LANES)
        k_ref[:, sl] = _rope(kf[:, sl], tq).astype(BF16)
    v_ref[...] = proj(C_V, C_QI).astype(BF16)
    qif = proj(C_QI, C_KI)
    for hh in range(D_QI // LANES):
        sl = slice(hh * LANES, (hh + 1) * LANES)
        qi_ref[:, sl] = _rope(qif[:, sl], ti).astype(BF16)
    tail = jnp.dot(hb, wt_ref[...], preferred_element_type=F32)
    ki_ref[...] = _rope(tail[:, :LANES], ti).astype(BF16)
    wi_ref[...] = tail[:, LANES:]


def _in_proj(x2, ada3, g_pre, pos2, w_main, w_tail, conv_w, g_conv, seq):
    n, d = x2.shape
    tm = min(ROW_TILE, seq)
    tiles_per_batch = seq // tm
    half_q = ROPE_DIM // 2
    half_i = IDX_ROPE_DIM // 2
    lane = np.arange(LANES)
    fq = jnp.asarray(ROPE_THETA, F32) ** (-jnp.asarray(lane % half_q, F32) / half_q)
    fi = jnp.asarray(ROPE_THETA, F32) ** (-jnp.asarray(lane % half_i, F32) / half_i)
    row = lambda i: (i, 0)
    fixed = lambda i: (0, 0)
    outs = [(D_CONV, BF16), (D_ATT, BF16), (D_KV, BF16), (D_KV, BF16), (D_QI, BF16), (LANES, BF16), (LANES, F32)]
    return pl.pallas_call(
        functools.partial(_in_proj_body, tiles_per_batch),
        grid=(n // tm,),
        in_specs=[pl.BlockSpec((tm, d), row),
                  pl.BlockSpec((None, 6, d), lambda i: (i // tiles_per_batch, 0, 0)),
                  pl.BlockSpec((1, d), fixed),
                  pl.BlockSpec((tm, 1), row),
                  pl.BlockSpec((d, C_KI), fixed, pipeline_mode=pl.Buffered(1)),
                  pl.BlockSpec((d, 2 * LANES), fixed),
                  pl.BlockSpec((CONV_WIDTH, D_CONV), fixed),
                  pl.BlockSpec((1, D_CONV), fixed),
                  pl.BlockSpec((1, LANES), fixed),
                  pl.BlockSpec((1, LANES), fixed)],
        out_specs=[pl.BlockSpec((tm, w), row) for w, _ in outs],
        out_shape=[jax.ShapeDtypeStruct((n, w), dt) for w, dt in outs],
        scratch_shapes=[pltpu.VMEM((tm + 8, D_CONV), F32)],
        compiler_params=_params("arbitrary"),
        name="in_proj",
    )(x2, ada3, g_pre, pos2, w_main, w_tail, conv_w, g_conv, fq.reshape(1, LANES), fi.reshape(1, LANES))


MAX_BISECT = 320

def _dsa_body(topk, nbits, q_ref, qi_ref, wi_ref, k_ref, v_ref, ki_ref, g_ref, o_ref,
              keys_ref, keys_t_ref, qm_ref, wb_ref, m_ref, l_ref, acc_ref):
    i = pl.program_id(1)
    tq = q_ref.shape[0]
    tk = tq
    nch = tk // LANES
    nkb = i + 1
    nt = (((1,), (1,)), ((), ()))

    lane = lax.broadcasted_iota(I32, (tq, LANES), 1)
    for p in range(IDX_HEADS // 2):
        qp = qi_ref[:, p * LANES:(p + 1) * LANES].astype(F32)
        qm_ref[(2 * p) * tq:(2 * p + 1) * tq, :] = jnp.where(lane < IDX_DIM, qp, 0.0).astype(BF16)
        qm_ref[(2 * p + 1) * tq:(2 * p + 2) * tq, :] = jnp.where(lane >= IDX_DIM, qp, 0.0).astype(BF16)
    w = wi_ref[...]
    for hh in range(IDX_HEADS):
        wb_ref[hh * tq:(hh + 1) * tq, :] = jnp.broadcast_to(w[:, hh:hh + 1], (tq, LANES))

    row = lax.broadcasted_iota(I32, (tq, tk), 0)
    col = lax.broadcasted_iota(I32, (tq, tk), 1)

    def score_block(kb, carry):
        k0 = pl.multiple_of(kb * tk, tk)
        kib = ki_ref[pl.ds(k0, tk), :]
        acc = jnp.zeros((tq, tk), F32)
        for j in range(IDX_HEADS // 4):
            d = lax.dot_general(qm_ref[4 * j * tq:4 * (j + 1) * tq, :], kib, nt, preferred_element_type=F32)
            wbj = wb_ref[4 * j * tq:4 * (j + 1) * tq, :]
            e = jnp.maximum(d, 0.0) * jnp.concatenate([wbj] * nch, axis=1)
            acc = acc + ((e[0:tq] + e[tq:2 * tq]) + (e[2 * tq:3 * tq] + e[3 * tq:4 * tq]))
        sc = jnp.where(k0 + col <= i * tq + row, acc, -jnp.inf)
        keys_ref[kb] = sc
        keys_t_ref[kb] = sc.T
        return carry

    lax.fori_loop(0, nkb, score_block, 0)

    def fold_t(elem_fn, merge, init):
        def body(kb, part):
            e = elem_fn(keys_t_ref[kb])
            for s0 in range(0, tk, 8):
                part = merge(part, e[s0:s0 + 8])
            return part
        return lax.fori_loop(0, nkb, body, jnp.full((8, tq), init, F32))

    def count_t(pred):
        return jnp.sum(fold_t(lambda s: jnp.where(pred(s), 1.0, 0.0), jnp.add, 0.0), axis=0, keepdims=True)

    kf = float(topk)
    lo0 = jnp.min(fold_t(lambda s: jnp.where(s > -jnp.inf, s, jnp.inf), jnp.minimum, jnp.inf),
                  axis=0, keepdims=True)
    hi0 = jnp.max(fold_t(lambda s: s, jnp.maximum, -jnp.inf), axis=0, keepdims=True)

    def bisect(carry):
        it, lo, hi, _ = carry
        mid = 0.5 * lo + 0.5 * hi
        keep = count_t(lambda s: s >= mid) >= kf
        moving = jnp.max(jnp.where((mid > lo) & (mid < hi), 1.0, 0.0))
        return it + 1, jnp.where(keep, mid, lo), jnp.where(keep, hi, mid), moving

    def unresolved(carry):
        return (carry[0] < MAX_BISECT) & (carry[3] > 0.0)

    _, thr_t, _, _ = lax.while_loop(unresolved, bisect, (jnp.int32(0), lo0, hi0, jnp.float32(1.0)))
    surplus = count_t(lambda s: s >= thr_t) - kf
    thr = jnp.broadcast_to(thr_t, (tq, tq)).T[:, :LANES]

    def count(pred):
        def body(kb, part):
            kk = keys_ref[kb]
            for c in range(nch):
                part = part + jnp.where(pred(kk[:, c * LANES:(c + 1) * LANES], kb * tk + c * LANES), 1.0, 0.0)
            return part
        part = lax.fori_loop(0, nkb, body, jnp.zeros((tq, LANES), F32))
        return jnp.broadcast_to(jnp.sum(part, axis=1, keepdims=True), (tq, LANES))

    zero = jnp.zeros((tq, LANES), I32)

    @pl.when(jnp.max(surplus) > 0.0)
    def _():
        need = kf - count(lambda kk, _: kk > thr)

        def index_bit(bi, p):
            cand = p + lax.shift_left(jnp.int32(1), nbits - 1 - bi)
            f = count(lambda kk, base: (kk == thr) & (base + lane < cand))
            return jnp.where(f <= need, cand, p)

        pend = lax.fori_loop(0, nbits, index_bit, zero)

        def drop(kb, carry):
            kk = keys_ref[kb]
            parts = []
            for c in range(nch):
                kc = kk[:, c * LANES:(c + 1) * LANES]
                parts.append(jnp.where((kc == thr) & (kb * tk + c * LANES + lane >= pend), -jnp.inf, kc))
            keys_ref[kb] = jnp.concatenate(parts, axis=1)
            return carry

        lax.fori_loop(0, nkb, drop, 0)

    tsel = thr

    m_ref[...] = jnp.full(m_ref.shape, NEG_BIG, F32)
    l_ref[...] = jnp.zeros(l_ref.shape, F32)
    acc_ref[...] = jnp.zeros(acc_ref.shape, F32)

    def attend(kb, carry):
        k0 = pl.multiple_of(kb * tk, tk)
        kk = keys_ref[kb]
        sel = jnp.concatenate([jnp.where(kk[:, c * LANES:(c + 1) * LANES] >= tsel, 0.0, NEG_BIG)
                               for c in range(nch)], axis=1)
        for g in range(N_KV_HEADS):
            kg = k_ref[pl.ds(k0, tk), g * LANES:(g + 1) * LANES]
            vg = v_ref[pl.ds(k0, tk), g * LANES:(g + 1) * LANES]
            qg = jnp.concatenate([q_ref[:, (KV_REP * g + r) * LANES:(KV_REP * g + r + 1) * LANES]
                                  for r in range(KV_REP)], axis=0)
            s = lax.dot_general(qg, kg, nt, preferred_element_type=F32)
            s = jnp.concatenate([s[r * tq:(r + 1) * tq] + sel for r in range(KV_REP)], axis=0)
            m_old = m_ref[g]
            m_new = jnp.maximum(m_old, jnp.max(s, axis=1, keepdims=True))
            alpha = jnp.exp(m_old - m_new)
            p = jnp.exp(s - jnp.concatenate([m_new] * nch, axis=1))
            l_ref[g] = alpha * l_ref[g] + jnp.sum(p, axis=1, keepdims=True)
            acc_ref[g] = alpha * acc_ref[g] + jnp.dot(p.astype(BF16), vg, preferred_element_type=F32)
            m_ref[g] = m_new
        return carry

    lax.fori_loop(0, nkb, attend, 0)

    heads = []
    for g in range(N_KV_HEADS):
        og = acc_ref[g] / l_ref[g]
        heads += [og[r * tq:(r + 1) * tq] for r in range(KV_REP)]
    o_ref[...] = _rms(jnp.concatenate(heads, axis=1), g_ref[...]).astype(BF16)


def _dsa(q, qi, wi, k, v, ki, g_attn):
    bsz, seq, _ = q.shape
    tq = min(ATT_TILE, seq)
    nq = seq // tq
    topk = min(TOPK_MAX, seq // 4)
    nbits = int(seq).bit_length()
    qblk = lambda w: pl.BlockSpec((None, tq, w), lambda b, i: (b, i, 0))
    full = lambda w: pl.BlockSpec((None, seq, w), lambda b, i: (b, 0, 0), pipeline_mode=pl.Buffered(1))
    return pl.pallas_call(
        functools.partial(_dsa_body, topk, nbits),
        grid=(bsz, nq),
        in_specs=[qblk(D_ATT), qblk(D_QI), qblk(LANES), full(D_KV), full(D_KV), full(LANES),
                  pl.BlockSpec((1, D_ATT), lambda b, i: (0, 0))],
        out_specs=qblk(D_ATT),
        out_shape=jax.ShapeDtypeStruct((bsz, seq, D_ATT), BF16),
        scratch_shapes=[pltpu.VMEM((nq, tq, tq), F32),
                        pltpu.VMEM((nq, tq, tq), F32),
                        pltpu.VMEM((IDX_HEADS * tq, LANES), BF16),
                        pltpu.VMEM((IDX_HEADS * tq, LANES), F32),
                        pltpu.VMEM((N_KV_HEADS, KV_REP * tq, LANES), F32),
                        pltpu.VMEM((N_KV_HEADS, KV_REP * tq, LANES), F32),
                        pltpu.VMEM((N_KV_HEADS, KV_REP * tq, LANES), F32)],
        compiler_params=_params("arbitrary", "arbitrary"),
        name="dsa",
    )(q, qi, wi, k, v, ki, g_attn)


def _split_bf16(x):
    hi = x.astype(BF16)
    return hi, (x - hi.astype(F32)).astype(BF16)


def _out_proj_body(yc_ref, ya_ref, x_ref, ada_ref, wo_ref, gpm_ref, gpf_ref, wr_ref, br_ref,
                   x1_ref, h2_ref, idx_ref, gate_ref):
    tm = x_ref.shape[0]
    mix = (jnp.dot(yc_ref[...], wo_ref[0:D_CONV, :], preferred_element_type=F32)
           + jnp.dot(ya_ref[...], wo_ref[D_CONV:, :], preferred_element_type=F32))
    x1 = x_ref[...] + ada_ref[2:3, :] * _rms(mix, gpm_ref[...])
    x1_ref[...] = x1
    h2 = _rms(x1, gpf_ref[...]) * (1.0 + ada_ref[4:5, :]) + ada_ref[3:4, :]
    h2_ref[...] = h2

    h_hi, h_lo = _split_bf16(h2)
    w_hi, w_lo = _split_bf16(wr_ref[...])
    logits = (jnp.dot(h_hi, w_hi, preferred_element_type=F32) + jnp.dot(h_hi, w_lo, preferred_element_type=F32)
              + jnp.dot(h_lo, w_hi, preferred_element_type=F32)) + br_ref[...]
    lane = lax.broadcasted_iota(I32, (tm, LANES), 1).astype(F32)
    cur = jnp.where(lane < N_EXPERTS, logits, -jnp.inf)
    vals, idxs = [], []
    for _ in range(TOP_K):
        m = jnp.max(cur, axis=1, keepdims=True)
        am = jnp.min(jnp.where(cur == m, lane, float(LANES)), axis=1, keepdims=True)
        vals.append(m)
        idxs.append(am)
        cur = jnp.where(lane == am, -jnp.inf, cur)
    es = [jnp.exp(vv - vals[0]) for vv in vals]
    tot = es[0] + es[1] + es[2] + es[3]
    idx_out = jnp.zeros((tm, LANES), I32)
    gate_out = jnp.zeros((tm, LANES), F32)
    for kk in range(TOP_K):
        idx_out = jnp.where(lane == kk, idxs[kk].astype(I32), idx_out)
        gate_out = jnp.where(lane == kk, es[kk] / tot, gate_out)
    idx_ref[...] = idx_out
    gate_ref[...] = gate_out


def _out_proj(yc, ya, x2, ada3, w_out, g_post_mix, g_pre_ffn, w_router, b_router, seq):
    n, d = x2.shape
    tm = min(ROW_TILE, seq)
    tiles_per_batch = seq // tm
    row = lambda i: (i, 0)
    fixed = lambda i: (0, 0)
    return pl.pallas_call(
        _out_proj_body,
        grid=(n // tm,),
        in_specs=[pl.BlockSpec((tm, D_CONV), row),
                  pl.BlockSpec((tm, D_ATT), row),
                  pl.BlockSpec((tm, d), row),
                  pl.BlockSpec((None, 6, d), lambda i: (i // tiles_per_batch, 0, 0)),
                  pl.BlockSpec((D_CONV + D_ATT, d), fixed, pipeline_mode=pl.Buffered(1)),
                  pl.BlockSpec((1, d), fixed),
                  pl.BlockSpec((1, d), fixed),
                  pl.BlockSpec((d, LANES), fixed),
                  pl.BlockSpec((1, LANES), fixed)],
        out_specs=[pl.BlockSpec((tm, d), row), pl.BlockSpec((tm, d), row),
                   pl.BlockSpec((tm, LANES), row), pl.BlockSpec((tm, LANES), row)],
        out_shape=[jax.ShapeDtypeStruct((n, d), F32), jax.ShapeDtypeStruct((n, d), F32),
                   jax.ShapeDtypeStruct((n, LANES), I32), jax.ShapeDtypeStruct((n, LANES), F32)],
        compiler_params=_params("arbitrary"),
        name="out_proj",
    )(yc, ya, x2, ada3, w_out, g_post_mix, g_pre_ffn, w_router, b_router)


def _rank_body(idx_ref, rank_ref, cnt_ref, carry):
    i = pl.program_id(0)
    tm = idx_ref.shape[0]

    @pl.when(i == 0)
    def _():
        carry[...] = jnp.zeros(carry.shape, F32)

    lane = lax.broadcasted_iota(I32, (tm, LANES), 1)
    r = lax.broadcasted_iota(I32, (tm, tm), 0)
    c = lax.broadcasted_iota(I32, (tm, tm), 1)
    before = jnp.where(c < r, 1.0, 0.0).astype(BF16)
    idx = idx_ref[...]
    base = carry[0:1, :]
    out = jnp.zeros((tm, LANES), F32)
    for kk in range(TOP_K):
        onehot = jnp.where(lane == idx[:, kk:kk + 1], 1.0, 0.0)
        prefix = jnp.dot(before, onehot.astype(BF16), preferred_element_type=F32) + base
        rk = jnp.sum(onehot * prefix, axis=1, keepdims=True)
        out = jnp.where(lane == kk, rk, out)
        base = base + jnp.sum(onehot, axis=0, keepdims=True)
    rank_ref[...] = out.astype(I32)
    carry[0:1, :] = base
    cnt_ref[...] = jnp.broadcast_to(base, cnt_ref.shape).astype(I32)


def _rank(top_idx):
    n = top_idx.shape[0]
    tm = min(ROW_TILE, n)
    return pl.pallas_call(
        _rank_body,
        grid=(n // tm,),
        in_specs=[pl.BlockSpec((tm, LANES), lambda i: (i, 0))],
        out_specs=[pl.BlockSpec((tm, LANES), lambda i: (i, 0)), pl.BlockSpec((8, LANES), lambda i: (0, 0))],
        out_shape=[jax.ShapeDtypeStruct((n, LANES), I32), jax.ShapeDtypeStruct((8, LANES), I32)],
        scratch_shapes=[pltpu.VMEM((8, LANES), F32)],
        compiler_params=_params("arbitrary"),
        name="rank",
    )(top_idx)


def _dest_body(idx_ref, rank_ref, start_ref, o_ref):
    tm = idx_ref.shape[0]
    lane = lax.broadcasted_iota(I32, (tm, LANES), 1)
    idx = idx_ref[...]
    rank = rank_ref[...]
    out = jnp.zeros((tm, LANES), I32)
    for kk in range(TOP_K):
        start = jnp.sum(jnp.where(lane == idx[:, kk:kk + 1], start_ref[...], 0.0), axis=1, keepdims=True)
        out = jnp.where(lane == kk, start.astype(I32) + rank[:, kk:kk + 1], out)
    o_ref[...] = out


def _dest(top_idx, rank, starts):
    n = top_idx.shape[0]
    tm = min(2 * ROW_TILE, n)
    row = lambda i: (i, 0)
    return pl.pallas_call(
        _dest_body,
        grid=(n // tm,),
        in_specs=[pl.BlockSpec((tm, LANES), row), pl.BlockSpec((tm, LANES), row),
                  pl.BlockSpec((1, LANES), lambda i: (0, 0))],
        out_specs=pl.BlockSpec((tm, LANES), row),
        out_shape=jax.ShapeDtypeStruct((n, LANES), I32),
        compiler_params=_params("arbitrary"),
        name="dest",
    )(top_idx, rank, starts)


def _dispatch_body(dest_ref, valid_ref, h_ref, o_ref, zeros_ref, sem, zsem):
    i = pl.program_id(0)
    tm = h_ref.shape[0]
    blk = zeros_ref.shape[0]
    nb = o_ref.shape[0] // blk

    @pl.when(i == 0)
    def _():
        zeros_ref[...] = jnp.zeros(zeros_ref.shape, zeros_ref.dtype)

        def for_blocks(fn):
            for jb in range(nb):
                @pl.when(valid_ref[jb] < blk)
                def _():
                    fn(pltpu.make_async_copy(zeros_ref, o_ref.at[pl.ds(jb * blk, blk)], zsem))

        for_blocks(lambda cp: cp.start())
        for_blocks(lambda cp: cp.wait())

    def row_copy(r, kk):
        dst = dest_ref[(i * tm + r) * TOP_K + kk]
        return pltpu.make_async_copy(h_ref.at[pl.ds(r, 1)], o_ref.at[pl.ds(dst, 1)], sem)

    def issue(rb, carry):
        for jj in range(DMA_UNROLL):
            for kk in range(TOP_K):
                row_copy(rb * DMA_UNROLL + jj, kk).start()
        return carry

    lax.fori_loop(0, tm // DMA_UNROLL, issue, 0)
    for kk in range(TOP_K):
        pltpu.make_async_copy(h_ref, o_ref.at[pl.ds(0, tm)], sem).wait()


def _dispatch(dest_flat, block_valid, h2p, n_rows):
    n, dw = h2p.shape
    tm = min(ROW_TILE, n)
    return pl.pallas_call(
        _dispatch_body,
        grid_spec=pltpu.PrefetchScalarGridSpec(
            num_scalar_prefetch=2,
            grid=(n // tm,),
            in_specs=[pl.BlockSpec((tm, dw), lambda i, dst, bv: (i, 0))],
            out_specs=pl.BlockSpec(memory_space=pl.ANY),
            scratch_shapes=[pltpu.VMEM((MOE_BLOCK, dw), h2p.dtype), pltpu.SemaphoreType.DMA,
                            pltpu.SemaphoreType.DMA]),
        out_shape=jax.ShapeDtypeStruct((n_rows, dw), h2p.dtype),
        compiler_params=_params("arbitrary"),
        name="dispatch",
    )(dest_flat, block_valid, h2p)


def _experts_body(be_ref, ns_ref, rev_ref, nu_ref, x_ref, wgu_ref, wd_ref, bgu_ref, bd_ref, y_ref, xb_ref):
    j = pl.program_id(0)
    f = pl.program_id(1)
    nsub = ns_ref[j]

    @pl.when(f == 0)
    def _():
        y_ref[...] = jnp.broadcast_to(bd_ref[...], y_ref.shape)

    @pl.when((f == 0) & (nsub > 0))
    def _():
        xb_ref[...] = x_ref[...].astype(BF16)

    def compute(r0, r1):
        gu = jnp.dot(xb_ref[r0:r1, :], wgu_ref[...].astype(BF16), preferred_element_type=F32) + bgu_ref[...]
        gate = jnp.minimum(gu, SWIGLU_LIMIT)
        glu = gate * jax.nn.sigmoid(SWIGLU_ALPHA * gate)
        up1 = jnp.clip(gu, -SWIGLU_LIMIT, SWIGLU_LIMIT) + 1.0
        n2 = gu.shape[1]
        rows2 = lax.broadcasted_iota(I32, (2 * LANES, LANES), 0)
        cols2 = lax.broadcasted_iota(I32, (2 * LANES, LANES), 1)
        sel = jnp.where(rows2 == 2 * cols2, 1.0, 0.0).astype(BF16)
        parts = []
        for c in range(n2 // LANES):
            sl = slice(c * LANES, (c + 1) * LANES)
            parts.append((glu[:, sl] * pltpu.roll(up1[:, sl], LANES - 1, 1)).astype(BF16))
        acts = [jnp.dot(jnp.concatenate(parts[2 * c:2 * c + 2], axis=1), sel, preferred_element_type=F32)
                for c in range(n2 // (2 * LANES))]
        act = jnp.concatenate(acts, axis=1).astype(BF16)
        y_ref[r0:r1, :] += jnp.dot(act, wd_ref[...].astype(BF16), preferred_element_type=F32)

    for ns in range(1, MOE_BLOCK // MOE_SUB + 1):
        @pl.when(nsub == ns)
        def _():
            for r0 in range(0, ns * MOE_SUB, MOE_ROWS):
                compute(r0, min(r0 + MOE_ROWS, ns * MOE_SUB))


def _experts(block_e, block_nsub, block_rev, n_used, x_rows, w_gate_up, w_down, b_gate_up, b_down):
    p, d = x_rows.shape
    dw = d
    dff = w_down.shape[1]
    nb = p // MOE_BLOCK
    nf = dff // FF_TILE

    def blk(j, f, be, ns, rev, nu):
        return jnp.minimum(j, nu[0] - 1)

    def fidx(j, f, be, ns, rev, nu):
        back = rev[jnp.minimum(j, nu[0] - 1)] == 1
        return jnp.where(j < nu[0], jnp.where(back, nf - 1 - f, f), jnp.where(back, 0, nf - 1))

    return pl.pallas_call(
        _experts_body,
        grid_spec=pltpu.PrefetchScalarGridSpec(
            num_scalar_prefetch=4,
            grid=(nb, nf),
            in_specs=[pl.BlockSpec((MOE_BLOCK, dw), lambda j, f, *s: (blk(j, f, *s), 0)),
                      pl.BlockSpec((None, d, 2 * FF_TILE), lambda j, f, *s: (s[0][j], 0, fidx(j, f, *s))),
                      pl.BlockSpec((None, FF_TILE, d), lambda j, f, *s: (s[0][j], fidx(j, f, *s), 0)),
                      pl.BlockSpec((None, 1, 2 * FF_TILE), lambda j, f, *s: (s[0][j], 0, fidx(j, f, *s))),
                      pl.BlockSpec((None, 1, d), lambda j, f, *s: (s[0][j], 0, 0))],
            out_specs=pl.BlockSpec((MOE_BLOCK, d), lambda j, f, *s: (j, 0)),
            scratch_shapes=[pltpu.VMEM((MOE_BLOCK, d), BF16)]),
        out_shape=jax.ShapeDtypeStruct((p, d), F32),
        compiler_params=_params("arbitrary", "arbitrary"),
        name="experts",
    )(block_e, block_nsub, block_rev, n_used, x_rows, w_gate_up, w_down, b_gate_up, b_down)


def _combine_body(tiles_per_batch, dest_ref, y_ref, gate_ref, x1_ref, ada_ref, g_ref, o_ref, buf, sem):
    i = pl.program_id(0)
    n_tiles = pl.num_programs(0)
    tm = x1_ref.shape[0]

    def row_copy(tile, slot, r, kk):
        src = dest_ref[(tile * tm + r) * TOP_K + kk]
        return pltpu.make_async_copy(y_ref.at[pl.ds(src, 1)], buf.at[slot, kk, pl.ds(r, 1)], sem.at[slot])

    def issue(tile, slot):
        def body(rb, carry):
            for jj in range(DMA_UNROLL):
                for kk in range(TOP_K):
                    row_copy(tile, slot, rb * DMA_UNROLL + jj, kk).start()
            return carry
        lax.fori_loop(0, tm // DMA_UNROLL, body, 0)

    def drain(tile, slot):
        for kk in range(TOP_K):
            pltpu.make_async_copy(y_ref.at[pl.ds(0, tm)], buf.at[slot, kk], sem.at[slot]).wait()

    slot = i % 2

    @pl.when(i == 0)
    def _():
        issue(0, 0)

    @pl.when(i + 1 < n_tiles)
    def _():
        issue(i + 1, 1 - slot)

    drain(i, slot)
    gates = gate_ref[...]
    y = jnp.zeros(x1_ref.shape, F32)
    for kk in range(TOP_K):
        y = y + gates[:, kk:kk + 1] * buf[slot, kk]
    o_ref[...] = x1_ref[...] + ada_ref[5:6, :] * _rms(y, g_ref[...])


def _combine(dest_flat, y_rows, gates, x1, ada3, g_post_ffn, seq):
    n, d = x1.shape
    tm = min(GATHER_TILE, seq)
    tiles_per_batch = seq // tm
    return pl.pallas_call(
        functools.partial(_combine_body, tiles_per_batch),
        grid_spec=pltpu.PrefetchScalarGridSpec(
            num_scalar_prefetch=1,
            grid=(n // tm,),
            in_specs=[pl.BlockSpec(memory_space=pl.ANY),
                      pl.BlockSpec((tm, LANES), lambda i, dst: (i, 0)),
                      pl.BlockSpec((tm, d), lambda i, dst: (i, 0)),
                      pl.BlockSpec((None, 6, d), lambda i, dst: (i // tiles_per_batch, 0, 0)),
                      pl.BlockSpec((1, d), lambda i, dst: (0, 0))],
            out_specs=pl.BlockSpec((tm, d), lambda i, dst: (i, 0)),
            scratch_shapes=[pltpu.VMEM((2, TOP_K, tm, d), F32), pltpu.SemaphoreType.DMA((2,))]),
        out_shape=jax.ShapeDtypeStruct((n, d), F32),
        compiler_params=_params("arbitrary"),
        name="combine",
    )(dest_flat, y_rows, gates, x1, ada3, g_post_ffn)


def _tail_w_in(w_in):
    d = w_in.shape[0]
    o_wi = C_KI + IDX_DIM
    ki = w_in[:, C_KI:o_wi]
    wi = w_in[:, o_wi:o_wi + IDX_HEADS]
    pad = jnp.zeros((d, LANES - IDX_HEADS), w_in.dtype)
    return jnp.concatenate([ki, ki, wi, pad], axis=1)


def _layer(x, c, positions, w_ada, b_ada, g_pre_mix, g_post_mix, w_in, conv_w, g_conv_out, g_attn_out, w_out,
           g_pre_ffn, g_post_ffn, w_router, b_router, w_gate_up, b_gate_up, w_down, b_down):
    bsz, seq, d = x.shape
    n = bsz * seq
    x2 = x.reshape(n, d)
    ada3 = _ada(c, w_ada, b_ada).reshape(bsz, 6, d)
    w_in_b = w_in.astype(BF16)

    yc, q, k, v, qi, ki, wi = _in_proj(
        x2, ada3, g_pre_mix.reshape(1, d), positions.reshape(n, 1).astype(I32), w_in_b, _tail_w_in(w_in_b),
        conv_w.reshape(CONV_WIDTH, D_CONV), g_conv_out.reshape(1, D_CONV), seq)
    b3 = lambda a: a.reshape(bsz, seq, a.shape[-1])
    ya = _dsa(b3(q), b3(qi), b3(wi), b3(k), b3(v), b3(ki), g_attn_out.reshape(1, D_ATT)).reshape(n, D_ATT)

    wr = jnp.zeros((d, LANES), F32).at[:, :N_EXPERTS].set(w_router)
    br = jnp.zeros((1, LANES), F32).at[0, :N_EXPERTS].set(b_router)
    x1, h2, top_idx, gates = _out_proj(yc, ya, x2, ada3, w_out.astype(BF16), g_post_mix.reshape(1, d),
                                       g_pre_ffn.reshape(1, d), wr, br, seq)

    rank, cnt = _rank(top_idx)
    counts = cnt[0, :N_EXPERTS]
    padded = (counts + MOE_BLOCK - 1) // MOE_BLOCK * MOE_BLOCK
    pad_ends = jnp.cumsum(padded)
    pad_starts = pad_ends - padded
    nb = -(-(n * TOP_K) // MOE_BLOCK) + N_EXPERTS
    starts_row = jnp.zeros((1, LANES), F32).at[0, :N_EXPERTS].set(pad_starts.astype(F32))
    dest = _dest(top_idx, rank, starts_row)[:, :TOP_K].reshape(n * TOP_K)
    block_start = jnp.arange(nb, dtype=I32) * MOE_BLOCK
    block_e = jnp.minimum(jnp.sum((pad_ends[None, :] <= block_start[:, None]).astype(I32), axis=1),
                          N_EXPERTS - 1)
    n_used = (pad_ends[-1:] // MOE_BLOCK).astype(I32)
    block_valid = jnp.clip(counts[block_e] - (block_start - pad_starts[block_e]), 0, MOE_BLOCK)
    block_valid = jnp.where(block_start < pad_ends[-1], block_valid, 0).astype(I32)
    block_nsub = (block_valid + MOE_SUB - 1) // MOE_SUB
    block_e = jnp.where(block_start < pad_ends[-1], block_e, block_e[n_used[0] - 1])
    block_rev = ((block_start - pad_starts[block_e]) // MOE_BLOCK) % 2

    x_rows = _dispatch(dest, block_valid, h2, nb * MOE_BLOCK)
    y_rows = _experts(block_e, block_nsub, block_rev.astype(I32), n_used, x_rows, w_gate_up, w_down,
                      b_gate_up[:, None, :], b_down[:, None, :])
    out = _combine(dest, y_rows, gates, x1, ada3, g_post_ffn.reshape(1, d), seq)
    return out.reshape(bsz, seq, d)


def kernel(x, c, positions, w_ada, b_ada, g_pre_mix, g_post_mix, w_in, conv_w, g_conv_out, g_attn_out, w_out,
           g_pre_ffn, g_post_ffn, w_router, b_router, w_gate_up, b_gate_up, w_down, b_down):
    for l in range(w_ada.shape[0]):
        x = _layer(x, c, positions, w_ada[l], b_ada[l], g_pre_mix[l], g_post_mix[l], w_in[l], conv_w[l],
                   g_conv_out[l], g_attn_out[l], w_out[l], g_pre_ffn[l], g_post_ffn[l], w_router[l], b_router[l],
                   w_gate_up[l], b_gate_up[l], w_down[l], b_down[l])
    return x
```

```python
import functools

import numpy as np
import jax
import jax.numpy as jnp
from jax import lax
from jax.experimental import pallas as pl
from jax.experimental.pallas import tpu as pltpu

F32 = jnp.float32
BF16 = jnp.bfloat16
I32 = jnp.int32

EPS = 1e-6
LANES = 128
D_CONV = 1024
CONV_WIDTH = 3
N_HEADS = 8
N_KV_HEADS = 2
HEAD_DIM = 128
KV_REP = N_HEADS // N_KV_HEADS
ROPE_DIM = HEAD_DIM // 4
ROPE_THETA = 500000.0
IDX_HEADS = 16
IDX_DIM = 64
IDX_ROPE_DIM = IDX_DIM // 4
TOPK_MAX = 256
N_EXPERTS = 32
TOP_K = 4
SWIGLU_LIMIT = 7.0
SWIGLU_ALPHA = 1.702

D_ATT = N_HEADS * HEAD_DIM
D_KV = N_KV_HEADS * HEAD_DIM
D_QI = IDX_HEADS * IDX_DIM

INT_MIN = -2147483648
NEG_BIG = -1e30

ROW_TILE = 512
ATT_TILE = 512
MOE_BLOCK = 512
MOE_SUB = 256
MOE_ROWS = 512
FF_TILE = 512
GATHER_TILE = 128
DMA_UNROLL = 8
VMEM_LIMIT = 56 * 1024 * 1024


def _params(*sem):
    return pltpu.CompilerParams(dimension_semantics=sem, vmem_limit_bytes=VMEM_LIMIT)


def _rms(x, g):
    return x * lax.rsqrt(jnp.mean(x * x, axis=-1, keepdims=True) + EPS) * g


def _ada_body(c_ref, w_ref, b_ref, o_ref):
    c = c_ref[...]
    s = (c * jax.nn.sigmoid(c)).astype(BF16)
    o_ref[...] = jnp.dot(s, w_ref[...].astype(BF16), preferred_element_type=F32) + b_ref[...]


def _ada(c, w, b):
    bsz, d = c.shape
    n = w.shape[1]
    tn = 1536
    rows = 8
    cp = jnp.zeros((rows, d), F32).at[:bsz].set(c)
    out = pl.pallas_call(
        _ada_body,
        grid=(n // tn,),
        in_specs=[pl.BlockSpec((rows, d), lambda j: (0, 0)),
                  pl.BlockSpec((d, tn), lambda j: (0, j)),
                  pl.BlockSpec((1, tn), lambda j: (0, j))],
        out_specs=pl.BlockSpec((rows, tn), lambda j: (0, j)),
        out_shape=jax.ShapeDtypeStruct((rows, n), F32),
        compiler_params=_params("arbitrary"),
        name="ada",
    )(cp, w, b.reshape(1, n))
    return out[:bsz]


C_BCU = 0
C_Q = 3 * D_CONV
C_K = C_Q + D_ATT
C_V = C_K + D_KV
C_QI = C_V + D_KV
C_KI = C_QI + D_QI


def _rope_tables(pos, inv_freq, width, rot):
    half = rot // 2
    rows = pos.shape[0]
    lane = lax.broadcasted_iota(I32, (rows, LANES), 1) & (width - 1)
    ang = pos * inv_freq
    cos = jnp.cos(ang)
    sin = jnp.sin(ang)
    c = jnp.where(lane < rot, cos, 1.0)
    a = jnp.where((lane >= half) & (lane < rot), sin, 0.0)
    b = jnp.where(lane < half, -sin, 0.0)
    return c, a, b, half


def _rope(x, tabs):
    c, a, b, half = tabs
    return x * c + pltpu.roll(x, half, 1) * a + pltpu.roll(x, LANES - half, 1) * b


def _in_proj_body(tiles_per_batch, x_ref, ada_ref, g_ref, pos_ref, w_ref, wt_ref, cw_ref, gc_ref, fq_ref, fi_ref,
                  yc_ref, q_ref, k_ref, v_ref, qi_ref, ki_ref, wi_ref, vbuf):
    i = pl.program_id(0)
    tm = x_ref.shape[0]
    h = _rms(x_ref[...], g_ref[...]) * (1.0 + ada_ref[1:2, :]) + ada_ref[0:1, :]
    hb = h.astype(BF16)

    def proj(lo, hi):
        return jnp.dot(hb, w_ref[:, lo:hi], preferred_element_type=F32)

    @pl.when(i % tiles_per_batch == 0)
    def _():
        vbuf[0:8, :] = jnp.zeros((8, D_CONV), F32)

    bcu = proj(C_BCU, C_Q)
    v0 = bcu[:, D_CONV:2 * D_CONV] * bcu[:, 2 * D_CONV:]
    vbuf[8:8 + tm, :] = v0
    v1 = vbuf[7:7 + tm, :]
    v2 = vbuf[6:6 + tm, :]
    y = cw_ref[0:1, :] * v2 + cw_ref[1:2, :] * v1 + cw_ref[2:3, :] * v0
    yc_ref[...] = _rms(bcu[:, :D_CONV] * y, gc_ref[...]).astype(BF16)
    vbuf[0:8, :] = vbuf[tm:tm + 8, :]

    pos = pos_ref[...].astype(F32)
    tq = _rope_tables(pos, fq_ref[...], HEAD_DIM, ROPE_DIM)
    ti = _rope_tables(pos, fi_ref[...], IDX_DIM, IDX_ROPE_DIM)
    scale = HEAD_DIM ** -0.5

    qf = proj(C_Q, C_K)
    for hh in range(N_HEADS):
        sl = slice(hh * LANES, (hh + 1) * LANES)
        q_ref[:, sl] = (_rope(qf[:, sl], tq) * scale).astype(BF16)
    kf = proj(C_K, C_V)
    for hh in range(N_KV_HEADS):
        sl = slice(hh * LANES, (hh + 1) * LANES)
        k_ref[:, sl] = _rope(kf[:, sl], tq).astype(BF16)
    v_ref[...] = proj(C_V, C_QI).astype(BF16)
    qif = proj(C_QI, C_KI)
    for hh in range(D_QI // LANES):
        sl = slice(hh * LANES, (hh + 1) * LANES)
        qi_ref[:, sl] = _rope(qif[:, sl], ti).astype(BF16)
    tail = jnp.dot(hb, wt_ref[...], preferred_element_type=F32)
    ki_ref[...] = _rope(tail[:, :LANES], ti).astype(BF16)
    wi_ref[...] = tail[:, LANES:]


def _in_proj(x2, ada3, g_pre, pos2, w_main, w_tail, conv_w, g_conv, seq):
    n, d = x2.shape
    tm = min(ROW_TILE, seq)
    tiles_per_batch = seq // tm
    half_q = ROPE_DIM // 2
    half_i = IDX_ROPE_DIM // 2
    lane = np.arange(LANES)
    fq = jnp.asarray(ROPE_THETA, F32) ** (-jnp.asarray(lane % half_q, F32) / half_q)
    fi = jnp.asarray(ROPE_THETA, F32) ** (-jnp.asarray(lane % half_i, F32) / half_i)
    row = lambda i: (i, 0)
    fixed = lambda i: (0, 0)
    outs = [(D_CONV, BF16), (D_ATT, BF16), (D_KV, BF16), (D_KV, BF16), (D_QI, BF16), (LANES, BF16), (LANES, F32)]
    return pl.pallas_call(
        functools.partial(_in_proj_body, tiles_per_batch),
        grid=(n // tm,),
        in_specs=[pl.BlockSpec((tm, d), row),
                  pl.BlockSpec((None, 6, d), lambda i: (i // tiles_per_batch, 0, 0)),
                  pl.BlockSpec((1, d), fixed),
                  pl.BlockSpec((tm, 1), row),
                  pl.BlockSpec((d, C_KI), fixed, pipeline_mode=pl.Buffered(1)),
                  pl.BlockSpec((d, 2 * LANES), fixed),
                  pl.BlockSpec((CONV_WIDTH, D_CONV), fixed),
                  pl.BlockSpec((1, D_CONV), fixed),
                  pl.BlockSpec((1, LANES), fixed),
                  pl.BlockSpec((1, LANES), fixed)],
        out_specs=[pl.BlockSpec((tm, w), row) for w, _ in outs],
        out_shape=[jax.ShapeDtypeStruct((n, w), dt) for w, dt in outs],
        scratch_shapes=[pltpu.VMEM((tm + 8, D_CONV), F32)],
        compiler_params=_params("arbitrary"),
        name="in_proj",
    )(x2, ada3, g_pre, pos2, w_main, w_tail, conv_w, g_conv, fq.reshape(1, LANES), fi.reshape(1, LANES))


MAX_BISECT = 320

def _dsa_body(topk, nbits, q_ref, qi_ref, wi_ref, k_ref, v_ref, ki_ref, g_ref, o_ref,
              keys_ref, keys_t_ref, qm_ref, wb_ref, m_ref, l_ref, acc_ref):
    i = pl.program_id(1)
    tq = q_ref.shape[0]
    tk = tq
    nch = tk // LANES
    nkb = i + 1
    nt = (((1,), (1,)), ((), ()))

    lane = lax.broadcasted_iota(I32, (tq, LANES), 1)
    for p in range(IDX_HEADS // 2):
        qp = qi_ref[:, p * LANES:(p + 1) * LANES].astype(F32)
        qm_ref[(2 * p) * tq:(2 * p + 1) * tq, :] = jnp.where(lane < IDX_DIM, qp, 0.0).astype(BF16)
        qm_ref[(2 * p + 1) * tq:(2 * p + 2) * tq, :] = jnp.where(lane >= IDX_DIM, qp, 0.0).astype(BF16)
    w = wi_ref[...]
    for hh in range(IDX_HEADS):
        wb_ref[hh * tq:(hh + 1) * tq, :] = jnp.broadcast_to(w[:, hh:hh + 1], (tq, LANES))

    row = lax.broadcasted_iota(I32, (tq, tk), 0)
    col = lax.broadcasted_iota(I32, (tq, tk), 1)

    def score_block(kb, carry):
        k0 = pl.multiple_of(kb * tk, tk)
        kib = ki_ref[pl.ds(k0, tk), :]
        acc = jnp.zeros((tq, tk), F32)
        for j in range(IDX_HEADS // 4):
            d = lax.dot_general(qm_ref[4 * j * tq:4 * (j + 1) * tq, :], kib, nt, preferred_element_type=F32)
            wbj = wb_ref[4 * j * tq:4 * (j + 1) * tq, :]
            e = jnp.maximum(d, 0.0) * jnp.concatenate([wbj] * nch, axis=1)
            acc = acc + ((e[0:tq] + e[tq:2 * tq]) + (e[2 * tq:3 * tq] + e[3 * tq:4 * tq]))
        sc = jnp.where(k0 + col <= i * tq + row, acc, -jnp.inf)
        keys_ref[kb] = sc
        keys_t_ref[kb] = sc.T
        return carry

    lax.fori_loop(0, nkb, score_block, 0)

    def fold_t(elem_fn, merge, init):
        def body(kb, part):
            e = elem_fn(keys_t_ref[kb])
            for s0 in range(0, tk, 8):
                part = merge(part, e[s0:s0 + 8])
            return part
        return lax.fori_loop(0, nkb, body, jnp.full((8, tq), init, F32))

    def count_t(pred):
        return jnp.sum(fold_t(lambda s: jnp.where(pred(s), 1.0, 0.0), jnp.add, 0.0), axis=0, keepdims=True)

    kf = float(topk)
    lo0 = jnp.min(fold_t(lambda s: jnp.where(s > -jnp.inf, s, jnp.inf), jnp.minimum, jnp.inf),
                  axis=0, keepdims=True)
    hi0 = jnp.max(fold_t(lambda s: s, jnp.maximum, -jnp.inf), axis=0, keepdims=True)

    def bisect(carry):
        it, lo, hi, _ = carry
        mid = 0.5 * lo + 0.5 * hi
        keep = count_t(lambda s: s >= mid) >= kf
        moving = jnp.max(jnp.where((mid > lo) & (mid < hi), 1.0, 0.0))
        return it + 1, jnp.where(keep, mid, lo), jnp.where(keep, hi, mid), moving

    def unresolved(carry):
        return (carry[0] < MAX_BISECT) & (carry[3] > 0.0)

    _, thr_t, _, _ = lax.while_loop(unresolved, bisect, (jnp.int32(0), lo0, hi0, jnp.float32(1.0)))
    surplus = count_t(lambda s: s >= thr_t) - kf
    thr = jnp.broadcast_to(thr_t, (tq, tq)).T[:, :LANES]

    def count(pred):
        def body(kb, part):
            kk = keys_ref[kb]
            for c in range(nch):
                part = part + jnp.where(pred(kk[:, c * LANES:(c + 1) * LANES], kb * tk + c * LANES), 1.0, 0.0)
            return part
        part = lax.fori_loop(0, nkb, body, jnp.zeros((tq, LANES), F32))
        return jnp.broadcast_to(jnp.sum(part, axis=1, keepdims=True), (tq, LANES))

    zero = jnp.zeros((tq, LANES), I32)

    @pl.when(jnp.max(surplus) > 0.0)
    def _():
        need = kf - count(lambda kk, _: kk > thr)

        def index_bit(bi, p):
            cand = p + lax.shift_left(jnp.int32(1), nbits - 1 - bi)
            f = count(lambda kk, base: (kk == thr) & (base + lane < cand))
            return jnp.where(f <= need, cand, p)

        pend = lax.fori_loop(0, nbits, index_bit, zero)

        def drop(kb, carry):
            kk = keys_ref[kb]
            parts = []
            for c in range(nch):
                kc = kk[:, c * LANES:(c + 1) * LANES]
                parts.append(jnp.where((kc == thr) & (kb * tk + c * LANES + lane >= pend), -jnp.inf, kc))
            keys_ref[kb] = jnp.concatenate(parts, axis=1)
            return carry

        lax.fori_loop(0, nkb, drop, 0)

    tsel = thr

    m_ref[...] = jnp.full(m_ref.shape, NEG_BIG, F32)
    l_ref[...] = jnp.zeros(l_ref.shape, F32)
    acc_ref[...] = jnp.zeros(acc_ref.shape, F32)

    def attend(kb, carry):
        k0 = pl.multiple_of(kb * tk, tk)
        kk = keys_ref[kb]
        sel = jnp.concatenate([jnp.where(kk[:, c * LANES:(c + 1) * LANES] >= tsel, 0.0, NEG_BIG)
                               for c in range(nch)], axis=1)
        for g in range(N_KV_HEADS):
            kg = k_ref[pl.ds(k0, tk), g * LANES:(g + 1) * LANES]
            vg = v_ref[pl.ds(k0, tk), g * LANES:(g + 1) * LANES]
            qg = jnp.concatenate([q_ref[:, (KV_REP * g + r) * LANES:(KV_REP * g + r + 1) * LANES]
                                  for r in range(KV_REP)], axis=0)
            s = lax.dot_general(qg, kg, nt, preferred_element_type=F32)
            s = jnp.concatenate([s[r * tq:(r + 1) * tq] + sel for r in range(KV_REP)], axis=0)
            m_old = m_ref[g]
            m_new = jnp.maximum(m_old, jnp.max(s, axis=1, keepdims=True))
            alpha = jnp.exp(m_old - m_new)
            p = jnp.exp(s - jnp.concatenate([m_new] * nch, axis=1))
            l_ref[g] = alpha * l_ref[g] + jnp.sum(p, axis=1, keepdims=True)
            acc_ref[g] = alpha * acc_ref[g] + jnp.dot(p.astype(BF16), vg, preferred_element_type=F32)
            m_ref[g] = m_new
        return carry

    lax.fori_loop(0, nkb, attend, 0)

    heads = []
    for g in range(N_KV_HEADS):
        og = acc_ref[g] / l_ref[g]
        heads += [og[r * tq:(r + 1) * tq] for r in range(KV_REP)]
    o_ref[...] = _rms(jnp.concatenate(heads, axis=1), g_ref[...]).astype(BF16)


def _dsa(q, qi, wi, k, v, ki, g_attn):
    bsz, seq, _ = q.shape
    tq = min(ATT_TILE, seq)
    nq = seq // tq
    topk = min(TOPK_MAX, seq // 4)
    nbits = int(seq).bit_length()
    qblk = lambda w: pl.BlockSpec((None, tq, w), lambda b, i: (b, i, 0))
    full = lambda w: pl.BlockSpec((None, seq, w), lambda b, i: (b, 0, 0), pipeline_mode=pl.Buffered(1))
    return pl.pallas_call(
        functools.partial(_dsa_body, topk, nbits),
        grid=(bsz, nq),
        in_specs=[qblk(D_ATT), qblk(D_QI), qblk(LANES), full(D_KV), full(D_KV), full(LANES),
                  pl.BlockSpec((1, D_ATT), lambda b, i: (0, 0))],
        out_specs=qblk(D_ATT),
        out_shape=jax.ShapeDtypeStruct((bsz, seq, D_ATT), BF16),
        scratch_shapes=[pltpu.VMEM((nq, tq, tq), F32),
                        pltpu.VMEM((nq, tq, tq), F32),
                        pltpu.VMEM((IDX_HEADS * tq, LANES), BF16),
                        pltpu.VMEM((IDX_HEADS * tq, LANES), F32),
                        pltpu.VMEM((N_KV_HEADS, KV_REP * tq, LANES), F32),
                        pltpu.VMEM((N_KV_HEADS, KV_REP * tq, LANES), F32),
                        pltpu.VMEM((N_KV_HEADS, KV_REP * tq, LANES), F32)],
        compiler_params=_params("arbitrary", "arbitrary"),
        name="dsa",
    )(q, qi, wi, k, v, ki, g_attn)


def _split_bf16(x):
    hi = x.astype(BF16)
    return hi, (x - hi.astype(F32)).astype(BF16)


def _out_proj_body(yc_ref, ya_ref, x_ref, ada_ref, wo_ref, gpm_ref, gpf_ref, wr_ref, br_ref,
                   x1_ref, h2_ref, idx_ref, gate_ref):
    tm = x_ref.shape[0]
    mix = (jnp.dot(yc_ref[...], wo_ref[0:D_CONV, :], preferred_element_type=F32)
           + jnp.dot(ya_ref[...], wo_ref[D_CONV:, :], preferred_element_type=F32))
    x1 = x_ref[...] + ada_ref[2:3, :] * _rms(mix, gpm_ref[...])
    x1_ref[...] = x1
    h2 = _rms(x1, gpf_ref[...]) * (1.0 + ada_ref[4:5, :]) + ada_ref[3:4, :]
    h2_ref[...] = h2

    h_hi, h_lo = _split_bf16(h2)
    w_hi, w_lo = _split_bf16(wr_ref[...])
    logits = (jnp.dot(h_hi, w_hi, preferred_element_type=F32) + jnp.dot(h_hi, w_lo, preferred_element_type=F32)
              + jnp.dot(h_lo, w_hi, preferred_element_type=F32)) + br_ref[...]
    lane = lax.broadcasted_iota(I32, (tm, LANES), 1).astype(F32)
    cur = jnp.where(lane < N_EXPERTS, logits, -jnp.inf)
    vals, idxs = [], []
    for _ in range(TOP_K):
        m = jnp.max(cur, axis=1, keepdims=True)
        am = jnp.min(jnp.where(cur == m, lane, float(LANES)), axis=1, keepdims=True)
        vals.append(m)
        idxs.append(am)
        cur = jnp.where(lane == am, -jnp.inf, cur)
    es = [jnp.exp(vv - vals[0]) for vv in vals]
    tot = es[0] + es[1] + es[2] + es[3]
    idx_out = jnp.zeros((tm, LANES), I32)
    gate_out = jnp.zeros((tm, LANES), F32)
    for kk in range(TOP_K):
        idx_out = jnp.where(lane == kk, idxs[kk].astype(I32), idx_out)
        gate_out = jnp.where(lane == kk, es[kk] / tot, gate_out)
    idx_ref[...] = idx_out
    gate_ref[...] = gate_out


def _out_proj(yc, ya, x2, ada3, w_out, g_post_mix, g_pre_ffn, w_router, b_router, seq):
    n, d = x2.shape
    tm = min(ROW_TILE, seq)
    tiles_per_batch = seq // tm
    row = lambda i: (i, 0)
    fixed = lambda i: (0, 0)
    return pl.pallas_call(
        _out_proj_body,
        grid=(n // tm,),
        in_specs=[pl.BlockSpec((tm, D_CONV), row),
                  pl.BlockSpec((tm, D_ATT), row),
                  pl.BlockSpec((tm, d), row),
                  pl.BlockSpec((None, 6, d), lambda i: (i // tiles_per_batch, 0, 0)),
                  pl.BlockSpec((D_CONV + D_ATT, d), fixed, pipeline_mode=pl.Buffered(1)),
                  pl.BlockSpec((1, d), fixed),
                  pl.BlockSpec((1, d), fixed),
                  pl.BlockSpec((d, LANES), fixed),
                  pl.BlockSpec((1, LANES), fixed)],
        out_specs=[pl.BlockSpec((tm, d), row), pl.BlockSpec((tm, d), row),
                   pl.BlockSpec((tm, LANES), row), pl.BlockSpec((tm, LANES), row)],
        out_shape=[jax.ShapeDtypeStruct((n, d), F32), jax.ShapeDtypeStruct((n, d), F32),
                   jax.ShapeDtypeStruct((n, LANES), I32), jax.ShapeDtypeStruct((n, LANES), F32)],
        compiler_params=_params("arbitrary"),
        name="out_proj",
    )(yc, ya, x2, ada3, w_out, g_post_mix, g_pre_ffn, w_router, b_router)


def _rank_body(idx_ref, rank_ref, cnt_ref, carry):
    i = pl.program_id(0)
    tm = idx_ref.shape[0]

    @pl.when(i == 0)
    def _():
        carry[...] = jnp.zeros(carry.shape, F32)

    lane = lax.broadcasted_iota(I32, (tm, LANES), 1)
    r = lax.broadcasted_iota(I32, (tm, tm), 0)
    c = lax.broadcasted_iota(I32, (tm, tm), 1)
    before = jnp.where(c < r, 1.0, 0.0).astype(BF16)
    idx = idx_ref[...]
    base = carry[0:1, :]
    out = jnp.zeros((tm, LANES), F32)
    for kk in range(TOP_K):
        onehot = jnp.where(lane == idx[:, kk:kk + 1], 1.0, 0.0)
        prefix = jnp.dot(before, onehot.astype(BF16), preferred_element_type=F32) + base
        rk = jnp.sum(onehot * prefix, axis=1, keepdims=True)
        out = jnp.where(lane == kk, rk, out)
        base = base + jnp.sum(onehot, axis=0, keepdims=True)
    rank_ref[...] = out.astype(I32)
    carry[0:1, :] = base
    cnt_ref[...] = jnp.broadcast_to(base, cnt_ref.shape).astype(I32)


def _rank(top_idx):
    n = top_idx.shape[0]
    tm = min(ROW_TILE, n)
    return pl.pallas_call(
        _rank_body,
        grid=(n // tm,),
        in_specs=[pl.BlockSpec((tm, LANES), lambda i: (i, 0))],
        out_specs=[pl.BlockSpec((tm, LANES), lambda i: (i, 0)), pl.BlockSpec((8, LANES), lambda i: (0, 0))],
        out_shape=[jax.ShapeDtypeStruct((n, LANES), I32), jax.ShapeDtypeStruct((8, LANES), I32)],
        scratch_shapes=[pltpu.VMEM((8, LANES), F32)],
        compiler_params=_params("arbitrary"),
        name="rank",
    )(top_idx)


def _dest_body(idx_ref, rank_ref, start_ref, o_ref):
    tm = idx_ref.shape[0]
    lane = lax.broadcasted_iota(I32, (tm, LANES), 1)
    idx = idx_ref[...]
    rank = rank_ref[...]
    out = jnp.zeros((tm, LANES), I32)
    for kk in range(TOP_K):
        start = jnp.sum(jnp.where(lane == idx[:, kk:kk + 1], start_ref[...], 0.0), axis=1, keepdims=True)
        out = jnp.where(lane == kk, start.astype(I32) + rank[:, kk:kk + 1], out)
    o_ref[...] = out


def _dest(top_idx, rank, starts):
    n = top_idx.shape[0]
    tm = min(2 * ROW_TILE, n)
    row = lambda i: (i, 0)
    return pl.pallas_call(
        _dest_body,
        grid=(n // tm,),
        in_specs=[pl.BlockSpec((tm, LANES), row), pl.BlockSpec((tm, LANES), row),
                  pl.BlockSpec((1, LANES), lambda i: (0, 0))],
        out_specs=pl.BlockSpec((tm, LANES), row),
        out_shape=jax.ShapeDtypeStruct((n, LANES), I32),
        compiler_params=_params("arbitrary"),
        name="dest",
    )(top_idx, rank, starts)


def _dispatch_body(dest_ref, valid_ref, h_ref, o_ref, zeros_ref, sem, zsem):
    i = pl.program_id(0)
    tm = h_ref.shape[0]
    blk = zeros_ref.shape[0]
    nb = o_ref.shape[0] // blk

    @pl.when(i == 0)
    def _():
        zeros_ref[...] = jnp.zeros(zeros_ref.shape, zeros_ref.dtype)

        def for_blocks(fn):
            for jb in range(nb):
                @pl.when(valid_ref[jb] < blk)
                def _():
                    fn(pltpu.make_async_copy(zeros_ref, o_ref.at[pl.ds(jb * blk, blk)], zsem))

        for_blocks(lambda cp: cp.start())
        for_blocks(lambda cp: cp.wait())

    def row_copy(r, kk):
        dst = dest_ref[(i * tm + r) * TOP_K + kk]
        return pltpu.make_async_copy(h_ref.at[pl.ds(r, 1)], o_ref.at[pl.ds(dst, 1)], sem)

    def issue(rb, carry):
        for jj in range(DMA_UNROLL):
            for kk in range(TOP_K):
                row_copy(rb * DMA_UNROLL + jj, kk).start()
        return carry

    lax.fori_loop(0, tm // DMA_UNROLL, issue, 0)
    for kk in range(TOP_K):
        pltpu.make_async_copy(h_ref, o_ref.at[pl.ds(0, tm)], sem).wait()


def _dispatch(dest_flat, block_valid, h2p, n_rows):
    n, dw = h2p.shape
    tm = min(ROW_TILE, n)
    return pl.pallas_call(
        _dispatch_body,
        grid_spec=pltpu.PrefetchScalarGridSpec(
            num_scalar_prefetch=2,
            grid=(n // tm,),
            in_specs=[pl.BlockSpec((tm, dw), lambda i, dst, bv: (i, 0))],
            out_specs=pl.BlockSpec(memory_space=pl.ANY),
            scratch_shapes=[pltpu.VMEM((MOE_BLOCK, dw), h2p.dtype), pltpu.SemaphoreType.DMA,
                            pltpu.SemaphoreType.DMA]),
        out_shape=jax.ShapeDtypeStruct((n_rows, dw), h2p.dtype),
        compiler_params=_params("arbitrary"),
        name="dispatch",
    )(dest_flat, block_valid, h2p)


def _experts_body(be_ref, ns_ref, rev_ref, nu_ref, x_ref, wgu_ref, wd_ref, bgu_ref, bd_ref, y_ref, xb_ref):
    j = pl.program_id(0)
    f = pl.program_id(1)
    nsub = ns_ref[j]

    @pl.when(f == 0)
    def _():
        y_ref[...] = jnp.broadcast_to(bd_ref[...], y_ref.shape)

    @pl.when((f == 0) & (nsub > 0))
    def _():
        xb_ref[...] = x_ref[...].astype(BF16)

    def compute(r0, r1):
        gu = jnp.dot(xb_ref[r0:r1, :], wgu_ref[...].astype(BF16), preferred_element_type=F32) + bgu_ref[...]
        gate = jnp.minimum(gu, SWIGLU_LIMIT)
        glu = gate * jax.nn.sigmoid(SWIGLU_ALPHA * gate)
        up1 = jnp.clip(gu, -SWIGLU_LIMIT, SWIGLU_LIMIT) + 1.0
        n2 = gu.shape[1]
        rows2 = lax.broadcasted_iota(I32, (2 * LANES, LANES), 0)
        cols2 = lax.broadcasted_iota(I32, (2 * LANES, LANES), 1)
        sel = jnp.where(rows2 == 2 * cols2, 1.0, 0.0).astype(BF16)
        parts = []
        for c in range(n2 // LANES):
            sl = slice(c * LANES, (c + 1) * LANES)
            parts.append((glu[:, sl] * pltpu.roll(up1[:, sl], LANES - 1, 1)).astype(BF16))
        acts = [jnp.dot(jnp.concatenate(parts[2 * c:2 * c + 2], axis=1), sel, preferred_element_type=F32)
                for c in range(n2 // (2 * LANES))]
        act = jnp.concatenate(acts, axis=1).astype(BF16)
        y_ref[r0:r1, :] += jnp.dot(act, wd_ref[...].astype(BF16), preferred_element_type=F32)

    for ns in range(1, MOE_BLOCK // MOE_SUB + 1):
        @pl.when(nsub == ns)
        def _():
            for r0 in range(0, ns * MOE_SUB, MOE_ROWS):
                compute(r0, min(r0 + MOE_ROWS, ns * MOE_SUB))


def _experts(block_e, block_nsub, block_rev, n_used, x_rows, w_gate_up, w_down, b_gate_up, b_down):
    p, d = x_rows.shape
    dw = d
    dff = w_down.shape[1]
    nb = p // MOE_BLOCK
    nf = dff // FF_TILE

    def blk(j, f, be, ns, rev, nu):
        return jnp.minimum(j, nu[0] - 1)

    def fidx(j, f, be, ns, rev, nu):
        back = rev[jnp.minimum(j, nu[0] - 1)] == 1
        return jnp.where(j < nu[0], jnp.where(back, nf - 1 - f, f), jnp.where(back, 0, nf - 1))

    return pl.pallas_call(
        _experts_body,
        grid_spec=pltpu.PrefetchScalarGridSpec(
            num_scalar_prefetch=4,
            grid=(nb, nf),
            in_specs=[pl.BlockSpec((MOE_BLOCK, dw), lambda j, f, *s: (blk(j, f, *s), 0)),
                      pl.BlockSpec((None, d, 2 * FF_TILE), lambda j, f, *s: (s[0][j], 0, fidx(j, f, *s))),
                      pl.BlockSpec((None, FF_TILE, d), lambda j, f, *s: (s[0][j], fidx(j, f, *s), 0)),
                      pl.BlockSpec((None, 1, 2 * FF_TILE), lambda j, f, *s: (s[0][j], 0, fidx(j, f, *s))),
                      pl.BlockSpec((None, 1, d), lambda j, f, *s: (s[0][j], 0, 0))],
            out_specs=pl.BlockSpec((MOE_BLOCK, d), lambda j, f, *s: (j, 0)),
            scratch_shapes=[pltpu.VMEM((MOE_BLOCK, d), BF16)]),
        out_shape=jax.ShapeDtypeStruct((p, d), F32),
        compiler_params=_params("arbitrary", "arbitrary"),
        name="experts",
    )(block_e, block_nsub, block_rev, n_used, x_rows, w_gate_up, w_down, b_gate_up, b_down)


def _combine_body(tiles_per_batch, dest_ref, y_ref, gate_ref, x1_ref, ada_ref, g_ref, o_ref, buf, sem):
    i = pl.program_id(0)
    n_tiles = pl.num_programs(0)
    tm = x1_ref.shape[0]

    def row_copy(tile, slot, r, kk):
        src = dest_ref[(tile * tm + r) * TOP_K + kk]
        return pltpu.make_async_copy(y_ref.at[pl.ds(src, 1)], buf.at[slot, kk, pl.ds(r, 1)], sem.at[slot])

    def issue(tile, slot):
        def body(rb, carry):
            for jj in range(DMA_UNROLL):
                for kk in range(TOP_K):
                    row_copy(tile, slot, rb * DMA_UNROLL + jj, kk).start()
            return carry
        lax.fori_loop(0, tm // DMA_UNROLL, body, 0)

    def drain(tile, slot):
        for kk in range(TOP_K):
            pltpu.make_async_copy(y_ref.at[pl.ds(0, tm)], buf.at[slot, kk], sem.at[slot]).wait()

    slot = i % 2

    @pl.when(i == 0)
    def _():
        issue(0, 0)

    drain(i, slot)
    nxt = jnp.minimum(i + 1, n_tiles - 1)
    for r in range(tm):
        for kk in range(TOP_K):
            row_copy(nxt, 1 - slot, r, kk).start()
    gates = gate_ref[...]
    y = jnp.zeros(x1_ref.shape, F32)
    for kk in range(TOP_K):
        y = y + gates[:, kk:kk + 1] * buf[slot, kk]
    o_ref[...] = x1_ref[...] + ada_ref[5:6, :] * _rms(y, g_ref[...])

    @pl.when(i == n_tiles - 1)
    def _():
        drain(nxt, 1 - slot)


def _combine(dest_flat, y_rows, gates, x1, ada3, g_post_ffn, seq):
    n, d = x1.shape
    tm = min(GATHER_TILE, seq)
    tiles_per_batch = seq // tm
    return pl.pallas_call(
        functools.partial(_combine_body, tiles_per_batch),
        grid_spec=pltpu.PrefetchScalarGridSpec(
            num_scalar_prefetch=1,
            grid=(n // tm,),
            in_specs=[pl.BlockSpec(memory_space=pl.ANY),
                      pl.BlockSpec((tm, LANES), lambda i, dst: (i, 0)),
                      pl.BlockSpec((tm, d), lambda i, dst: (i, 0)),
                      pl.BlockSpec((None, 6, d), lambda i, dst: (i // tiles_per_batch, 0, 0)),
                      pl.BlockSpec((1, d), lambda i, dst: (0, 0))],
            out_specs=pl.BlockSpec((tm, d), lambda i, dst: (i, 0)),
            scratch_shapes=[pltpu.VMEM((2, TOP_K, tm, d), F32), pltpu.SemaphoreType.DMA((2,))]),
        out_shape=jax.ShapeDtypeStruct((n, d), F32),
        compiler_params=_params("arbitrary"),
        name="combine",
    )(dest_flat, y_rows, gates, x1, ada3, g_post_ffn)


def _tail_w_in(w_in):
    d = w_in.shape[0]
    o_wi = C_KI + IDX_DIM
    ki = w_in[:, C_KI:o_wi]
    wi = w_in[:, o_wi:o_wi + IDX_HEADS]
    pad = jnp.zeros((d, LANES - IDX_HEADS), w_in.dtype)
    return jnp.concatenate([ki, ki, wi, pad], axis=1)


def _layer(x, c, positions, w_ada, b_ada, g_pre_mix, g_post_mix, w_in, conv_w, g_conv_out, g_attn_out, w_out,
           g_pre_ffn, g_post_ffn, w_router, b_router, w_gate_up, b_gate_up, w_down, b_down):
    bsz, seq, d = x.shape
    n = bsz * seq
    x2 = x.reshape(n, d)
    ada3 = _ada(c, w_ada, b_ada).reshape(bsz, 6, d)
    w_in_b = w_in.astype(BF16)

    yc, q, k, v, qi, ki, wi = _in_proj(
        x2, ada3, g_pre_mix.reshape(1, d), positions.reshape(n, 1).astype(I32), w_in_b, _tail_w_in(w_in_b),
        conv_w.reshape(CONV_WIDTH, D_CONV), g_conv_out.reshape(1, D_CONV), seq)
    b3 = lambda a: a.reshape(bsz, seq, a.shape[-1])
    ya = _dsa(b3(q), b3(qi), b3(wi), b3(k), b3(v), b3(ki), g_attn_out.reshape(1, D_ATT)).reshape(n, D_ATT)

    wr = jnp.zeros((d, LANES), F32).at[:, :N_EXPERTS].set(w_router)
    br = jnp.zeros((1, LANES), F32).at[0, :N_EXPERTS].set(b_router)
    x1, h2, top_idx, gates = _out_proj(yc, ya, x2, ada3, w_out.astype(BF16), g_post_mix.reshape(1, d),
                                       g_pre_ffn.reshape(1, d), wr, br, seq)

    rank, cnt = _rank(top_idx)
    counts = cnt[0, :N_EXPERTS]
    padded = (counts + MOE_BLOCK - 1) // MOE_BLOCK * MOE_BLOCK
    pad_ends = jnp.cumsum(padded)
    pad_starts = pad_ends - padded
    nb = -(-(n * TOP_K) // MOE_BLOCK) + N_EXPERTS
    starts_row = jnp.zeros((1, LANES), F32).at[0, :N_EXPERTS].set(pad_starts.astype(F32))
    dest = _dest(top_idx, rank, starts_row)[:, :TOP_K].reshape(n * TOP_K)
    block_start = jnp.arange(nb, dtype=I32) * MOE_BLOCK
    block_e = jnp.minimum(jnp.sum((pad_ends[None, :] <= block_start[:, None]).astype(I32), axis=1),
                          N_EXPERTS - 1)
    n_used = (pad_ends[-1:] // MOE_BLOCK).astype(I32)
    block_valid = jnp.clip(counts[block_e] - (block_start - pad_starts[block_e]), 0, MOE_BLOCK)
    block_valid = jnp.where(block_start < pad_ends[-1], block_valid, 0).astype(I32)
    block_nsub = (block_valid + MOE_SUB - 1) // MOE_SUB
    block_e = jnp.where(block_start < pad_ends[-1], block_e, block_e[n_used[0] - 1])
    block_rev = ((block_start - pad_starts[block_e]) // MOE_BLOCK) % 2

    x_rows = _dispatch(dest, block_valid, h2, nb * MOE_BLOCK)
    y_rows = _experts(block_e, block_nsub, block_rev.astype(I32), n_used, x_rows, w_gate_up, w_down,
                      b_gate_up[:, None, :], b_down[:, None, :])
    out = _combine(dest, y_rows, gates, x1, ada3, g_post_ffn.reshape(1, d), seq)
    return out.reshape(bsz, seq, d)


def kernel(x, c, positions, w_ada, b_ada, g_pre_mix, g_post_mix, w_in, conv_w, g_conv_out, g_attn_out, w_out,
           g_pre_ffn, g_post_ffn, w_router, b_router, w_gate_up, b_gate_up, w_down, b_down):
    for l in range(w_ada.shape[0]):
        x = _layer(x, c, positions, w_ada[l], b_ada[l], g_pre_mix[l], g_post_mix[l], w_in[l], conv_w[l],
                   g_conv_out[l], g_attn_out[l], w_out[l], g_pre_ffn[l], g_post_ffn[l], w_router[l], b_router[l],
                   w_gate_up[l], b_gate_up[l], w_down[l], b_down[l])
    return x
```

```python
import functools

import numpy as np
import jax
import jax.numpy as jnp
from jax import lax
from jax.experimental import pallas as pl
from jax.experimental.pallas import tpu as pltpu

F32 = jnp.float32
BF16 = jnp.bfloat16
I32 = jnp.int32

EPS = 1e-6
LANES = 128
D_CONV = 1024
CONV_WIDTH = 3
N_HEADS = 8
N_KV_HEADS = 2
HEAD_DIM = 128
KV_REP = N_HEADS // N_KV_HEADS
ROPE_DIM = HEAD_DIM // 4
ROPE_THETA = 500000.0
IDX_HEADS = 16
IDX_DIM = 64
IDX_ROPE_DIM = IDX_DIM // 4
TOPK_MAX = 256
N_EXPERTS = 32
TOP_K = 4
SWIGLU_LIMIT = 7.0
SWIGLU_ALPHA = 1.702

D_ATT = N_HEADS * HEAD_DIM
D_KV = N_KV_HEADS * HEAD_DIM
D_QI = IDX_HEADS * IDX_DIM

INT_MIN = -2147483648
NEG_BIG = -1e30

ROW_TILE = 512
ATT_TILE = 512
MOE_BLOCK = 512
MOE_SUB = 256
MOE_ROWS = 512
FF_TILE = 512
GATHER_TILE = 128
DMA_UNROLL = 8
VMEM_LIMIT = 56 * 1024 * 1024


def _params(*sem):
    return pltpu.CompilerParams(dimension_semantics=sem, vmem_limit_bytes=VMEM_LIMIT)


def _rms(x, g):
    return x * lax.rsqrt(jnp.mean(x * x, axis=-1, keepdims=True) + EPS) * g


def _ada_body(c_ref, w_ref, b_ref, o_ref):
    c = c_ref[...]
    s = (c * jax.nn.sigmoid(c)).astype(BF16)
    o_ref[...] = jnp.dot(s, w_ref[...].astype(BF16), preferred_element_type=F32) + b_ref[...]


def _ada(c, w, b):
    bsz, d = c.shape
    n = w.shape[1]
    tn = 1536
    rows = 8
    cp = jnp.zeros((rows, d), F32).at[:bsz].set(c)
    out = pl.pallas_call(
        _ada_body,
        grid=(n // tn,),
        in_specs=[pl.BlockSpec((rows, d), lambda j: (0, 0)),
                  pl.BlockSpec((d, tn), lambda j: (0, j)),
                  pl.BlockSpec((1, tn), lambda j: (0, j))],
        out_specs=pl.BlockSpec((rows, tn), lambda j: (0, j)),
        out_shape=jax.ShapeDtypeStruct((rows, n), F32),
        compiler_params=_params("arbitrary"),
        name="ada",
    )(cp, w, b.reshape(1, n))
    return out[:bsz]


C_BCU = 0
C_Q = 3 * D_CONV
C_K = C_Q + D_ATT
C_V = C_K + D_KV
C_QI = C_V + D_KV
C_KI = C_QI + D_QI


def _rope_tables(pos, inv_freq, width, rot):
    half = rot // 2
    rows = pos.shape[0]
    lane = lax.broadcasted_iota(I32, (rows, LANES), 1) & (width - 1)
    ang = pos * inv_freq
    cos = jnp.cos(ang)
    sin = jnp.sin(ang)
    c = jnp.where(lane < rot, cos, 1.0)
    a = jnp.where((lane >= half) & (lane < rot), sin, 0.0)
    b = jnp.where(lane < half, -sin, 0.0)
    return c, a, b, half


def _rope(x, tabs):
    c, a, b, half = tabs
    return x * c + pltpu.roll(x, half, 1) * a + pltpu.roll(x, LANES - half, 1) * b


def _in_proj_body(tiles_per_batch, x_ref, ada_ref, g_ref, pos_ref, w_ref, wt_ref, cw_ref, gc_ref, fq_ref, fi_ref,
                  yc_ref, q_ref, k_ref, v_ref, qi_ref, ki_ref, wi_ref, vbuf):
    i = pl.program_id(0)
    tm = x_ref.shape[0]
    h = _rms(x_ref[...], g_ref[...]) * (1.0 + ada_ref[1:2, :]) + ada_ref[0:1, :]
    hb = h.astype(BF16)

    def proj(lo, hi):
        return jnp.dot(hb, w_ref[:, lo:hi], preferred_element_type=F32)

    @pl.when(i % tiles_per_batch == 0)
    def _():
        vbuf[0:8, :] = jnp.zeros((8, D_CONV), F32)

    bcu = proj(C_BCU, C_Q)
    v0 = bcu[:, D_CONV:2 * D_CONV] * bcu[:, 2 * D_CONV:]
    vbuf[8:8 + tm, :] = v0
    v1 = vbuf[7:7 + tm, :]
    v2 = vbuf[6:6 + tm, :]
    y = cw_ref[0:1, :] * v2 + cw_ref[1:2, :] * v1 + cw_ref[2:3, :] * v0
    yc_ref[...] = _rms(bcu[:, :D_CONV] * y, gc_ref[...]).astype(BF16)
    vbuf[0:8, :] = vbuf[tm:tm + 8, :]

    pos = pos_ref[...].astype(F32)
    tq = _rope_tables(pos, fq_ref[...], HEAD_DIM, ROPE_DIM)
    ti = _rope_tables(pos, fi_ref[...], IDX_DIM, IDX_ROPE_DIM)
    scale = HEAD_DIM ** -0.5

    qf = proj(C_Q, C_K)
    for hh in range(N_HEADS):
        sl = slice(hh * LANES, (hh + 1) * LANES)
        q_ref[:, sl] = (_rope(qf[:, sl], tq) * scale).astype(BF16)
    kf = proj(C_K, C_V)
    for hh in range(N_KV_HEADS):
        sl = slice(hh * LANES, (hh + 1) * LANES)
        k_ref[:, sl] = _rope(kf[:, sl], tq).astype(BF16)
    v_ref[...] = proj(C_V, C_QI).astype(BF16)
    qif = proj(C_QI, C_KI)
    for hh in range(D_QI // LANES):
        sl = slice(hh * LANES, (hh + 1) * LANES)
        qi_ref[:, sl] = _rope(qif[:, sl], ti).astype(BF16)
    tail = jnp.dot(hb, wt_ref[...], preferred_element_type=F32)
    ki_ref[...] = _rope(tail[:, :LANES], ti).astype(BF16)
    wi_ref[...] = tail[:, LANES:]


def _in_proj(x2, ada3, g_pre, pos2, w_main, w_tail, conv_w, g_conv, seq):
    n, d = x2.shape
    tm = min(ROW_TILE, seq)
    tiles_per_batch = seq // tm
    half_q = ROPE_DIM // 2
    half_i = IDX_ROPE_DIM // 2
    lane = np.arange(LANES)
    fq = jnp.asarray(ROPE_THETA, F32) ** (-jnp.asarray(lane % half_q, F32) / half_q)
    fi = jnp.asarray(ROPE_THETA, F32) ** (-jnp.asarray(lane % half_i, F32) / half_i)
    row = lambda i: (i, 0)
    fixed = lambda i: (0, 0)
    outs = [(D_CONV, BF16), (D_ATT, BF16), (D_KV, BF16), (D_KV, BF16), (D_QI, BF16), (LANES, BF16), (LANES, F32)]
    return pl.pallas_call(
        functools.partial(_in_proj_body, tiles_per_batch),
        grid=(n // tm,),
        in_specs=[pl.BlockSpec((tm, d), row),
                  pl.BlockSpec((None, 6, d), lambda i: (i // tiles_per_batch, 0, 0)),
                  pl.BlockSpec((1, d), fixed),
                  pl.BlockSpec((tm, 1), row),
                  pl.BlockSpec((d, C_KI), fixed, pipeline_mode=pl.Buffered(1)),
                  pl.BlockSpec((d, 2 * LANES), fixed),
                  pl.BlockSpec((CONV_WIDTH, D_CONV), fixed),
                  pl.BlockSpec((1, D_CONV), fixed),
                  pl.BlockSpec((1, LANES), fixed),
                  pl.BlockSpec((1, LANES), fixed)],
        out_specs=[pl.BlockSpec((tm, w), row) for w, _ in outs],
        out_shape=[jax.ShapeDtypeStruct((n, w), dt) for w, dt in outs],
        scratch_shapes=[pltpu.VMEM((tm + 8, D_CONV), F32)],
        compiler_params=_params("arbitrary"),
        name="in_proj",
    )(x2, ada3, g_pre, pos2, w_main, w_tail, conv_w, g_conv, fq.reshape(1, LANES), fi.reshape(1, LANES))


MAX_BISECT = 320

def _dsa_body(topk, nbits, q_ref, qi_ref, wi_ref, k_ref, v_ref, ki_ref, g_ref, o_ref,
              keys_ref, keys_t_ref, qm_ref, wb_ref, m_ref, l_ref, acc_ref):
    i = pl.program_id(1)
    tq = q_ref.shape[0]
    tk = tq
    nch = tk // LANES
    nkb = i + 1
    nt = (((1,), (1,)), ((), ()))

    lane = lax.broadcasted_iota(I32, (tq, LANES), 1)
    for p in range(IDX_HEADS // 2):
        qp = qi_ref[:, p * LANES:(p + 1) * LANES].astype(F32)
        qm_ref[(2 * p) * tq:(2 * p + 1) * tq, :] = jnp.where(lane < IDX_DIM, qp, 0.0).astype(BF16)
        qm_ref[(2 * p + 1) * tq:(2 * p + 2) * tq, :] = jnp.where(lane >= IDX_DIM, qp, 0.0).astype(BF16)
    w = wi_ref[...]
    for hh in range(IDX_HEADS):
        wb_ref[hh * tq:(hh + 1) * tq, :] = jnp.broadcast_to(w[:, hh:hh + 1], (tq, LANES))

    row = lax.broadcasted_iota(I32, (tq, tk), 0)
    col = lax.broadcasted_iota(I32, (tq, tk), 1)

    def score_block(kb, carry):
        k0 = pl.multiple_of(kb * tk, tk)
        kib = ki_ref[pl.ds(k0, tk), :]
        acc = jnp.zeros((tq, tk), F32)
        for j in range(IDX_HEADS // 4):
            d = lax.dot_general(qm_ref[4 * j * tq:4 * (j + 1) * tq, :], kib, nt, preferred_element_type=F32)
            wbj = wb_ref[4 * j * tq:4 * (j + 1) * tq, :]
            e = jnp.maximum(d, 0.0) * jnp.concatenate([wbj] * nch, axis=1)
            acc = acc + ((e[0:tq] + e[tq:2 * tq]) + (e[2 * tq:3 * tq] + e[3 * tq:4 * tq]))
        sc = jnp.where(k0 + col <= i * tq + row, acc, -jnp.inf)
        keys_ref[kb] = sc
        keys_t_ref[kb] = sc.T
        return carry

    lax.fori_loop(0, nkb, score_block, 0)

    def fold_t(elem_fn, merge, init):
        def body(kb, part):
            e = elem_fn(keys_t_ref[kb])
            for s0 in range(0, tk, 8):
                part = merge(part, e[s0:s0 + 8])
            return part
        return lax.fori_loop(0, nkb, body, jnp.full((8, tq), init, F32))

    def count_t(pred):
        return jnp.sum(fold_t(lambda s: jnp.where(pred(s), 1.0, 0.0), jnp.add, 0.0), axis=0, keepdims=True)

    kf = float(topk)
    lo0 = jnp.min(fold_t(lambda s: jnp.where(s > -jnp.inf, s, jnp.inf), jnp.minimum, jnp.inf),
                  axis=0, keepdims=True)
    hi0 = jnp.max(fold_t(lambda s: s, jnp.maximum, -jnp.inf), axis=0, keepdims=True)

    def bisect(carry):
        it, lo, hi, _ = carry
        mid = 0.5 * lo + 0.5 * hi
        keep = count_t(lambda s: s >= mid) >= kf
        moving = jnp.max(jnp.where((mid > lo) & (mid < hi), 1.0, 0.0))
        return it + 1, jnp.where(keep, mid, lo), jnp.where(keep, hi, mid), moving

    def unresolved(carry):
        return (carry[0] < MAX_BISECT) & (carry[3] > 0.0)

    _, thr_t, _, _ = lax.while_loop(unresolved, bisect, (jnp.int32(0), lo0, hi0, jnp.float32(1.0)))
    surplus = count_t(lambda s: s >= thr_t) - kf
    thr = jnp.broadcast_to(thr_t, (tq, tq)).T[:, :LANES]

    def count(pred):
        def body(kb, part):
            kk = keys_ref[kb]
            for c in range(nch):
                part = part + jnp.where(pred(kk[:, c * LANES:(c + 1) * LANES], kb * tk + c * LANES), 1.0, 0.0)
            return part
        part = lax.fori_loop(0, nkb, body, jnp.zeros((tq, LANES), F32))
        return jnp.broadcast_to(jnp.sum(part, axis=1, keepdims=True), (tq, LANES))

    zero = jnp.zeros((tq, LANES), I32)

    @pl.when(jnp.max(surplus) > 0.0)
    def _():
        need = kf - count(lambda kk, _: kk > thr)

        def index_bit(bi, p):
            cand = p + lax.shift_left(jnp.int32(1), nbits - 1 - bi)
            f = count(lambda kk, base: (kk == thr) & (base + lane < cand))
            return jnp.where(f <= need, cand, p)

        pend = lax.fori_loop(0, nbits, index_bit, zero)

        def drop(kb, carry):
            kk = keys_ref[kb]
            parts = []
            for c in range(nch):
                kc = kk[:, c * LANES:(c + 1) * LANES]
                parts.append(jnp.where((kc == thr) & (kb * tk + c * LANES + lane >= pend), -jnp.inf, kc))
            keys_ref[kb] = jnp.concatenate(parts, axis=1)
            return carry

        lax.fori_loop(0, nkb, drop, 0)

    tsel = thr

    m_ref[...] = jnp.full(m_ref.shape, NEG_BIG, F32)
    l_ref[...] = jnp.zeros(l_ref.shape, F32)
    acc_ref[...] = jnp.zeros(acc_ref.shape, F32)

    def attend(kb, carry):
        k0 = pl.multiple_of(kb * tk, tk)
        kk = keys_ref[kb]
        sel = jnp.concatenate([jnp.where(kk[:, c * LANES:(c + 1) * LANES] >= tsel, 0.0, NEG_BIG)
                               for c in range(nch)], axis=1)
        for g in range(N_KV_HEADS):
            kg = k_ref[pl.ds(k0, tk), g * LANES:(g + 1) * LANES]
            vg = v_ref[pl.ds(k0, tk), g * LANES:(g + 1) * LANES]
            qg = jnp.concatenate([q_ref[:, (KV_REP * g + r) * LANES:(KV_REP * g + r + 1) * LANES]
                                  for r in range(KV_REP)], axis=0)
            s = lax.dot_general(qg, kg, nt, preferred_element_type=F32)
            s = jnp.concatenate([s[r * tq:(r + 1) * tq] + sel for r in range(KV_REP)], axis=0)
            m_old = m_ref[g]
            m_new = jnp.maximum(m_old, jnp.max(s, axis=1, keepdims=True))
            alpha = jnp.exp(m_old - m_new)
            p = jnp.exp(s - jnp.concatenate([m_new] * nch, axis=1))
            l_ref[g] = alpha * l_ref[g] + jnp.sum(p, axis=1, keepdims=True)
            acc_ref[g] = alpha * acc_ref[g] + jnp.dot(p.astype(BF16), vg, preferred_element_type=F32)
            m_ref[g] = m_new
        return carry

    lax.fori_loop(0, nkb, attend, 0)

    heads = []
    for g in range(N_KV_HEADS):
        og = acc_ref[g] / l_ref[g]
        heads += [og[r * tq:(r + 1) * tq] for r in range(KV_REP)]
    o_ref[...] = _rms(jnp.concatenate(heads, axis=1), g_ref[...]).astype(BF16)


def _dsa(q, qi, wi, k, v, ki, g_attn):
    bsz, seq, _ = q.shape
    tq = min(ATT_TILE, seq)
    nq = seq // tq
    topk = min(TOPK_MAX, seq // 4)
    nbits = int(seq).bit_length()
    qblk = lambda w: pl.BlockSpec((None, tq, w), lambda b, i: (b, i, 0))
    full = lambda w: pl.BlockSpec((None, seq, w), lambda b, i: (b, 0, 0), pipeline_mode=pl.Buffered(1))
    return pl.pallas_call(
        functools.partial(_dsa_body, topk, nbits),
        grid=(bsz, nq),
        in_specs=[qblk(D_ATT), qblk(D_QI), qblk(LANES), full(D_KV), full(D_KV), full(LANES),
                  pl.BlockSpec((1, D_ATT), lambda b, i: (0, 0))],
        out_specs=qblk(D_ATT),
        out_shape=jax.ShapeDtypeStruct((bsz, seq, D_ATT), BF16),
        scratch_shapes=[pltpu.VMEM((nq, tq, tq), F32),
                        pltpu.VMEM((nq, tq, tq), F32),
                        pltpu.VMEM((IDX_HEADS * tq, LANES), BF16),
                        pltpu.VMEM((IDX_HEADS * tq, LANES), F32),
                        pltpu.VMEM((N_KV_HEADS, KV_REP * tq, LANES), F32),
                        pltpu.VMEM((N_KV_HEADS, KV_REP * tq, LANES), F32),
                        pltpu.VMEM((N_KV_HEADS, KV_REP * tq, LANES), F32)],
        compiler_params=_params("arbitrary", "arbitrary"),
        name="dsa",
    )(q, qi, wi, k, v, ki, g_attn)


def _split_bf16(x):
    hi = x.astype(BF16)
    return hi, (x - hi.astype(F32)).astype(BF16)


def _out_proj_body(yc_ref, ya_ref, x_ref, ada_ref, wo_ref, gpm_ref, gpf_ref, wr_ref, br_ref,
                   x1_ref, h2_ref, idx_ref, gate_ref):
    tm = x_ref.shape[0]
    mix = (jnp.dot(yc_ref[...], wo_ref[0:D_CONV, :], preferred_element_type=F32)
           + jnp.dot(ya_ref[...], wo_ref[D_CONV:, :], preferred_element_type=F32))
    x1 = x_ref[...] + ada_ref[2:3, :] * _rms(mix, gpm_ref[...])
    x1_ref[...] = x1
    h2 = _rms(x1, gpf_ref[...]) * (1.0 + ada_ref[4:5, :]) + ada_ref[3:4, :]
    h2_ref[...] = h2

    h_hi, h_lo = _split_bf16(h2)
    w_hi, w_lo = _split_bf16(wr_ref[...])
    logits = (jnp.dot(h_hi, w_hi, preferred_element_type=F32) + jnp.dot(h_hi, w_lo, preferred_element_type=F32)
              + jnp.dot(h_lo, w_hi, preferred_element_type=F32)) + br_ref[...]
    lane = lax.broadcasted_iota(I32, (tm, LANES), 1).astype(F32)
    cur = jnp.where(lane < N_EXPERTS, logits, -jnp.inf)
    vals, idxs = [], []
    for _ in range(TOP_K):
        m = jnp.max(cur, axis=1, keepdims=True)
        am = jnp.min(jnp.where(cur == m, lane, float(LANES)), axis=1, keepdims=True)
        vals.append(m)
        idxs.append(am)
        cur = jnp.where(lane == am, -jnp.inf, cur)
    es = [jnp.exp(vv - vals[0]) for vv in vals]
    tot = es[0] + es[1] + es[2] + es[3]
    idx_out = jnp.zeros((tm, LANES), I32)
    gate_out = jnp.zeros((tm, LANES), F32)
    for kk in range(TOP_K):
        idx_out = jnp.where(lane == kk, idxs[kk].astype(I32), idx_out)
        gate_out = jnp.where(lane == kk, es[kk] / tot, gate_out)
    idx_ref[...] = idx_out
    gate_ref[...] = gate_out


def _out_proj(yc, ya, x2, ada3, w_out, g_post_mix, g_pre_ffn, w_router, b_router, seq):
    n, d = x2.shape
    tm = min(ROW_TILE, seq)
    tiles_per_batch = seq // tm
    row = lambda i: (i, 0)
    fixed = lambda i: (0, 0)
    return pl.pallas_call(
        _out_proj_body,
        grid=(n // tm,),
        in_specs=[pl.BlockSpec((tm, D_CONV), row),
                  pl.BlockSpec((tm, D_ATT), row),
                  pl.BlockSpec((tm, d), row),
                  pl.BlockSpec((None, 6, d), lambda i: (i // tiles_per_batch, 0, 0)),
                  pl.BlockSpec((D_CONV + D_ATT, d), fixed, pipeline_mode=pl.Buffered(1)),
                  pl.BlockSpec((1, d), fixed),
                  pl.BlockSpec((1, d), fixed),
                  pl.BlockSpec((d, LANES), fixed),
                  pl.BlockSpec((1, LANES), fixed)],
        out_specs=[pl.BlockSpec((tm, d), row), pl.BlockSpec((tm, d), row),
                   pl.BlockSpec((tm, LANES), row), pl.BlockSpec((tm, LANES), row)],
        out_shape=[jax.ShapeDtypeStruct((n, d), F32), jax.ShapeDtypeStruct((n, d), F32),
                   jax.ShapeDtypeStruct((n, LANES), I32), jax.ShapeDtypeStruct((n, LANES), F32)],
        compiler_params=_params("arbitrary"),
        name="out_proj",
    )(yc, ya, x2, ada3, w_out, g_post_mix, g_pre_ffn, w_router, b_router)


def _rank_body(idx_ref, rank_ref, cnt_ref, carry):
    i = pl.program_id(0)
    tm = idx_ref.shape[0]

    @pl.when(i == 0)
    def _():
        carry[...] = jnp.zeros(carry.shape, F32)

    lane = lax.broadcasted_iota(I32, (tm, LANES), 1)
    r = lax.broadcasted_iota(I32, (tm, tm), 0)
    c = lax.broadcasted_iota(I32, (tm, tm), 1)
    before = jnp.where(c < r, 1.0, 0.0).astype(BF16)
    idx = idx_ref[...]
    base = carry[0:1, :]
    out = jnp.zeros((tm, LANES), F32)
    for kk in range(TOP_K):
        onehot = jnp.where(lane == idx[:, kk:kk + 1], 1.0, 0.0)
        prefix = jnp.dot(before, onehot.astype(BF16), preferred_element_type=F32) + base
        rk = jnp.sum(onehot * prefix, axis=1, keepdims=True)
        out = jnp.where(lane == kk, rk, out)
        base = base + jnp.sum(onehot, axis=0, keepdims=True)
    rank_ref[...] = out.astype(I32)
    carry[0:1, :] = base
    cnt_ref[...] = jnp.broadcast_to(base, cnt_ref.shape).astype(I32)


def _rank(top_idx):
    n = top_idx.shape[0]
    tm = min(ROW_TILE, n)
    return pl.pallas_call(
        _rank_body,
        grid=(n // tm,),
        in_specs=[pl.BlockSpec((tm, LANES), lambda i: (i, 0))],
        out_specs=[pl.BlockSpec((tm, LANES), lambda i: (i, 0)), pl.BlockSpec((8, LANES), lambda i: (0, 0))],
        out_shape=[jax.ShapeDtypeStruct((n, LANES), I32), jax.ShapeDtypeStruct((8, LANES), I32)],
        scratch_shapes=[pltpu.VMEM((8, LANES), F32)],
        compiler_params=_params("arbitrary"),
        name="rank",
    )(top_idx)


def _dest_body(idx_ref, rank_ref, start_ref, o_ref):
    tm = idx_ref.shape[0]
    lane = lax.broadcasted_iota(I32, (tm, LANES), 1)
    idx = idx_ref[...]
    rank = rank_ref[...]
    out = jnp.zeros((tm, LANES), I32)
    for kk in range(TOP_K):
        start = jnp.sum(jnp.where(lane == idx[:, kk:kk + 1], start_ref[...], 0.0), axis=1, keepdims=True)
        out = jnp.where(lane == kk, start.astype(I32) + rank[:, kk:kk + 1], out)
    o_ref[...] = out


def _dest(top_idx, rank, starts):
    n = top_idx.shape[0]
    tm = min(2 * ROW_TILE, n)
    row = lambda i: (i, 0)
    return pl.pallas_call(
        _dest_body,
        grid=(n // tm,),
        in_specs=[pl.BlockSpec((tm, LANES), row), pl.BlockSpec((tm, LANES), row),
                  pl.BlockSpec((1, LANES), lambda i: (0, 0))],
        out_specs=pl.BlockSpec((tm, LANES), row),
        out_shape=jax.ShapeDtypeStruct((n, LANES), I32),
        compiler_params=_params("arbitrary"),
        name="dest",
    )(top_idx, rank, starts)


def _dispatch_body(dest_ref, valid_ref, h_ref, o_ref, zeros_ref, sem, zsem):
    i = pl.program_id(0)
    tm = h_ref.shape[0]
    blk = zeros_ref.shape[0]
    nb = o_ref.shape[0] // blk

    @pl.when(i == 0)
    def _():
        zeros_ref[...] = jnp.zeros(zeros_ref.shape, zeros_ref.dtype)

        def for_blocks(fn):
            for jb in range(nb):
                @pl.when(valid_ref[jb] < blk)
                def _():
                    fn(pltpu.make_async_copy(zeros_ref, o_ref.at[pl.ds(jb * blk, blk)], zsem))

        for_blocks(lambda cp: cp.start())
        for_blocks(lambda cp: cp.wait())

    def row_copy(r, kk):
        dst = dest_ref[(i * tm + r) * TOP_K + kk]
        return pltpu.make_async_copy(h_ref.at[pl.ds(r, 1)], o_ref.at[pl.ds(dst, 1)], sem)

    def issue(rb, carry):
        for jj in range(DMA_UNROLL):
            for kk in range(TOP_K):
                row_copy(rb * DMA_UNROLL + jj, kk).start()
        return carry

    lax.fori_loop(0, tm // DMA_UNROLL, issue, 0)
    for kk in range(TOP_K):
        pltpu.make_async_copy(h_ref, o_ref.at[pl.ds(0, tm)], sem).wait()


def _dispatch(dest_flat, block_valid, h2p, n_rows):
    n, dw = h2p.shape
    tm = min(ROW_TILE, n)
    return pl.pallas_call(
        _dispatch_body,
        grid_spec=pltpu.PrefetchScalarGridSpec(
            num_scalar_prefetch=2,
            grid=(n // tm,),
            in_specs=[pl.BlockSpec((tm, dw), lambda i, dst, bv: (i, 0))],
            out_specs=pl.BlockSpec(memory_space=pl.ANY),
            scratch_shapes=[pltpu.VMEM((MOE_BLOCK, dw), h2p.dtype), pltpu.SemaphoreType.DMA,
                            pltpu.SemaphoreType.DMA]),
        out_shape=jax.ShapeDtypeStruct((n_rows, dw), h2p.dtype),
        compiler_params=_params("arbitrary"),
        name="dispatch",
    )(dest_flat, block_valid, h2p)


def _experts_body(be_ref, ns_ref, rev_ref, nu_ref, x_ref, wgu_ref, wd_ref, bgu_ref, bd_ref, y_ref, xb_ref):
    j = pl.program_id(0)
    f = pl.program_id(1)
    nsub = ns_ref[j]

    def compute(r0, r1, first):
        if first:
            xb = x_ref[r0:r1, :].astype(BF16)
            xb_ref[r0:r1, :] = xb
        else:
            xb = xb_ref[r0:r1, :]
        gu = jnp.dot(xb, wgu_ref[...].astype(BF16), preferred_element_type=F32) + bgu_ref[...]
        gate = jnp.minimum(gu, SWIGLU_LIMIT)
        glu = gate * jax.nn.sigmoid(SWIGLU_ALPHA * gate)
        up1 = jnp.clip(gu, -SWIGLU_LIMIT, SWIGLU_LIMIT) + 1.0
        n2 = gu.shape[1]
        rows2 = lax.broadcasted_iota(I32, (2 * LANES, LANES), 0)
        cols2 = lax.broadcasted_iota(I32, (2 * LANES, LANES), 1)
        sel = jnp.where(rows2 == 2 * cols2, 1.0, 0.0).astype(BF16)
        parts = []
        for c in range(n2 // LANES):
            sl = slice(c * LANES, (c + 1) * LANES)
            parts.append((glu[:, sl] * pltpu.roll(up1[:, sl], LANES - 1, 1)).astype(BF16))
        acts = [jnp.dot(jnp.concatenate(parts[2 * c:2 * c + 2], axis=1), sel, preferred_element_type=F32)
                for c in range(n2 // (2 * LANES))]
        act = jnp.concatenate(acts, axis=1).astype(BF16)
        down = jnp.dot(act, wd_ref[...].astype(BF16), preferred_element_type=F32)
        if first:
            y_ref[r0:r1, :] = down + bd_ref[...]
        else:
            y_ref[r0:r1, :] += down

    for ns in range(0, MOE_BLOCK // MOE_SUB + 1):
        @pl.when((nsub == ns) & (f == 0))
        def _():
            for r0 in range(0, ns * MOE_SUB, MOE_ROWS):
                compute(r0, min(r0 + MOE_ROWS, ns * MOE_SUB), True)
            if ns * MOE_SUB < MOE_BLOCK:
                y_ref[ns * MOE_SUB:, :] = jnp.broadcast_to(bd_ref[...], (MOE_BLOCK - ns * MOE_SUB, y_ref.shape[1]))

        if ns > 0:
            @pl.when((nsub == ns) & (f > 0))
            def _():
                for r0 in range(0, ns * MOE_SUB, MOE_ROWS):
                    compute(r0, min(r0 + MOE_ROWS, ns * MOE_SUB), False)


def _experts(block_e, block_nsub, block_rev, n_used, x_rows, w_gate_up, w_down, b_gate_up, b_down):
    p, d = x_rows.shape
    dw = d
    dff = w_down.shape[1]
    nb = p // MOE_BLOCK
    nf = dff // FF_TILE

    def blk(j, f, be, ns, rev, nu):
        return jnp.minimum(j, nu[0] - 1)

    def fidx(j, f, be, ns, rev, nu):
        back = rev[jnp.minimum(j, nu[0] - 1)] == 1
        return jnp.where(j < nu[0], jnp.where(back, nf - 1 - f, f), jnp.where(back, 0, nf - 1))

    return pl.pallas_call(
        _experts_body,
        grid_spec=pltpu.PrefetchScalarGridSpec(
            num_scalar_prefetch=4,
            grid=(nb, nf),
            in_specs=[pl.BlockSpec((MOE_BLOCK, dw), lambda j, f, *s: (blk(j, f, *s), 0)),
                      pl.BlockSpec((None, d, 2 * FF_TILE), lambda j, f, *s: (s[0][j], 0, fidx(j, f, *s))),
                      pl.BlockSpec((None, FF_TILE, d), lambda j, f, *s: (s[0][j], fidx(j, f, *s), 0)),
                      pl.BlockSpec((None, 1, 2 * FF_TILE), lambda j, f, *s: (s[0][j], 0, fidx(j, f, *s))),
                      pl.BlockSpec((None, 1, d), lambda j, f, *s: (s[0][j], 0, 0))],
            out_specs=pl.BlockSpec((MOE_BLOCK, d), lambda j, f, *s: (j, 0)),
            scratch_shapes=[pltpu.VMEM((MOE_BLOCK, d), BF16)]),
        out_shape=jax.ShapeDtypeStruct((p, d), F32),
        compiler_params=_params("arbitrary", "arbitrary"),
        name="experts",
    )(block_e, block_nsub, block_rev, n_used, x_rows, w_gate_up, w_down, b_gate_up, b_down)


def _combine_body(tiles_per_batch, dest_ref, y_ref, gate_ref, x1_ref, ada_ref, g_ref, o_ref, buf, sem):
    i = pl.program_id(0)
    n_tiles = pl.num_programs(0)
    tm = x1_ref.shape[0]

    def row_copy(tile, slot, r, kk):
        src = dest_ref[(tile * tm + r) * TOP_K + kk]
        return pltpu.make_async_copy(y_ref.at[pl.ds(src, 1)], buf.at[slot, kk, pl.ds(r, 1)], sem.at[slot])

    def issue(tile, slot):
        def body(rb, carry):
            for jj in range(DMA_UNROLL):
                for kk in range(TOP_K):
                    row_copy(tile, slot, rb * DMA_UNROLL + jj, kk).start()
            return carry
        lax.fori_loop(0, tm // DMA_UNROLL, body, 0)

    def drain(tile, slot):
        for kk in range(TOP_K):
            pltpu.make_async_copy(y_ref.at[pl.ds(0, tm)], buf.at[slot, kk], sem.at[slot]).wait()

    slot = i % 2

    @pl.when(i == 0)
    def _():
        issue(0, 0)

    drain(i, slot)
    nxt = jnp.minimum(i + 1, n_tiles - 1)
    for r in range(tm):
        for kk in range(TOP_K):
            row_copy(nxt, 1 - slot, r, kk).start()
    gates = gate_ref[...]
    y = jnp.zeros(x1_ref.shape, F32)
    for kk in range(TOP_K):
        y = y + gates[:, kk:kk + 1] * buf[slot, kk]
    o_ref[...] = x1_ref[...] + ada_ref[5:6, :] * _rms(y, g_ref[...])

    @pl.when(i == n_tiles - 1)
    def _():
        drain(nxt, 1 - slot)


def _combine(dest_flat, y_rows, gates, x1, ada3, g_post_ffn, seq):
    n, d = x1.shape
    tm = min(GATHER_TILE, seq)
    tiles_per_batch = seq // tm
    return pl.pallas_call(
        functools.partial(_combine_body, tiles_per_batch),
        grid_spec=pltpu.PrefetchScalarGridSpec(
            num_scalar_prefetch=1,
            grid=(n // tm,),
            in_specs=[pl.BlockSpec(memory_space=pl.ANY),
                      pl.BlockSpec((tm, LANES), lambda i, dst: (i, 0)),
                      pl.BlockSpec((tm, d), lambda i, dst: (i, 0)),
                      pl.BlockSpec((None, 6, d), lambda i, dst: (i // tiles_per_batch, 0, 0)),
                      pl.BlockSpec((1, d), lambda i, dst: (0, 0))],
            out_specs=pl.BlockSpec((tm, d), lambda i, dst: (i, 0)),
            scratch_shapes=[pltpu.VMEM((2, TOP_K, tm, d), F32), pltpu.SemaphoreType.DMA((2,))]),
        out_shape=jax.ShapeDtypeStruct((n, d), F32),
        compiler_params=_params("arbitrary"),
        name="combine",
    )(dest_flat, y_rows, gates, x1, ada3, g_post_ffn)


def _tail_w_in(w_in):
    d = w_in.shape[0]
    o_wi = C_KI + IDX_DIM
    ki = w_in[:, C_KI:o_wi]
    wi = w_in[:, o_wi:o_wi + IDX_HEADS]
    pad = jnp.zeros((d, LANES - IDX_HEADS), w_in.dtype)
    return jnp.concatenate([ki, ki, wi, pad], axis=1)


def _layer(x, c, positions, w_ada, b_ada, g_pre_mix, g_post_mix, w_in, conv_w, g_conv_out, g_attn_out, w_out,
           g_pre_ffn, g_post_ffn, w_router, b_router, w_gate_up, b_gate_up, w_down, b_down):
    bsz, seq, d = x.shape
    n = bsz * seq
    x2 = x.reshape(n, d)
    ada3 = _ada(c, w_ada, b_ada).reshape(bsz, 6, d)
    w_in_b = w_in.astype(BF16)

    yc, q, k, v, qi, ki, wi = _in_proj(
        x2, ada3, g_pre_mix.reshape(1, d), positions.reshape(n, 1).astype(I32), w_in_b, _tail_w_in(w_in_b),
        conv_w.reshape(CONV_WIDTH, D_CONV), g_conv_out.reshape(1, D_CONV), seq)
    b3 = lambda a: a.reshape(bsz, seq, a.shape[-1])
    ya = _dsa(b3(q), b3(qi), b3(wi), b3(k), b3(v), b3(ki), g_attn_out.reshape(1, D_ATT)).reshape(n, D_ATT)

    wr = jnp.zeros((d, LANES), F32).at[:, :N_EXPERTS].set(w_router)
    br = jnp.zeros((1, LANES), F32).at[0, :N_EXPERTS].set(b_router)
    x1, h2, top_idx, gates = _out_proj(yc, ya, x2, ada3, w_out.astype(BF16), g_post_mix.reshape(1, d),
                                       g_pre_ffn.reshape(1, d), wr, br, seq)

    rank, cnt = _rank(top_idx)
    counts = cnt[0, :N_EXPERTS]
    padded = (counts + MOE_BLOCK - 1) // MOE_BLOCK * MOE_BLOCK
    pad_ends = jnp.cumsum(padded)
    pad_starts = pad_ends - padded
    nb = -(-(n * TOP_K) // MOE_BLOCK) + N_EXPERTS
    starts_row = jnp.zeros((1, LANES), F32).at[0, :N_EXPERTS].set(pad_starts.astype(F32))
    dest = _dest(top_idx, rank, starts_row)[:, :TOP_K].reshape(n * TOP_K)
    block_start = jnp.arange(nb, dtype=I32) * MOE_BLOCK
    block_e = jnp.minimum(jnp.sum((pad_ends[None, :] <= block_start[:, None]).astype(I32), axis=1),
                          N_EXPERTS - 1)
    n_used = (pad_ends[-1:] // MOE_BLOCK).astype(I32)
    block_valid = jnp.clip(counts[block_e] - (block_start - pad_starts[block_e]), 0, MOE_BLOCK)
    block_valid = jnp.where(block_start < pad_ends[-1], block_valid, 0).astype(I32)
    block_nsub = (block_valid + MOE_SUB - 1) // MOE_SUB
    block_e = jnp.where(block_start < pad_ends[-1], block_e, block_e[n_used[0] - 1])
    block_rev = ((block_start - pad_starts[block_e]) // MOE_BLOCK) % 2

    x_rows = _dispatch(dest, block_valid, h2, nb * MOE_BLOCK)
    y_rows = _experts(block_e, block_nsub, block_rev.astype(I32), n_used, x_rows, w_gate_up, w_down,
                      b_gate_up[:, None, :], b_down[:, None, :])
    out = _combine(dest, y_rows, gates, x1, ada3, g_post_ffn.reshape(1, d), seq)
    return out.reshape(bsz, seq, d)


def kernel(x, c, positions, w_ada, b_ada, g_pre_mix, g_post_mix, w_in, conv_w, g_conv_out, g_attn_out, w_out,
           g_pre_ffn, g_post_ffn, w_router, b_router, w_gate_up, b_gate_up, w_down, b_down):
    for l in range(w_ada.shape[0]):
        x = _layer(x, c, positions, w_ada[l], b_ada[l], g_pre_mix[l], g_post_mix[l], w_in[l], conv_w[l],
                   g_conv_out[l], g_attn_out[l], w_out[l], g_pre_ffn[l], g_post_ffn[l], w_router[l], b_router[l],
                   w_gate_up[l], b_gate_up[l], w_down[l], b_down[l])
    return x
```

```python
import functools

import numpy as np
import jax
import jax.numpy as jnp
from jax import lax
from jax.experimental import pallas as pl
from jax.experimental.pallas import tpu as pltpu

F32 = jnp.float32
BF16 = jnp.bfloat16
I32 = jnp.int32

EPS = 1e-6
LANES = 128
D_CONV = 1024
CONV_WIDTH = 3
N_HEADS = 8
N_KV_HEADS = 2
HEAD_DIM = 128
KV_REP = N_HEADS // N_KV_HEADS
ROPE_DIM = HEAD_DIM // 4
ROPE_THETA = 500000.0
IDX_HEADS = 16
IDX_DIM = 64
IDX_ROPE_DIM = IDX_DIM // 4
TOPK_MAX = 256
N_EXPERTS = 32
TOP_K = 4
SWIGLU_LIMIT = 7.0
SWIGLU_ALPHA = 1.702

D_ATT = N_HEADS * HEAD_DIM
D_KV = N_KV_HEADS * HEAD_DIM
D_QI = IDX_HEADS * IDX_DIM

INT_MIN = -2147483648
NEG_BIG = -1e30

ROW_TILE = 512
ATT_TILE = 512
MOE_BLOCK = 512
MOE_SUB = 256
MOE_ROWS = 512
FF_TILE = 512
GATHER_TILE = 128
DMA_UNROLL = 8
VMEM_LIMIT = 56 * 1024 * 1024


def _params(*sem):
    return pltpu.CompilerParams(dimension_semantics=sem, vmem_limit_bytes=VMEM_LIMIT)


def _rms(x, g):
    return x * lax.rsqrt(jnp.mean(x * x, axis=-1, keepdims=True) + EPS) * g


def _ada_body(c_ref, w_ref, b_ref, o_ref):
    c = c_ref[...]
    s = (c * jax.nn.sigmoid(c)).astype(BF16)
    o_ref[...] = jnp.dot(s, w_ref[...].astype(BF16), preferred_element_type=F32) + b_ref[...]


def _ada(c, w, b):
    bsz, d = c.shape
    n = w.shape[1]
    tn = 1536
    rows = 8
    cp = jnp.zeros((rows, d), F32).at[:bsz].set(c)
    out = pl.pallas_call(
        _ada_body,
        grid=(n // tn,),
        in_specs=[pl.BlockSpec((rows, d), lambda j: (0, 0)),
                  pl.BlockSpec((d, tn), lambda j: (0, j)),
                  pl.BlockSpec((1, tn), lambda j: (0, j))],
        out_specs=pl.BlockSpec((rows, tn), lambda j: (0, j)),
        out_shape=jax.ShapeDtypeStruct((rows, n), F32),
        compiler_params=_params("arbitrary"),
        name="ada",
    )(cp, w, b.reshape(1, n))
    return out[:bsz]


C_BCU = 0
C_Q = 3 * D_CONV
C_K = C_Q + D_ATT
C_V = C_K + D_KV
C_QI = C_V + D_KV
C_KI = C_QI + D_QI


def _rope_tables(pos, inv_freq, width, rot):
    half = rot // 2
    rows = pos.shape[0]
    lane = lax.broadcasted_iota(I32, (rows, LANES), 1) & (width - 1)
    ang = pos * inv_freq
    cos = jnp.cos(ang)
    sin = jnp.sin(ang)
    c = jnp.where(lane < rot, cos, 1.0)
    a = jnp.where((lane >= half) & (lane < rot), sin, 0.0)
    b = jnp.where(lane < half, -sin, 0.0)
    return c, a, b, half


def _rope(x, tabs):
    c, a, b, half = tabs
    return x * c + pltpu.roll(x, half, 1) * a + pltpu.roll(x, LANES - half, 1) * b


def _in_proj_body(tiles_per_batch, x_ref, ada_ref, g_ref, pos_ref, w_ref, wt_ref, cw_ref, gc_ref, fq_ref, fi_ref,
                  yc_ref, q_ref, k_ref, v_ref, qi_ref, ki_ref, wi_ref, vbuf):
    i = pl.program_id(0)
    tm = x_ref.shape[0]
    h = _rms(x_ref[...], g_ref[...]) * (1.0 + ada_ref[1:2, :]) + ada_ref[0:1, :]
    hb = h.astype(BF16)

    def proj(lo, hi):
        return jnp.dot(hb, w_ref[:, lo:hi], preferred_element_type=F32)

    @pl.when(i % tiles_per_batch == 0)
    def _():
        vbuf[0:8, :] = jnp.zeros((8, D_CONV), F32)

    bcu = proj(C_BCU, C_Q)
    v0 = bcu[:, D_CONV:2 * D_CONV] * bcu[:, 2 * D_CONV:]
    vbuf[8:8 + tm, :] = v0
    v1 = vbuf[7:7 + tm, :]
    v2 = vbuf[6:6 + tm, :]
    y = cw_ref[0:1, :] * v2 + cw_ref[1:2, :] * v1 + cw_ref[2:3, :] * v0
    yc_ref[...] = _rms(bcu[:, :D_CONV] * y, gc_ref[...]).astype(BF16)
    vbuf[0:8, :] = vbuf[tm:tm + 8, :]

    pos = pos_ref[...].astype(F32)
    tq = _rope_tables(pos, fq_ref[...], HEAD_DIM, ROPE_DIM)
    ti = _rope_tables(pos, fi_ref[...], IDX_DIM, IDX_ROPE_DIM)
    scale = HEAD_DIM ** -0.5

    qf = proj(C_Q, C_K)
    for hh in range(N_HEADS):
        sl = slice(hh * LANES, (hh + 1) * LANES)
        q_ref[:, sl] = (_rope(qf[:, sl], tq) * scale).astype(BF16)
    kf = proj(C_K, C_V)
    for hh in range(N_KV_HEADS):
        sl = slice(hh * LANES, (hh + 1) * LANES)
        k_ref[:, sl] = _rope(kf[:, sl], tq).astype(BF16)
    v_ref[...] = proj(C_V, C_QI).astype(BF16)
    qif = proj(C_QI, C_KI)
    for hh in range(D_QI // LANES):
        sl = slice(hh * LANES, (hh + 1) * LANES)
        qi_ref[:, sl] = _rope(qif[:, sl], ti).astype(BF16)
    tail = jnp.dot(hb, wt_ref[...], preferred_element_type=F32)
    ki_ref[...] = _rope(tail[:, :LANES], ti).astype(BF16)
    wi_ref[...] = tail[:, LANES:]


def _in_proj(x2, ada3, g_pre, pos2, w_main, w_tail, conv_w, g_conv, seq):
    n, d = x2.shape
    tm = min(ROW_TILE, seq)
    tiles_per_batch = seq // tm
    half_q = ROPE_DIM // 2
    half_i = IDX_ROPE_DIM // 2
    lane = np.arange(LANES)
    fq = jnp.asarray(ROPE_THETA, F32) ** (-jnp.asarray(lane % half_q, F32) / half_q)
    fi = jnp.asarray(ROPE_THETA, F32) ** (-jnp.asarray(lane % half_i, F32) / half_i)
    row = lambda i: (i, 0)
    fixed = lambda i: (0, 0)
    outs = [(D_CONV, BF16), (D_ATT, BF16), (D_KV, BF16), (D_KV, BF16), (D_QI, BF16), (LANES, BF16), (LANES, F32)]
    return pl.pallas_call(
        functools.partial(_in_proj_body, tiles_per_batch),
        grid=(n // tm,),
        in_specs=[pl.BlockSpec((tm, d), row),
                  pl.BlockSpec((None, 6, d), lambda i: (i // tiles_per_batch, 0, 0)),
                  pl.BlockSpec((1, d), fixed),
                  pl.BlockSpec((tm, 1), row),
                  pl.BlockSpec((d, C_KI), fixed, pipeline_mode=pl.Buffered(1)),
                  pl.BlockSpec((d, 2 * LANES), fixed),
                  pl.BlockSpec((CONV_WIDTH, D_CONV), fixed),
                  pl.BlockSpec((1, D_CONV), fixed),
                  pl.BlockSpec((1, LANES), fixed),
                  pl.BlockSpec((1, LANES), fixed)],
        out_specs=[pl.BlockSpec((tm, w), row) for w, _ in outs],
        out_shape=[jax.ShapeDtypeStruct((n, w), dt) for w, dt in outs],
        scratch_shapes=[pltpu.VMEM((tm + 8, D_CONV), F32)],
        compiler_params=_params("arbitrary"),
        name="in_proj",
    )(x2, ada3, g_pre, pos2, w_main, w_tail, conv_w, g_conv, fq.reshape(1, LANES), fi.reshape(1, LANES))


MAX_BISECT = 320

def _dsa_body(topk, nbits, q_ref, qi_ref, wi_ref, k_ref, v_ref, ki_ref, g_ref, o_ref,
              keys_ref, keys_t_ref, qm_ref, wb_ref, m_ref, l_ref, acc_ref):
    i = pl.program_id(1)
    tq = q_ref.shape[0]
    tk = tq
    nch = tk // LANES
    nkb = i + 1
    nt = (((1,), (1,)), ((), ()))

    lane = lax.broadcasted_iota(I32, (tq, LANES), 1)
    for p in range(IDX_HEADS // 2):
        qp = qi_ref[:, p * LANES:(p + 1) * LANES].astype(F32)
        qm_ref[(2 * p) * tq:(2 * p + 1) * tq, :] = jnp.where(lane < IDX_DIM, qp, 0.0).astype(BF16)
        qm_ref[(2 * p + 1) * tq:(2 * p + 2) * tq, :] = jnp.where(lane >= IDX_DIM, qp, 0.0).astype(BF16)
    w = wi_ref[...]
    for hh in range(IDX_HEADS):
        wb_ref[hh * tq:(hh + 1) * tq, :] = jnp.broadcast_to(w[:, hh:hh + 1], (tq, LANES))

    row = lax.broadcasted_iota(I32, (tq, tk), 0)
    col = lax.broadcasted_iota(I32, (tq, tk), 1)

    def score_block(kb, carry):
        k0 = pl.multiple_of(kb * tk, tk)
        kib = ki_ref[pl.ds(k0, tk), :]
        acc = jnp.zeros((tq, tk), F32)
        for j in range(IDX_HEADS // 4):
            d = lax.dot_general(qm_ref[4 * j * tq:4 * (j + 1) * tq, :], kib, nt, preferred_element_type=F32)
            wbj = wb_ref[4 * j * tq:4 * (j + 1) * tq, :]
            e = jnp.maximum(d, 0.0) * jnp.concatenate([wbj] * nch, axis=1)
            acc = acc + ((e[0:tq] + e[tq:2 * tq]) + (e[2 * tq:3 * tq] + e[3 * tq:4 * tq]))
        sc = jnp.where(k0 + col <= i * tq + row, acc, -jnp.inf)
        keys_ref[kb] = sc
        keys_t_ref[kb] = sc.T
        return carry

    lax.fori_loop(0, nkb, score_block, 0)

    def fold_t(elem_fn, merge, init):
        def body(kb, part):
            e = elem_fn(keys_t_ref[kb])
            for s0 in range(0, tk, 8):
                part = merge(part, e[s0:s0 + 8])
            return part
        return lax.fori_loop(0, nkb, body, jnp.full((8, tq), init, F32))

    def count_t(pred):
        return jnp.sum(fold_t(lambda s: jnp.where(pred(s), 1.0, 0.0), jnp.add, 0.0), axis=0, keepdims=True)

    kf = float(topk)
    lo0 = jnp.min(fold_t(lambda s: jnp.where(s > -jnp.inf, s, jnp.inf), jnp.minimum, jnp.inf),
                  axis=0, keepdims=True)
    hi0 = jnp.max(fold_t(lambda s: s, jnp.maximum, -jnp.inf), axis=0, keepdims=True)

    def bisect(carry):
        it, lo, hi, _ = carry
        mid = 0.5 * lo + 0.5 * hi
        keep = count_t(lambda s: s >= mid) >= kf
        moving = jnp.max(jnp.where((mid > lo) & (mid < hi), 1.0, 0.0))
        return it + 1, jnp.where(keep, mid, lo), jnp.where(keep, hi, mid), moving

    def unresolved(carry):
        return (carry[0] < MAX_BISECT) & (carry[3] > 0.0)

    _, thr_t, _, _ = lax.while_loop(unresolved, bisect, (jnp.int32(0), lo0, hi0, jnp.float32(1.0)))
    surplus = count_t(lambda s: s >= thr_t) - kf
    thr = jnp.broadcast_to(thr_t, (tq, tq)).T[:, :LANES]

    def count(pred):
        def body(kb, part):
            kk = keys_ref[kb]
            for c in range(nch):
                part = part + jnp.where(pred(kk[:, c * LANES:(c + 1) * LANES], kb * tk + c * LANES), 1.0, 0.0)
            return part
        part = lax.fori_loop(0, nkb, body, jnp.zeros((tq, LANES), F32))
        return jnp.broadcast_to(jnp.sum(part, axis=1, keepdims=True), (tq, LANES))

    zero = jnp.zeros((tq, LANES), I32)

    @pl.when(jnp.max(surplus) > 0.0)
    def _():
        need = kf - count(lambda kk, _: kk > thr)

        def index_bit(bi, p):
            cand = p + lax.shift_left(jnp.int32(1), nbits - 1 - bi)
            f = count(lambda kk, base: (kk == thr) & (base + lane < cand))
            return jnp.where(f <= need, cand, p)

        pend = lax.fori_loop(0, nbits, index_bit, zero)

        def drop(kb, carry):
            kk = keys_ref[kb]
            parts = []
            for c in range(nch):
                kc = kk[:, c * LANES:(c + 1) * LANES]
                parts.append(jnp.where((kc == thr) & (kb * tk + c * LANES + lane >= pend), -jnp.inf, kc))
            keys_ref[kb] = jnp.concatenate(parts, axis=1)
            return carry

        lax.fori_loop(0, nkb, drop, 0)

    tsel = thr

    m_ref[...] = jnp.full(m_ref.shape, NEG_BIG, F32)
    l_ref[...] = jnp.zeros(l_ref.shape, F32)
    acc_ref[...] = jnp.zeros(acc_ref.shape, F32)

    def attend(kb, carry):
        k0 = pl.multiple_of(kb * tk, tk)
        kk = keys_ref[kb]
        sel = jnp.concatenate([jnp.where(kk[:, c * LANES:(c + 1) * LANES] >= tsel, 0.0, NEG_BIG)
                               for c in range(nch)], axis=1)
        for g in range(N_KV_HEADS):
            kg = k_ref[pl.ds(k0, tk), g * LANES:(g + 1) * LANES]
            vg = v_ref[pl.ds(k0, tk), g * LANES:(g + 1) * LANES]
            qg = jnp.concatenate([q_ref[:, (KV_REP * g + r) * LANES:(KV_REP * g + r + 1) * LANES]
                                  for r in range(KV_REP)], axis=0)
            s = lax.dot_general(qg, kg, nt, preferred_element_type=F32)
            s = jnp.concatenate([s[r * tq:(r + 1) * tq] + sel for r in range(KV_REP)], axis=0)
            m_old = m_ref[g]
            m_new = jnp.maximum(m_old, jnp.max(s, axis=1, keepdims=True))
            alpha = jnp.exp(m_old - m_new)
            p = jnp.exp(s - jnp.concatenate([m_new] * nch, axis=1))
            l_ref[g] = alpha * l_ref[g] + jnp.sum(p, axis=1, keepdims=True)
            acc_ref[g] = alpha * acc_ref[g] + jnp.dot(p.astype(BF16), vg, preferred_element_type=F32)
            m_ref[g] = m_new
        return carry

    lax.fori_loop(0, nkb, attend, 0)

    heads = []
    for g in range(N_KV_HEADS):
        og = acc_ref[g] / l_ref[g]
        heads += [og[r * tq:(r + 1) * tq] for r in range(KV_REP)]
    o_ref[...] = _rms(jnp.concatenate(heads, axis=1), g_ref[...]).astype(BF16)


def _dsa(q, qi, wi, k, v, ki, g_attn):
    bsz, seq, _ = q.shape
    tq = min(ATT_TILE, seq)
    nq = seq // tq
    topk = min(TOPK_MAX, seq // 4)
    nbits = int(seq).bit_length()
    qblk = lambda w: pl.BlockSpec((None, tq, w), lambda b, i: (b, i, 0))
    full = lambda w: pl.BlockSpec((None, seq, w), lambda b, i: (b, 0, 0), pipeline_mode=pl.Buffered(1))
    return pl.pallas_call(
        functools.partial(_dsa_body, topk, nbits),
        grid=(bsz, nq),
        in_specs=[qblk(D_ATT), qblk(D_QI), qblk(LANES), full(D_KV), full(D_KV), full(LANES),
                  pl.BlockSpec((1, D_ATT), lambda b, i: (0, 0))],
        out_specs=qblk(D_ATT),
        out_shape=jax.ShapeDtypeStruct((bsz, seq, D_ATT), BF16),
        scratch_shapes=[pltpu.VMEM((nq, tq, tq), F32),
                        pltpu.VMEM((nq, tq, tq), F32),
                        pltpu.VMEM((IDX_HEADS * tq, LANES), BF16),
                        pltpu.VMEM((IDX_HEADS * tq, LANES), F32),
                        pltpu.VMEM((N_KV_HEADS, KV_REP * tq, LANES), F32),
                        pltpu.VMEM((N_KV_HEADS, KV_REP * tq, LANES), F32),
                        pltpu.VMEM((N_KV_HEADS, KV_REP * tq, LANES), F32)],
        compiler_params=_params("arbitrary", "arbitrary"),
        name="dsa",
    )(q, qi, wi, k, v, ki, g_attn)


def _split_bf16(x):
    hi = x.astype(BF16)
    return hi, (x - hi.astype(F32)).astype(BF16)


def _out_proj_body(yc_ref, ya_ref, x_ref, ada_ref, wo_ref, gpm_ref, gpf_ref, wr_ref, br_ref,
                   x1_ref, h2_ref, idx_ref, gate_ref):
    tm = x_ref.shape[0]
    mix = (jnp.dot(yc_ref[...], wo_ref[0:D_CONV, :], preferred_element_type=F32)
           + jnp.dot(ya_ref[...], wo_ref[D_CONV:, :], preferred_element_type=F32))
    x1 = x_ref[...] + ada_ref[2:3, :] * _rms(mix, gpm_ref[...])
    x1_ref[...] = x1
    h2 = _rms(x1, gpf_ref[...]) * (1.0 + ada_ref[4:5, :]) + ada_ref[3:4, :]
    h2_ref[...] = h2

    h_hi, h_lo = _split_bf16(h2)
    w_hi, w_lo = _split_bf16(wr_ref[...])
    logits = (jnp.dot(h_hi, w_hi, preferred_element_type=F32) + jnp.dot(h_hi, w_lo, preferred_element_type=F32)
              + jnp.dot(h_lo, w_hi, preferred_element_type=F32)) + br_ref[...]
    lane = lax.broadcasted_iota(I32, (tm, LANES), 1).astype(F32)
    cur = jnp.where(lane < N_EXPERTS, logits, -jnp.inf)
    vals, idxs = [], []
    for _ in range(TOP_K):
        m = jnp.max(cur, axis=1, keepdims=True)
        am = jnp.min(jnp.where(cur == m, lane, float(LANES)), axis=1, keepdims=True)
        vals.append(m)
        idxs.append(am)
        cur = jnp.where(lane == am, -jnp.inf, cur)
    es = [jnp.exp(vv - vals[0]) for vv in vals]
    tot = es[0] + es[1] + es[2] + es[3]
    idx_out = jnp.zeros((tm, LANES), I32)
    gate_out = jnp.zeros((tm, LANES), F32)
    for kk in range(TOP_K):
        idx_out = jnp.where(lane == kk, idxs[kk].astype(I32), idx_out)
        gate_out = jnp.where(lane == kk, es[kk] / tot, gate_out)
    idx_ref[...] = idx_out
    gate_ref[...] = gate_out


def _out_proj(yc, ya, x2, ada3, w_out, g_post_mix, g_pre_ffn, w_router, b_router, seq):
    n, d = x2.shape
    tm = min(ROW_TILE, seq)
    tiles_per_batch = seq // tm
    row = lambda i: (i, 0)
    fixed = lambda i: (0, 0)
    return pl.pallas_call(
        _out_proj_body,
        grid=(n // tm,),
        in_specs=[pl.BlockSpec((tm, D_CONV), row),
                  pl.BlockSpec((tm, D_ATT), row),
                  pl.BlockSpec((tm, d), row),
                  pl.BlockSpec((None, 6, d), lambda i: (i // tiles_per_batch, 0, 0)),
                  pl.BlockSpec((D_CONV + D_ATT, d), fixed, pipeline_mode=pl.Buffered(1)),
                  pl.BlockSpec((1, d), fixed),
                  pl.BlockSpec((1, d), fixed),
                  pl.BlockSpec((d, LANES), fixed),
                  pl.BlockSpec((1, LANES), fixed)],
        out_specs=[pl.BlockSpec((tm, d), row), pl.BlockSpec((tm, d), row),
                   pl.BlockSpec((tm, LANES), row), pl.BlockSpec((tm, LANES), row)],
        out_shape=[jax.ShapeDtypeStruct((n, d), F32), jax.ShapeDtypeStruct((n, d), F32),
                   jax.ShapeDtypeStruct((n, LANES), I32), jax.ShapeDtypeStruct((n, LANES), F32)],
        compiler_params=_params("arbitrary"),
        name="out_proj",
    )(yc, ya, x2, ada3, w_out, g_post_mix, g_pre_ffn, w_router, b_router)


def _rank_body(idx_ref, rank_ref, cnt_ref, carry):
    i = pl.program_id(0)
    tm = idx_ref.shape[0]

    @pl.when(i == 0)
    def _():
        carry[...] = jnp.zeros(carry.shape, F32)

    lane = lax.broadcasted_iota(I32, (tm, LANES), 1)
    r = lax.broadcasted_iota(I32, (tm, tm), 0)
    c = lax.broadcasted_iota(I32, (tm, tm), 1)
    before = jnp.where(c < r, 1.0, 0.0).astype(BF16)
    idx = idx_ref[...]
    base = carry[0:1, :]
    out = jnp.zeros((tm, LANES), F32)
    for kk in range(TOP_K):
        onehot = jnp.where(lane == idx[:, kk:kk + 1], 1.0, 0.0)
        prefix = jnp.dot(before, onehot.astype(BF16), preferred_element_type=F32) + base
        rk = jnp.sum(onehot * prefix, axis=1, keepdims=True)
        out = jnp.where(lane == kk, rk, out)
        base = base + jnp.sum(onehot, axis=0, keepdims=True)
    rank_ref[...] = out.astype(I32)
    carry[0:1, :] = base
    cnt_ref[...] = jnp.broadcast_to(base, cnt_ref.shape).astype(I32)


def _rank(top_idx):
    n = top_idx.shape[0]
    tm = min(ROW_TILE, n)
    return pl.pallas_call(
        _rank_body,
        grid=(n // tm,),
        in_specs=[pl.BlockSpec((tm, LANES), lambda i: (i, 0))],
        out_specs=[pl.BlockSpec((tm, LANES), lambda i: (i, 0)), pl.BlockSpec((8, LANES), lambda i: (0, 0))],
        out_shape=[jax.ShapeDtypeStruct((n, LANES), I32), jax.ShapeDtypeStruct((8, LANES), I32)],
        scratch_shapes=[pltpu.VMEM((8, LANES), F32)],
        compiler_params=_params("arbitrary"),
        name="rank",
    )(top_idx)


def _dest_body(idx_ref, rank_ref, start_ref, o_ref):
    tm = idx_ref.shape[0]
    lane = lax.broadcasted_iota(I32, (tm, LANES), 1)
    idx = idx_ref[...]
    rank = rank_ref[...]
    out = jnp.zeros((tm, LANES), I32)
    for kk in range(TOP_K):
        start = jnp.sum(jnp.where(lane == idx[:, kk:kk + 1], start_ref[...], 0.0), axis=1, keepdims=True)
        out = jnp.where(lane == kk, start.astype(I32) + rank[:, kk:kk + 1], out)
    o_ref[...] = out


def _dest(top_idx, rank, starts):
    n = top_idx.shape[0]
    tm = min(2 * ROW_TILE, n)
    row = lambda i: (i, 0)
    return pl.pallas_call(
        _dest_body,
        grid=(n // tm,),
        in_specs=[pl.BlockSpec((tm, LANES), row), pl.BlockSpec((tm, LANES), row),
                  pl.BlockSpec((1, LANES), lambda i: (0, 0))],
        out_specs=pl.BlockSpec((tm, LANES), row),
        out_shape=jax.ShapeDtypeStruct((n, LANES), I32),
        compiler_params=_params("arbitrary"),
        name="dest",
    )(top_idx, rank, starts)


def _dispatch_body(dest_ref, valid_ref, h_ref, o_ref, zeros_ref, sem, zsem):
    i = pl.program_id(0)
    tm = h_ref.shape[0]
    blk = zeros_ref.shape[0]
    nb = o_ref.shape[0] // blk

    @pl.when(i == 0)
    def _():
        zeros_ref[...] = jnp.zeros(zeros_ref.shape, zeros_ref.dtype)

        def for_blocks(fn):
            for jb in range(nb):
                @pl.when(valid_ref[jb] < blk)
                def _():
                    fn(pltpu.make_async_copy(zeros_ref, o_ref.at[pl.ds(jb * blk, blk)], zsem))

        for_blocks(lambda cp: cp.start())
        for_blocks(lambda cp: cp.wait())

    def row_copy(r, kk):
        dst = dest_ref[(i * tm + r) * TOP_K + kk]
        return pltpu.make_async_copy(h_ref.at[pl.ds(r, 1)], o_ref.at[pl.ds(dst, 1)], sem)

    def issue(rb, carry):
        for jj in range(DMA_UNROLL):
            for kk in range(TOP_K):
                row_copy(rb * DMA_UNROLL + jj, kk).start(priority=kk % 2)
        return carry

    lax.fori_loop(0, tm // DMA_UNROLL, issue, 0)
    for kk in range(TOP_K):
        pltpu.make_async_copy(h_ref, o_ref.at[pl.ds(0, tm)], sem).wait()


def _dispatch(dest_flat, block_valid, h2p, n_rows):
    n, dw = h2p.shape
    tm = min(ROW_TILE, n)
    return pl.pallas_call(
        _dispatch_body,
        grid_spec=pltpu.PrefetchScalarGridSpec(
            num_scalar_prefetch=2,
            grid=(n // tm,),
            in_specs=[pl.BlockSpec((tm, dw), lambda i, dst, bv: (i, 0))],
            out_specs=pl.BlockSpec(memory_space=pl.ANY),
            scratch_shapes=[pltpu.VMEM((MOE_BLOCK, dw), h2p.dtype), pltpu.SemaphoreType.DMA,
                            pltpu.SemaphoreType.DMA]),
        out_shape=jax.ShapeDtypeStruct((n_rows, dw), h2p.dtype),
        compiler_params=_params("arbitrary"),
        name="dispatch",
    )(dest_flat, block_valid, h2p)


def _experts_body(be_ref, ns_ref, rev_ref, nu_ref, x_ref, wgu_ref, wd_ref, bgu_ref, bd_ref, y_ref, xb_ref):
    j = pl.program_id(0)
    f = pl.program_id(1)
    nsub = ns_ref[j]

    def compute(r0, r1, first):
        if first:
            xb = x_ref[r0:r1, :].astype(BF16)
            xb_ref[r0:r1, :] = xb
        else:
            xb = xb_ref[r0:r1, :]
        gu = jnp.dot(xb, wgu_ref[...].astype(BF16), preferred_element_type=F32) + bgu_ref[...]
        gate = jnp.minimum(gu, SWIGLU_LIMIT)
        glu = gate * jax.nn.sigmoid(SWIGLU_ALPHA * gate)
        up1 = jnp.clip(gu, -SWIGLU_LIMIT, SWIGLU_LIMIT) + 1.0
        n2 = gu.shape[1]
        rows2 = lax.broadcasted_iota(I32, (2 * LANES, LANES), 0)
        cols2 = lax.broadcasted_iota(I32, (2 * LANES, LANES), 1)
        sel = jnp.where(rows2 == 2 * cols2, 1.0, 0.0).astype(BF16)
        parts = []
        for c in range(n2 // LANES):
            sl = slice(c * LANES, (c + 1) * LANES)
            parts.append((glu[:, sl] * pltpu.roll(up1[:, sl], LANES - 1, 1)).astype(BF16))
        acts = [jnp.dot(jnp.concatenate(parts[2 * c:2 * c + 2], axis=1), sel, preferred_element_type=F32)
                for c in range(n2 // (2 * LANES))]
        act = jnp.concatenate(acts, axis=1).astype(BF16)
        down = jnp.dot(act, wd_ref[...].astype(BF16), preferred_element_type=F32)
        if first:
            y_ref[r0:r1, :] = down + bd_ref[...]
        else:
            y_ref[r0:r1, :] += down

    for ns in range(0, MOE_BLOCK // MOE_SUB + 1):
        @pl.when((nsub == ns) & (f == 0))
        def _():
            for r0 in range(0, ns * MOE_SUB, MOE_ROWS):
                compute(r0, min(r0 + MOE_ROWS, ns * MOE_SUB), True)
            if ns * MOE_SUB < MOE_BLOCK:
                y_ref[ns * MOE_SUB:, :] = jnp.broadcast_to(bd_ref[...], (MOE_BLOCK - ns * MOE_SUB, y_ref.shape[1]))

        if ns > 0:
            @pl.when((nsub == ns) & (f > 0))
            def _():
                for r0 in range(0, ns * MOE_SUB, MOE_ROWS):
                    compute(r0, min(r0 + MOE_ROWS, ns * MOE_SUB), False)


def _experts(block_e, block_nsub, block_rev, n_used, x_rows, w_gate_up, w_down, b_gate_up, b_down):
    p, d = x_rows.shape
    dw = d
    dff = w_down.shape[1]
    nb = p // MOE_BLOCK
    nf = dff // FF_TILE

    def blk(j, f, be, ns, rev, nu):
        return jnp.minimum(j, nu[0] - 1)

    def fidx(j, f, be, ns, rev, nu):
        back = rev[jnp.minimum(j, nu[0] - 1)] == 1
        return jnp.where(j < nu[0], jnp.where(back, nf - 1 - f, f), jnp.where(back, 0, nf - 1))

    return pl.pallas_call(
        _experts_body,
        grid_spec=pltpu.PrefetchScalarGridSpec(
            num_scalar_prefetch=4,
            grid=(nb, nf),
            in_specs=[pl.BlockSpec((MOE_BLOCK, dw), lambda j, f, *s: (blk(j, f, *s), 0)),
                      pl.BlockSpec((None, d, 2 * FF_TILE), lambda j, f, *s: (s[0][j], 0, fidx(j, f, *s))),
                      pl.BlockSpec((None, FF_TILE, d), lambda j, f, *s: (s[0][j], fidx(j, f, *s), 0)),
                      pl.BlockSpec((None, 1, 2 * FF_TILE), lambda j, f, *s: (s[0][j], 0, fidx(j, f, *s))),
                      pl.BlockSpec((None, 1, d), lambda j, f, *s: (s[0][j], 0, 0))],
            out_specs=pl.BlockSpec((MOE_BLOCK, d), lambda j, f, *s: (j, 0)),
            scratch_shapes=[pltpu.VMEM((MOE_BLOCK, d), BF16)]),
        out_shape=jax.ShapeDtypeStruct((p, d), F32),
        compiler_params=_params("arbitrary", "arbitrary"),
        name="experts",
    )(block_e, block_nsub, block_rev, n_used, x_rows, w_gate_up, w_down, b_gate_up, b_down)


def _combine_body(tiles_per_batch, dest_ref, y_ref, gate_ref, x1_ref, ada_ref, g_ref, o_ref, buf, sem):
    i = pl.program_id(0)
    n_tiles = pl.num_programs(0)
    tm = x1_ref.shape[0]

    def row_copy(tile, slot, r, kk):
        src = dest_ref[(tile * tm + r) * TOP_K + kk]
        return pltpu.make_async_copy(y_ref.at[pl.ds(src, 1)], buf.at[slot, kk, pl.ds(r, 1)], sem.at[slot])

    def issue(tile, slot):
        def body(rb, carry):
            for jj in range(DMA_UNROLL):
                for kk in range(TOP_K):
                    row_copy(tile, slot, rb * DMA_UNROLL + jj, kk).start()
            return carry
        lax.fori_loop(0, tm // DMA_UNROLL, body, 0)

    def drain(tile, slot):
        for kk in range(TOP_K):
            pltpu.make_async_copy(y_ref.at[pl.ds(0, tm)], buf.at[slot, kk], sem.at[slot]).wait()

    slot = i % 2

    @pl.when(i == 0)
    def _():
        issue(0, 0)

    drain(i, slot)
    nxt = jnp.minimum(i + 1, n_tiles - 1)
    for r in range(tm):
        for kk in range(TOP_K):
            row_copy(nxt, 1 - slot, r, kk).start(priority=kk % 2)
    gates = gate_ref[...]
    y = jnp.zeros(x1_ref.shape, F32)
    for kk in range(TOP_K):
        y = y + gates[:, kk:kk + 1] * buf[slot, kk]
    o_ref[...] = x1_ref[...] + ada_ref[5:6, :] * _rms(y, g_ref[...])

    @pl.when(i == n_tiles - 1)
    def _():
        drain(nxt, 1 - slot)


def _combine(dest_flat, y_rows, gates, x1, ada3, g_post_ffn, seq):
    n, d = x1.shape
    tm = min(GATHER_TILE, seq)
    tiles_per_batch = seq // tm
    return pl.pallas_call(
        functools.partial(_combine_body, tiles_per_batch),
        grid_spec=pltpu.PrefetchScalarGridSpec(
            num_scalar_prefetch=1,
            grid=(n // tm,),
            in_specs=[pl.BlockSpec(memory_space=pl.ANY),
                      pl.BlockSpec((tm, LANES), lambda i, dst: (i, 0)),
                      pl.BlockSpec((tm, d), lambda i, dst: (i, 0)),
                      pl.BlockSpec((None, 6, d), lambda i, dst: (i // tiles_per_batch, 0, 0)),
                      pl.BlockSpec((1, d), lambda i, dst: (0, 0))],
            out_specs=pl.BlockSpec((tm, d), lambda i, dst: (i, 0)),
            scratch_shapes=[pltpu.VMEM((2, TOP_K, tm, d), F32), pltpu.SemaphoreType.DMA((2,))]),
        out_shape=jax.ShapeDtypeStruct((n, d), F32),
        compiler_params=_params("arbitrary"),
        name="combine",
    )(dest_flat, y_rows, gates, x1, ada3, g_post_ffn)


def _tail_w_in(w_in):
    d = w_in.shape[0]
    o_wi = C_KI + IDX_DIM
    ki = w_in[:, C_KI:o_wi]
    wi = w_in[:, o_wi:o_wi + IDX_HEADS]
    pad = jnp.zeros((d, LANES - IDX_HEADS), w_in.dtype)
    return jnp.concatenate([ki, ki, wi, pad], axis=1)


def _layer(x, c, positions, w_ada, b_ada, g_pre_mix, g_post_mix, w_in, conv_w, g_conv_out, g_attn_out, w_out,
           g_pre_ffn, g_post_ffn, w_router, b_router, w_gate_up, b_gate_up, w_down, b_down):
    bsz, seq, d = x.shape
    n = bsz * seq
    x2 = x.reshape(n, d)
    ada3 = _ada(c, w_ada, b_ada).reshape(bsz, 6, d)
    w_in_b = w_in.astype(BF16)

    yc, q, k, v, qi, ki, wi = _in_proj(
        x2, ada3, g_pre_mix.reshape(1, d), positions.reshape(n, 1).astype(I32), w_in_b, _tail_w_in(w_in_b),
        conv_w.reshape(CONV_WIDTH, D_CONV), g_conv_out.reshape(1, D_CONV), seq)
    b3 = lambda a: a.reshape(bsz, seq, a.shape[-1])
    ya = _dsa(b3(q), b3(qi), b3(wi), b3(k), b3(v), b3(ki), g_attn_out.reshape(1, D_ATT)).reshape(n, D_ATT)

    wr = jnp.zeros((d, LANES), F32).at[:, :N_EXPERTS].set(w_router)
    br = jnp.zeros((1, LANES), F32).at[0, :N_EXPERTS].set(b_router)
    x1, h2, top_idx, gates = _out_proj(yc, ya, x2, ada3, w_out.astype(BF16), g_post_mix.reshape(1, d),
                                       g_pre_ffn.reshape(1, d), wr, br, seq)

    rank, cnt = _rank(top_idx)
    counts = cnt[0, :N_EXPERTS]
    padded = (counts + MOE_BLOCK - 1) // MOE_BLOCK * MOE_BLOCK
    pad_ends = jnp.cumsum(padded)
    pad_starts = pad_ends - padded
    nb = -(-(n * TOP_K) // MOE_BLOCK) + N_EXPERTS
    starts_row = jnp.zeros((1, LANES), F32).at[0, :N_EXPERTS].set(pad_starts.astype(F32))
    dest = _dest(top_idx, rank, starts_row)[:, :TOP_K].reshape(n * TOP_K)
    block_start = jnp.arange(nb, dtype=I32) * MOE_BLOCK
    block_e = jnp.minimum(jnp.sum((pad_ends[None, :] <= block_start[:, None]).astype(I32), axis=1),
                          N_EXPERTS - 1)
    n_used = (pad_ends[-1:] // MOE_BLOCK).astype(I32)
    block_valid = jnp.clip(counts[block_e] - (block_start - pad_starts[block_e]), 0, MOE_BLOCK)
    block_valid = jnp.where(block_start < pad_ends[-1], block_valid, 0).astype(I32)
    block_nsub = (block_valid + MOE_SUB - 1) // MOE_SUB
    block_e = jnp.where(block_start < pad_ends[-1], block_e, block_e[n_used[0] - 1])
    block_rev = ((block_start - pad_starts[block_e]) // MOE_BLOCK) % 2

    x_rows = _dispatch(dest, block_valid, h2, nb * MOE_BLOCK)
    y_rows = _experts(block_e, block_nsub, block_rev.astype(I32), n_used, x_rows, w_gate_up, w_down,
                      b_gate_up[:, None, :], b_down[:, None, :])
    out = _combine(dest, y_rows, gates, x1, ada3, g_post_ffn.reshape(1, d), seq)
    return out.reshape(bsz, seq, d)


def kernel(x, c, positions, w_ada, b_ada, g_pre_mix, g_post_mix, w_in, conv_w, g_conv_out, g_attn_out, w_out,
           g_pre_ffn, g_post_ffn, w_router, b_router, w_gate_up, b_gate_up, w_down, b_down):
    for l in range(w_ada.shape[0]):
        x = _layer(x, c, positions, w_ada[l], b_ada[l], g_pre_mix[l], g_post_mix[l], w_in[l], conv_w[l],
                   g_conv_out[l], g_attn_out[l], w_out[l], g_pre_ffn[l], g_post_ffn[l], w_router[l], b_router[l],
                   w_gate_up[l], b_gate_up[l], w_down[l], b_down[l])
    return x
```
